```python
import jax, jax.numpy as jnp
from jax import lax
import numpy as np

D_MODEL = 1024
BATCH = 2
SEQ = 8192
DEPTH = 1
DEC_BATCH = 128
DEC_SEQ = 4
PAST_LEN = 8192
PAGE_SIZE = 128

HEAD_DIM = 64
N_HEADS = D_MODEL // HEAD_DIM
NSA_HEADS = N_HEADS // 2
SB_HEADS = N_HEADS - NSA_HEADS
NSA_KV_GROUPS = 2
SB_KV_HEADS = 4
NSA_WIDTH = NSA_HEADS * HEAD_DIM
SB_WIDTH = SB_HEADS * HEAD_DIM
NSA_KV_WIDTH = NSA_KV_GROUPS * HEAD_DIM
SB_KV_WIDTH = SB_KV_HEADS * HEAD_DIM
N_BRANCH = 3
CMP_LEN = 32
CMP_STRIDE = 16
CMP_HIDDEN = 128
SEL_BLOCK = 64
SEL_TOPK = 16
WINDOW = 512
Q_BLOCK = 128
RMS_EPS = 1e-6
BIG = 1e30
SPLIT_WIDTHS = (NSA_WIDTH, NSA_KV_WIDTH, NSA_KV_WIDTH, NSA_KV_WIDTH, NSA_KV_WIDTH, NSA_KV_WIDTH, NSA_KV_WIDTH,
                N_BRANCH * NSA_HEADS, NSA_WIDTH, SB_WIDTH, SB_KV_WIDTH, SB_KV_WIDTH, SB_WIDTH)
IN_WIDTH = 2 * NSA_WIDTH + 6 * NSA_KV_WIDTH + N_BRANCH * NSA_HEADS + 2 * SB_WIDTH + 2 * SB_KV_WIDTH

kernel_name = 'hymba_nsa_stickbreaking_step'


def rms_norm(x, g):
    xf = x.astype(jnp.float32)
    y = xf * lax.rsqrt(jnp.mean(xf * xf, axis=-1, keepdims=True) + RMS_EPS)
    return (y * g.astype(jnp.float32)).astype(x.dtype)


def alibi_slopes():
    return jnp.asarray(2.0 ** (-8.0 * np.arange(1, NSA_HEADS + 1) / NSA_HEADS), dtype=jnp.float32)


def masked_softmax(s, mask, axes):
    s = jnp.where(mask, s, -BIG)
    m = jnp.max(s, axis=axes, keepdims=True)
    e = jnp.where(mask, jnp.exp(s - m), 0.0)
    return e / jnp.maximum(jnp.sum(e, axis=axes, keepdims=True), 1e-30)


def cmp_to_sel(nc, ns):
    cs = np.arange(nc)[:, None] * CMP_STRIDE
    js = np.arange(ns)[None, :] * SEL_BLOCK
    return jnp.asarray((cs < js + SEL_BLOCK) & (cs + CMP_LEN > js), dtype=jnp.float32)


def compress(kraw, pe, w1, b1, w2):
    B, Tk, G, D = kraw.shape
    r = CMP_LEN // CMP_STRIDE
    n_sub = Tk // CMP_STRIDE
    nc = n_sub - r + 1
    sub = kraw[:, :n_sub * CMP_STRIDE].reshape(B, n_sub, CMP_STRIDE, G, D)
    sub = sub.transpose(0, 1, 3, 2, 4).reshape(B, n_sub, G, CMP_STRIDE * D)
    proj = jnp.einsum('bngc,rch->rbngh', sub, w1.reshape(r, CMP_STRIDE * D, CMP_HIDDEN))
    h = proj[0, :, 0:nc]
    for i in range(1, r):
        h = h + proj[i, :, i:i + nc]
    h = h + (pe.reshape(-1) @ w1 + b1)
    return jax.nn.silu(h) @ w2


def gather_pages(pool, table):
    g = pool[table]
    return g.reshape((table.shape[0], table.shape[1] * pool.shape[1]) + pool.shape[2:])


def project(x, norm_g, w_in, q_norm_g, k_norm_g):
    B, T, _ = x.shape
    h = rms_norm(x, norm_g) @ w_in
    offs = np.cumsum(SPLIT_WIDTHS)[:-1].tolist()
    qa, kc, vc, ks, vs, kw, vw, gt, za, qs, kb, vb, zb = jnp.split(h, offs, axis=-1)
    heads = lambda t, n: t.reshape(B, T, n, HEAD_DIM)
    q_a = rms_norm(heads(qa, NSA_HEADS), q_norm_g)
    cmp_kv = jnp.stack([heads(kc, NSA_KV_GROUPS), heads(vc, NSA_KV_GROUPS)], axis=2)
    slc_kv = jnp.stack([rms_norm(heads(ks, NSA_KV_GROUPS), k_norm_g[1]), heads(vs, NSA_KV_GROUPS)], axis=2)
    win_kv = jnp.stack([rms_norm(heads(kw, NSA_KV_GROUPS), k_norm_g[2]), heads(vw, NSA_KV_GROUPS)], axis=2)
    sb_kv = jnp.stack([heads(kb, SB_KV_HEADS), heads(vb, SB_KV_HEADS)], axis=2)
    gates = jax.nn.sigmoid(gt.astype(jnp.float32)).reshape(B, T, N_BRANCH, NSA_HEADS)
    return (q_a, heads(qs, SB_HEADS), gates, za, zb), (cmp_kv, slc_kv, win_kv, sb_kv)


def mix_tokens(q_a, q_b, cmp_kv, slc_kv, win_kv, win_start, sb_kv, off,
               cmp_pe, cmp_w1, cmp_b1, cmp_w2, k_norm_cmp):
    B, T = q_a.shape[0], q_a.shape[1]
    Tk = cmp_kv.shape[1]
    G, HPG, D = NSA_KV_GROUPS, NSA_HEADS // NSA_KV_GROUPS, HEAD_DIM
    R = SB_HEADS // SB_KV_HEADS
    scale = HEAD_DIM ** -0.5
    kc = rms_norm(compress(cmp_kv[:, :, 0], cmp_pe[0], cmp_w1[0], cmp_b1[0], cmp_w2[0]), k_norm_cmp)
    vc = compress(cmp_kv[:, :, 1], cmp_pe[1], cmp_w1[1], cmp_b1[1], cmp_w2[1])
    nc = kc.shape[1]
    cmp_end = jnp.arange(nc) * CMP_STRIDE + (CMP_LEN - 1)
    ns = -(-Tk // SEL_BLOCK)
    slc = jnp.pad(slc_kv, ((0, 0), (0, ns * SEL_BLOCK - Tk), (0, 0), (0, 0), (0, 0)))
    slc = slc.reshape(B, ns, SEL_BLOCK, 2, G, D).transpose(3, 0, 4, 1, 2, 5)
    k_blk, v_blk = slc[0], slc[1]
    sel_map = cmp_to_sel(nc, ns)
    n_top = min(SEL_TOPK, ns)
    jblk = jnp.arange(ns)
    win_pad = jnp.pad(win_kv, ((0, 0), (WINDOW, 0), (0, 0), (0, 0), (0, 0)))
    k_sb, v_sb = sb_kv[:, :, 0], sb_kv[:, :, 1]
    sb_kpos = jnp.arange(Tk)
    slopes = alibi_slopes().reshape(G, HPG)
    qb = min(Q_BLOCK, T)
    nb = T // qb
    bi = jnp.arange(B)[:, None, None, None]
    gi = jnp.arange(G)[None, :, None, None]

    def block(i):
        q0 = i * qb
        q0pos = off + q0
        qpos = q0pos + jnp.arange(qb)
        qg = lax.dynamic_slice_in_dim(q_a, q0, qb, axis=1).reshape(B, qb, G, HPG, D)
        s = jnp.einsum('bqghd,bngd->bghqn', qg, kc).astype(jnp.float32) * scale
        s = s - slopes[:, :, None, None] * (qpos[:, None] - cmp_end[None, :]).astype(jnp.float32)
        p_cmp = masked_softmax(s, cmp_end[None, :] <= qpos[:, None], (-1,))
        o_cmp = jnp.einsum('bghqn,bngd->bqghd', p_cmp, vc)
        imp = jnp.einsum('bghqn,nj->bgqj', p_cmp, sel_map)
        cur = qpos // SEL_BLOCK
        forced = (jblk[None, :] == 0) | (jblk[None, :] == cur[:, None]) | (jblk[None, :] == cur[:, None] - 1)
        imp = jnp.where(forced, BIG, jnp.where(jblk[None, :] <= cur[:, None], imp, -BIG))
        _, idx = lax.top_k(imp, n_top)
        ks = k_blk[bi, gi, idx]
        vs = v_blk[bi, gi, idx]
        kpos = idx[..., None] * SEL_BLOCK + jnp.arange(SEL_BLOCK)
        rel = (qpos[None, None, :, None, None] - kpos)
        s = jnp.einsum('bqghd,bgqksd->bghqks', qg, ks).astype(jnp.float32) * scale
        s = s - slopes[None, :, :, None, None, None] * rel[:, :, None].astype(jnp.float32)
        p = masked_softmax(s, (rel >= 0)[:, :, None], (-2, -1))
        o_slc = jnp.einsum('bghqks,bgqksd->bqghd', p, vs)
        kwv = lax.dynamic_slice_in_dim(win_pad, q0pos - win_start, WINDOW + qb, axis=1)
        kpos_w = q0pos - WINDOW + jnp.arange(WINDOW + qb)
        rel_w = qpos[:, None] - kpos_w[None, :]
        mask_w = (rel_w >= 0) & (rel_w < WINDOW) & (kpos_w[None, :] >= win_start)
        s = jnp.einsum('bqghd,bkgd->bghqk', qg, kwv[:, :, 0]).astype(jnp.float32) * scale
        s = s - slopes[:, :, None, None] * rel_w.astype(jnp.float32)
        p = masked_softmax(s, mask_w, (-1,))
        o_win = jnp.einsum('bghqk,bkgd->bqghd', p, kwv[:, :, 1])
        qs = lax.dynamic_slice_in_dim(q_b, q0, qb, axis=1).reshape(B, qb, SB_KV_HEADS, R, D)
        z = jnp.einsum('bqkrd,bskd->bkrqs', qs, k_sb).astype(jnp.float32) * scale
        causal = sb_kpos[None, :] < qpos[:, None]
        log1m = jnp.where(causal, jax.nn.log_sigmoid(-z), 0.0)
        suffix = lax.cumsum(log1m, axis=4, reverse=True) - log1m
        a = jnp.where(causal, jnp.exp(jax.nn.log_sigmoid(z) + suffix), 0.0)
        o_sb = jnp.einsum('bkrqs,bskd->bqkrd', a, v_sb)
        return (o_cmp.reshape(B, qb, NSA_HEADS, D), o_slc.reshape(B, qb, NSA_HEADS, D),
                o_win.reshape(B, qb, NSA_HEADS, D), o_sb.reshape(B, qb, SB_HEADS, D))

    outs = lax.map(block, jnp.arange(nb))
    unblock = lambda o: o.transpose(1, 0, 2, 3, 4).reshape(B, T, o.shape[3], D)
    return (unblock(outs[0]), unblock(outs[1]), unblock(outs[2]), unblock(outs[3]))


def layer_output(x, proj, cmp_full, slc_full, win_full, win_start, sb_full, off,
                 k_norm_g, cmp_pe, cmp_w1, cmp_b1, cmp_w2, w_out):
    q_a, q_b, gates, z_a, z_b = proj
    B, T, _ = x.shape
    o_cmp, o_slc, o_win, o_sb = mix_tokens(q_a, q_b, cmp_full, slc_full, win_full, win_start, sb_full, off,
                                           cmp_pe, cmp_w1, cmp_b1, cmp_w2, k_norm_g[0])
    o_a = gates[:, :, 0, :, None] * o_cmp + gates[:, :, 1, :, None] * o_slc + gates[:, :, 2, :, None] * o_win
    mixed = jnp.concatenate([o_a.reshape(B, T, NSA_WIDTH) * jax.nn.silu(z_a),
                             o_sb.reshape(B, T, SB_WIDTH) * jax.nn.silu(z_b)], axis=-1)
    return x + mixed.astype(x.dtype) @ w_out


def setup_inputs(seed: int = 0) -> dict:
    key = jax.random.key(seed)
    k = jax.random.split(key, 18)
    n_pages = PAST_LEN // PAGE_SIZE
    n_used = DEC_BATCH * n_pages
    n_pool = n_used + n_used // 4 + 1
    perm = jax.random.permutation(k[0], n_pool)
    page_table = perm[:n_used].reshape(DEC_BATCH, n_pages).astype(jnp.int32)
    wb = min(WINDOW, PAST_LEN)
    nrm = lambda kk, shape: jax.random.normal(kk, shape, jnp.float32)
    return {
        'x_prompt': nrm(k[1], (BATCH, SEQ, D_MODEL)),
        'x_sample': nrm(k[2], (DEC_BATCH, DEC_SEQ, D_MODEL)),
        'cache_cmp_kv': nrm(k[3], (DEPTH, n_pool, PAGE_SIZE, 2, NSA_KV_GROUPS, HEAD_DIM)),
        'cache_slc_kv': nrm(k[4], (DEPTH, n_pool, PAGE_SIZE, 2, NSA_KV_GROUPS, HEAD_DIM)),
        'cache_sb_kv': nrm(k[5], (DEPTH, n_pool, PAGE_SIZE, 2, SB_KV_HEADS, HEAD_DIM)),
        'state_win_kv': nrm(k[6], (DEPTH, DEC_BATCH, wb, 2, NSA_KV_GROUPS, HEAD_DIM)),
        'page_table': page_table,
        'norm_g': 1.0 + 0.02 * nrm(k[7], (DEPTH, D_MODEL)),
        'w_in': nrm(k[8], (DEPTH, D_MODEL, IN_WIDTH)) * D_MODEL ** -0.5,
        'q_norm_g': 1.0 + 0.02 * nrm(k[9], (DEPTH, HEAD_DIM)),
        'k_norm_g': 1.0 + 0.02 * nrm(k[10], (DEPTH, N_BRANCH, HEAD_DIM)),
        'cmp_pe': 0.5 * nrm(k[11], (DEPTH, 2, CMP_LEN, HEAD_DIM)),
        'cmp_w1': nrm(k[12], (DEPTH, 2, CMP_LEN * HEAD_DIM, CMP_HIDDEN)) * (CMP_LEN * HEAD_DIM) ** -0.5,
        'cmp_b1': 0.01 * nrm(k[13], (DEPTH, 2, CMP_HIDDEN)),
        'cmp_w2': nrm(k[14], (DEPTH, 2, CMP_HIDDEN, HEAD_DIM)) * CMP_HIDDEN ** -0.5,
        'w_out': nrm(k[15], (DEPTH, D_MODEL, D_MODEL)) * D_MODEL ** -0.5,
    }


def reference(x_prompt, x_sample, cache_cmp_kv, cache_slc_kv, cache_sb_kv, state_win_kv, page_table,
              norm_g, w_in, q_norm_g, k_norm_g, cmp_pe, cmp_w1, cmp_b1, cmp_w2, w_out):
    past_len = page_table.shape[1] * PAGE_SIZE
    wb = state_win_kv.shape[2]
    wp = min(WINDOW, x_prompt.shape[1])
    yp, ys = x_prompt, x_sample
    cmp_p, cmp_s, slc_p, slc_s, sb_p, sb_s, win_p, win_s = [], [], [], [], [], [], [], []
    for l in range(DEPTH):
        proj_p, (ckv_p, skv_p, wkv_p, bkv_p) = project(yp, norm_g[l], w_in[l], q_norm_g[l], k_norm_g[l])
        yp = layer_output(yp, proj_p, ckv_p, skv_p, wkv_p, 0, bkv_p, 0,
                          k_norm_g[l], cmp_pe[l], cmp_w1[l], cmp_b1[l], cmp_w2[l], w_out[l])
        proj_s, (ckv_s, skv_s, wkv_s, bkv_s) = project(ys, norm_g[l], w_in[l], q_norm_g[l], k_norm_g[l])
        cmp_full = jnp.concatenate([gather_pages(cache_cmp_kv[l], page_table), ckv_s], axis=1)
        slc_full = jnp.concatenate([gather_pages(cache_slc_kv[l], page_table), skv_s], axis=1)
        sb_full = jnp.concatenate([gather_pages(cache_sb_kv[l], page_table), bkv_s], axis=1)
        win_full = jnp.concatenate([state_win_kv[l], wkv_s], axis=1)
        ys = layer_output(ys, proj_s, cmp_full, slc_full, win_full, past_len - wb, sb_full, past_len,
                          k_norm_g[l], cmp_pe[l], cmp_w1[l], cmp_b1[l], cmp_w2[l], w_out[l])
        cmp_p.append(ckv_p); cmp_s.append(ckv_s)
        slc_p.append(skv_p); slc_s.append(skv_s)
        sb_p.append(bkv_p); sb_s.append(bkv_s)
        win_p.append(wkv_p[:, -wp:]); win_s.append(win_full[:, -wb:])
    return (yp, ys, jnp.stack(cmp_p), jnp.stack(cmp_s), jnp.stack(slc_p), jnp.stack(slc_s),
            jnp.stack(sb_p), jnp.stack(sb_s), jnp.stack(win_p), jnp.stack(win_s))
```

```python
import functools

import numpy as np
import jax
import jax.numpy as jnp
from jax import lax
from jax.experimental import pallas as pl
from jax.experimental.pallas import tpu as pltpu

F32 = jnp.float32
BF16 = jnp.bfloat16

HEAD_DIM = 64
NSA_HEADS = 8
SB_HEADS = 8
NSA_KV_GROUPS = 2
SB_KV_HEADS = 4
HEADS_PER_GROUP = NSA_HEADS // NSA_KV_GROUPS
N_BRANCH = 3
CMP_LEN = 32
CMP_STRIDE = 16
CMP_HIDDEN = 128
SEL_BLOCK = 64
SEL_TOPK = 16
WINDOW = 512
RMS_EPS = 1e-6
BIG = 1e30
PICKED = -3e38
SCALE = HEAD_DIM ** -0.5
SEL_SHIFT = 6
assert 1 << SEL_SHIFT == SEL_BLOCK

LANES = 128
SUBLANES = 8
N_SEL_LANES = 128
VMEM_LIMIT = 52 * 1024 * 1024

PERM_A = (0, 4, 1, 5, 2, 6, 3, 7)
PERM_B = (0, 2, 1, 3, 4, 6, 5, 7)


def _slope(h):
    return float(2.0 ** (-(h + 1)))


def _dot(a, b):
    return jnp.dot(a, b, preferred_element_type=F32)


def _dot_nt(a, b):
    return lax.dot_general(a, b, (((1,), (1,)), ((), ())), preferred_element_type=F32)


def _dot_split(x, m):
    hi = x.astype(BF16)
    lo = (x - hi.astype(F32)).astype(BF16)
    return _dot(hi, m) + _dot(lo, m)


def _softplus(z):
    return jnp.maximum(z, 0.0) + jnp.log1p(jnp.exp(-jnp.abs(z)))


def _silu(z):
    return z * jax.nn.sigmoid(z)


def _proj_kernel(x_ref, g_ref, w_ref, gq_ref, gks_ref, gkw_ref, seg_ref,
                 q_ref, cmp_ref, slc_ref, win_ref, slck_ref, slcv_ref, wink_ref, winv_ref,
                 gate_ref, sza_ref, qs_ref, sb_ref, sbk_ref, sbv_ref, szb_ref):
    x = x_ref[...]
    ms = jnp.mean(x * x, axis=-1, keepdims=True)
    hn = ((x * lax.rsqrt(ms + RMS_EPS)) * g_ref[...]).astype(BF16)
    seg = seg_ref[...]
    lane = lax.broadcasted_iota(jnp.int32, (x.shape[0], LANES), 1)
    low = lane < HEAD_DIM

    def sec(a, n=LANES):
        return _dot(hn, w_ref[:, a:a + n])

    def headnorm(t, gain):
        ss = _dot_split(t * t, seg) * (1.0 / HEAD_DIM)
        return (t * lax.rsqrt(ss + RMS_EPS)) * gain

    def variants(v):
        one = jnp.ones_like(v)
        return jnp.where(low, v, one).astype(BF16), jnp.where(low, one, v).astype(BF16)

    for c in range(4):
        t = headnorm(sec(c * LANES), gq_ref[...])
        q_ref[:, c * LANES:(c + 1) * LANES] = (t * SCALE).astype(BF16)
    base = 4 * LANES
    cmp_ref[:, 0:LANES] = sec(base)
    cmp_ref[:, LANES:2 * LANES] = sec(base + LANES)
    for k_out, v_out, kk_out, vv_out, gain_ref, off in (
            (slc_ref, slc_ref, slck_ref, slcv_ref, gks_ref, base + 2 * LANES),
            (win_ref, win_ref, wink_ref, winv_ref, gkw_ref, base + 4 * LANES)):
        kn = headnorm(sec(off), gain_ref[...])
        v = sec(off + LANES)
        k_out[:, 0:LANES] = kn
        v_out[:, LANES:2 * LANES] = v
        kk_out[...] = kn.astype(BF16)
        v0, v1 = variants(v)
        vv_out[:, 0:LANES] = v0
        vv_out[:, LANES:2 * LANES] = v1
    base = 10 * LANES
    for c in range(4):
        sza_ref[:, c * LANES:(c + 1) * LANES] = _silu(sec(base + c * LANES))
    base = 14 * LANES
    for c in range(4):
        qs_ref[:, c * LANES:(c + 1) * LANES] = (sec(base + c * LANES) * SCALE).astype(BF16)
    base = 18 * LANES
    for c in range(2):
        kb = sec(base + c * LANES)
        vb = sec(base + (2 + c) * LANES)
        sb_ref[:, c * LANES:(c + 1) * LANES] = kb
        sb_ref[:, (2 + c) * LANES:(3 + c) * LANES] = vb
        sbk_ref[:, c * LANES:(c + 1) * LANES] = kb.astype(BF16)
        sbv_ref[:, c * LANES:(c + 1) * LANES] = vb.astype(BF16)
    base = 22 * LANES
    for c in range(4):
        szb_ref[:, c * LANES:(c + 1) * LANES] = _silu(sec(base + c * LANES))
    gate_ref[...] = jax.nn.sigmoid(sec(26 * LANES))


def _proj(x2, g, w_perm, gq, gks, gkw, seg, tm):
    rows, d = x2.shape
    wcols = w_perm.shape[1]
    row = lambda i: (i, 0)
    const = lambda i: (0, 0)
    widths = [(512, BF16), (256, F32), (256, F32), (256, F32), (128, BF16), (256, BF16),
              (128, BF16), (256, BF16), (128, F32), (512, F32), (512, BF16), (512, F32),
              (256, BF16), (256, BF16), (512, F32)]
    return pl.pallas_call(
        _proj_kernel,
        grid=(rows // tm,),
        in_specs=[pl.BlockSpec((tm, d), row), pl.BlockSpec((1, d), const),
                  pl.BlockSpec((d, wcols), const), pl.BlockSpec((1, LANES), const),
                  pl.BlockSpec((1, LANES), const), pl.BlockSpec((1, LANES), const),
                  pl.BlockSpec((LANES, LANES), const)],
        out_specs=[pl.BlockSpec((tm, w), row) for w, _ in widths],
        out_shape=[jax.ShapeDtypeStruct((rows, w), dt) for w, dt in widths],
        compiler_params=pltpu.CompilerParams(dimension_semantics=("arbitrary",),
                                             vmem_limit_bytes=VMEM_LIMIT),
        name="proj",
    )(x2, g, w_perm, gq, gks, gkw, seg)


def _cbias_kernel(pe_ref, w1_ref, b1_ref, o_ref):
    for kv in range(2):
        pe = jnp.broadcast_to(pe_ref[kv], (SUBLANES, pe_ref.shape[-1])).astype(BF16)
        o_ref[kv] = _dot(pe, w1_ref[kv].astype(BF16)) + b1_ref[kv]


def _cbias(pe_flat, w1, b1):
    return pl.pallas_call(
        _cbias_kernel,
        out_shape=jax.ShapeDtypeStruct((2, SUBLANES, CMP_HIDDEN), F32),
        name="cbias",
    )(pe_flat, w1, b1)


def _compress_kernel(*refs, n_in, has_table):
    if has_table:
        refs = refs[1:]
    x_refs = refs[:n_in]
    if has_table:
        perm_ref = refs[n_in]
        refs = refs[1:]
    wc_ref, cb_ref, w2_ref, gk_ref, seg_ref = refs[n_in:n_in + 5]
    ck_ref, cv_ref = refs[n_in + 5:n_in + 7]
    carry_ref = refs[n_in + 7]
    n_rows = ck_ref.shape[0]

    @pl.when(pl.program_id(1) == 0)
    def _():
        carry_ref[...] = jnp.zeros(carry_ref.shape, F32)

    row = lax.broadcasted_iota(jnp.int32, (n_rows, CMP_HIDDEN), 0)
    lane = lax.broadcasted_iota(jnp.int32, (n_rows, LANES), 1)
    low = lane < HEAD_DIM
    for kv in range(2):
        if has_table:
            perm = perm_ref[...]
            pages = [_dot_nt(perm, r[0, kv * LANES:(kv + 1) * LANES, :].astype(BF16)) for r in x_refs]
        acc = jnp.zeros((n_rows, 4 * CMP_HIDDEN), F32)
        for s in range(CMP_STRIDE):
            if has_table:
                nb = LANES // CMP_STRIDE
                xk = jnp.concatenate([pg[s * nb:(s + 1) * nb] for pg in pages], axis=0)
            else:
                a = s * 2 * LANES + kv * LANES
                xk = x_refs[0][:, a:a + LANES]
            acc = acc + _dot(xk.astype(BF16), wc_ref[kv, s])
        cb = cb_ref[kv, 0:1, :]
        out = jnp.zeros((n_rows, LANES), F32)
        for g in range(NSA_KV_GROUPS):
            a0 = acc[:, (2 * g) * CMP_HIDDEN:(2 * g + 1) * CMP_HIDDEN]
            a1 = acc[:, (2 * g + 1) * CMP_HIDDEN:(2 * g + 2) * CMP_HIDDEN]
            prev = carry_ref[kv, SUBLANES - 1:SUBLANES, (2 * g) * CMP_HIDDEN:(2 * g + 1) * CMP_HIDDEN]
            shifted = jnp.where(row == 0, prev, pltpu.roll(a0, 1, axis=0))
            h = shifted + a1 + cb
            out = out + _dot(_silu(h).astype(BF16), w2_ref[kv, g])
        carry_ref[kv] = acc[n_rows - SUBLANES:, :]
        if kv == 0:
            ss = _dot_split(out * out, seg_ref[...]) * (1.0 / HEAD_DIM)
            ck_ref[...] = ((out * lax.rsqrt(ss + RMS_EPS)) * gk_ref[...]).astype(BF16)
        else:
            one = jnp.ones_like(out)
            cv_ref[:, 0:LANES] = jnp.where(low, out, one).astype(BF16)
            cv_ref[:, LANES:2 * LANES] = jnp.where(low, one, out).astype(BF16)


def _compress_common(n_out_rows, n_steps_total):
    out_shape = [jax.ShapeDtypeStruct((n_out_rows, LANES), BF16),
                 jax.ShapeDtypeStruct((n_out_rows, 2 * LANES), BF16)]
    return out_shape


def _compress_weight_specs(idx):
    return [pl.BlockSpec((2, CMP_STRIDE, LANES, 4 * CMP_HIDDEN), lambda *a: (0, 0, 0, 0)),
            pl.BlockSpec((2, SUBLANES, CMP_HIDDEN), lambda *a: (0, 0, 0)),
            pl.BlockSpec((2, NSA_KV_GROUPS, CMP_HIDDEN, LANES), lambda *a: (0, 0, 0, 0)),
            pl.BlockSpec((1, LANES), lambda *a: (0, 0)),
            pl.BlockSpec((LANES, LANES), lambda *a: (0, 0))]


def _compress_prompt(xsub, wc, cb, w2p, gk, seg, n_batch, rows_per_step):
    total = xsub.shape[0]
    per_b = total // n_batch
    nj = per_b // rows_per_step
    kern = functools.partial(_compress_kernel, n_in=1, has_table=False)
    return pl.pallas_call(
        kern,
        grid=(n_batch, nj),
        in_specs=[pl.BlockSpec((rows_per_step, xsub.shape[1]), lambda b, j: (b * nj + j, 0))]
        + _compress_weight_specs(None),
        out_specs=[pl.BlockSpec((rows_per_step, LANES), lambda b, j: (b * nj + j, 0)),
                   pl.BlockSpec((rows_per_step, 2 * LANES), lambda b, j: (b * nj + j, 0))],
        out_shape=_compress_common(total, None),
        scratch_shapes=[pltpu.VMEM((2, SUBLANES, 4 * CMP_HIDDEN), F32)],
        compiler_params=pltpu.CompilerParams(dimension_semantics=("arbitrary", "arbitrary"),
                                             vmem_limit_bytes=VMEM_LIMIT),
        name="compress_prompt",
    )(xsub, wc, cb, w2p, gk, seg)


def _compress_sample(table, pool_t, wc, cb, w2p, gk, seg, n_batch, n_pages, pages_per_step):
    page = pool_t.shape[2]
    sub_per_page = page // CMP_STRIDE
    nj = n_pages // pages_per_step
    rows_per_step = pages_per_step * sub_per_page
    total = n_batch * n_pages * sub_per_page
    kern = functools.partial(_compress_kernel, n_in=pages_per_step, has_table=True)
    rho = np.arange(page)
    perm = jnp.asarray(np.arange(page)[None, :] == (CMP_STRIDE * (rho % sub_per_page) + rho // sub_per_page)[:, None],
                       BF16)

    def page_spec(p):
        return pl.BlockSpec((1, pool_t.shape[1], page),
                            lambda b, j, t: (t[b * n_pages + j * pages_per_step + p], 0, 0))

    grid_spec = pltpu.PrefetchScalarGridSpec(
        num_scalar_prefetch=1,
        grid=(n_batch, nj),
        in_specs=[page_spec(p) for p in range(pages_per_step)]
        + [pl.BlockSpec((page, page), lambda b, j, t: (0, 0))] + _compress_weight_specs(None),
        out_specs=[pl.BlockSpec((rows_per_step, LANES), lambda b, j, t: (b * nj + j, 0)),
                   pl.BlockSpec((rows_per_step, 2 * LANES), lambda b, j, t: (b * nj + j, 0))],
        scratch_shapes=[pltpu.VMEM((2, SUBLANES, 4 * CMP_HIDDEN), F32)],
    )
    return pl.pallas_call(
        kern,
        grid_spec=grid_spec,
        out_shape=_compress_common(total, None),
        compiler_params=pltpu.CompilerParams(dimension_semantics=("arbitrary", "arbitrary"),
                                             vmem_limit_bytes=VMEM_LIMIT),
        name="compress_sample",
    )(table, *([pool_t] * pages_per_step), perm, wc, cb, w2p, gk, seg)


def _pick_blocks(v, lane_f, n_pick):
    sel = jnp.zeros(v.shape, F32)
    for _ in range(n_pick):
        mx = jnp.max(v, axis=1, keepdims=True)
        cand = jnp.where(v == mx, lane_f, float(2 * N_SEL_LANES))
        idx = jnp.min(cand, axis=1, keepdims=True)
        hit = lane_f == idx
        sel = jnp.where(hit, 1.0, sel)
        v = jnp.where(hit, PICKED, v)
    return sel


def _online_update(s, mask, m_old, acc_old, vt):
    s = jnp.where(mask, s, -BIG)
    m_new = jnp.maximum(m_old, jnp.max(s, axis=1, keepdims=True))
    p = jnp.where(mask, jnp.exp(s - m_new), 0.0)
    alpha = jnp.exp(m_old - m_new)
    return m_new, alpha * acc_old + _dot(p.astype(BF16), vt)


def _normalise(acc):
    return acc / jnp.maximum(pltpu.roll(acc, HEAD_DIM, axis=1), 1e-30)


def _sb_tile(z, causal, carry, u):
    lsn = -_softplus(z)
    if causal is not None:
        l1m = jnp.where(causal, lsn, 0.0)
    else:
        l1m = lsn
    suffix = _dot_split(l1m, u)
    a = jnp.exp((z + lsn) + suffix + carry)
    if causal is not None:
        a = jnp.where(causal, a, 0.0)
    return a, carry + jnp.sum(l1m, axis=1, keepdims=True)


def _nsa_prompt_kernel(q_ref, gate_ref, ck_ref, cv_ref, selmap_ref,
                       sk_ref, sv_ref, wk_ref, wv_ref, o_ref,
                       qm_s, sel_s, m_s, acc_s, oc_s):
    i = pl.program_id(1)
    nq = LANES
    q0 = i * nq
    lane = lax.broadcasted_iota(jnp.int32, (nq, LANES), 1)
    row = lax.broadcasted_iota(jnp.int32, (nq, LANES), 0)
    low = lane < HEAD_DIM
    qpos = q0 + row
    lane_f = lane.astype(F32)

    for g in range(NSA_KV_GROUPS):
        for c in range(HEADS_PER_GROUP):
            qc = q_ref[0, :, c * LANES:(c + 1) * LANES]
            keep = low if g == 0 else jnp.logical_not(low)
            qm_s[g * HEADS_PER_GROUP + c] = jnp.where(keep, qc, jnp.zeros_like(qc))

    n_c = ck_ref.shape[1]
    crow = lax.broadcasted_iota(jnp.int32, (nq, n_c), 1)
    cend = crow * CMP_STRIDE + (CMP_STRIDE - 1)
    cq = q0 + lax.broadcasted_iota(jnp.int32, (nq, n_c), 0)
    cmask = jnp.logical_and(cend <= cq, crow >= 1)
    crel = (cq - cend).astype(F32)
    ck = ck_ref[0]
    cur = qpos >> SEL_SHIFT
    forced = jnp.logical_or(lane == 0, jnp.logical_or(lane == cur, lane == cur - 1))
    allowed = lane <= cur
    for g in range(NSA_KV_GROUPS):
        imp = jnp.zeros((nq, N_SEL_LANES), F32)
        for c in range(HEADS_PER_GROUP):
            h = g * HEADS_PER_GROUP + c
            s = _dot_nt(qm_s[h], ck) - _slope(h) * crel
            s = jnp.where(cmask, s, -BIG)
            m = jnp.max(s, axis=1, keepdims=True)
            e = jnp.where(cmask, jnp.exp(s - m), 0.0)
            p = e / jnp.maximum(jnp.sum(e, axis=1, keepdims=True), 1e-30)
            oc = _dot(p.astype(BF16), cv_ref[0, :, g * LANES:(g + 1) * LANES])
            oc_s[h] = oc * gate_ref[0, :, h:h + 1]
            imp = imp + _dot_split(p, selmap_ref[...])
        v = jnp.where(forced, BIG, jnp.where(allowed, imp, -BIG))
        sel = _pick_blocks(v, lane_f, SEL_TOPK)
        sel_s[g] = jnp.where(allowed, sel, 0.0).astype(BF16)

    def flash_branch(k_ref, v_ref, lo, hi, banded, slot):
        for h in range(NSA_HEADS):
            m_s[slot + h] = jnp.full((nq, LANES), -BIG, F32)
            acc_s[slot + h] = jnp.zeros((nq, LANES), F32)

        def body(kb, carry):
            k0 = pl.multiple_of(kb * LANES, LANES)
            kt = k_ref[0, pl.ds(k0, LANES), :]
            rel = qpos - (k0 + lane)
            relf = rel.astype(F32)
            if banded:
                base_mask = jnp.logical_and(rel >= 0, rel < WINDOW)
            else:
                base_mask = rel >= 0
                blk = (LANES // SEL_BLOCK) * kb + (lane >> SEL_SHIFT)
                expand = jnp.where(row == blk, 1.0, 0.0).astype(BF16)
            for g in range(NSA_KV_GROUPS):
                if banded:
                    mask = base_mask
                else:
                    mask = jnp.logical_and(base_mask, _dot(sel_s[g], expand) > 0.5)
                vt = v_ref[0, pl.ds(k0, LANES), g * LANES:(g + 1) * LANES]
                for c in range(HEADS_PER_GROUP):
                    h = g * HEADS_PER_GROUP + c
                    s = _dot_nt(qm_s[h], kt) - _slope(h) * relf
                    m_new, acc_new = _online_update(s, mask, m_s[slot + h], acc_s[slot + h], vt)
                    m_s[slot + h] = m_new
                    acc_s[slot + h] = acc_new
            return carry

        lax.fori_loop(lo, hi, body, 0)

    flash_branch(sk_ref, sv_ref, 0, i + 1, False, 0)
    flash_branch(wk_ref, wv_ref, jnp.maximum(i - WINDOW // LANES, 0), i + 1, True, NSA_HEADS)

    for c in range(HEADS_PER_GROUP):
        tot = []
        for g in range(NSA_KV_GROUPS):
            h = g * HEADS_PER_GROUP + c
            o_slc = _normalise(acc_s[h]) * gate_ref[0, :, NSA_HEADS + h:NSA_HEADS + h + 1]
            o_win = _normalise(acc_s[NSA_HEADS + h]) * gate_ref[0, :, 2 * NSA_HEADS + h:2 * NSA_HEADS + h + 1]
            tot.append(oc_s[h] + o_slc + o_win)
        o_ref[0, :, c * LANES:(c + 1) * LANES] = jnp.where(low, tot[0], tot[1])


def _nsa_prompt(q, gates, ck, cv, selmap, sk, sv, wk, wv):
    n_b, t, _ = q.shape
    nqb = t // LANES
    n_c = ck.shape[1]
    per_q = lambda b, i: (b, i, 0)
    per_b = lambda b, i: (b, 0, 0)
    return pl.pallas_call(
        _nsa_prompt_kernel,
        grid=(n_b, nqb),
        in_specs=[pl.BlockSpec((1, LANES, 4 * LANES), per_q),
                  pl.BlockSpec((1, LANES, LANES), per_q),
                  pl.BlockSpec((1, n_c, LANES), per_b),
                  pl.BlockSpec((1, n_c, 2 * LANES), per_b),
                  pl.BlockSpec((n_c, N_SEL_LANES), lambda b, i: (0, 0)),
                  pl.BlockSpec((1, t, LANES), per_b),
                  pl.BlockSpec((1, t, 2 * LANES), per_b),
                  pl.BlockSpec((1, t, LANES), per_b),
                  pl.BlockSpec((1, t, 2 * LANES), per_b)],
        out_specs=pl.BlockSpec((1, LANES, 4 * LANES), per_q),
        out_shape=jax.ShapeDtypeStruct((n_b, t, 4 * LANES), F32),
        scratch_shapes=[pltpu.VMEM((NSA_HEADS, LANES, LANES), BF16),
                        pltpu.VMEM((NSA_KV_GROUPS, LANES, N_SEL_LANES), BF16),
                        pltpu.VMEM((2 * NSA_HEADS, LANES, LANES), F32),
                        pltpu.VMEM((2 * NSA_HEADS, LANES, LANES), F32),
                        pltpu.VMEM((NSA_HEADS, LANES, LANES), F32)],
        compiler_params=pltpu.CompilerParams(dimension_semantics=("arbitrary", "arbitrary"),
                                             vmem_limit_bytes=VMEM_LIMIT),
        name="nsa_prompt",
    )(q, gates, ck, cv, selmap, sk, sv, wk, wv)


def _sb_prompt_kernel(q_ref, k_ref, v_ref, u_ref, o_ref, qm_s, carry_s, acc_s):
    i = pl.program_id(2)
    nq = LANES
    lane = lax.broadcasted_iota(jnp.int32, (nq, LANES), 1)
    row = lax.broadcasted_iota(jnp.int32, (nq, LANES), 0)
    low = lane < HEAD_DIM
    qpos = i * nq + row
    for r in range(2):
        qc = q_ref[0, :, r * LANES:(r + 1) * LANES]
        zero = jnp.zeros_like(qc)
        qm_s[2 * r] = jnp.where(low, qc, zero)
        qm_s[2 * r + 1] = jnp.where(low, zero, qc)
    for h in range(4):
        carry_s[h] = jnp.zeros((nq, LANES), F32)
        acc_s[h] = jnp.zeros((nq, LANES), F32)
    u = u_ref[...]

    def body(step, c):
        kb = i - step
        k0 = pl.multiple_of(kb * LANES, LANES)
        kt = k_ref[0, pl.ds(k0, LANES), :]
        vt = v_ref[0, pl.ds(k0, LANES), :]
        causal = (k0 + lane) < qpos
        for h in range(4):
            z = _dot_nt(qm_s[h], kt)
            a, carry_new = _sb_tile(z, causal, carry_s[h], u)
            acc_s[h] = acc_s[h] + _dot(a.astype(BF16), vt)
            carry_s[h] = jnp.broadcast_to(carry_new, (nq, LANES))
        return c

    lax.fori_loop(0, i + 1, body, 0)
    for r in range(2):
        o_ref[0, :, r * LANES:(r + 1) * LANES] = jnp.where(low, acc_s[2 * r], acc_s[2 * r + 1])


def _sb_prompt(qs, sbk, sbv, u):
    n_b, t, _ = qs.shape
    nqb = t // LANES
    return pl.pallas_call(
        _sb_prompt_kernel,
        grid=(n_b, 2, nqb),
        in_specs=[pl.BlockSpec((1, LANES, 2 * LANES), lambda b, kp, i: (b, i, kp)),
                  pl.BlockSpec((1, t, LANES), lambda b, kp, i: (b, 0, kp)),
                  pl.BlockSpec((1, t, LANES), lambda b, kp, i: (b, 0, kp)),
                  pl.BlockSpec((LANES, LANES), lambda b, kp, i: (0, 0))],
        out_specs=pl.BlockSpec((1, LANES, 2 * LANES), lambda b, kp, i: (b, i, kp)),
        out_shape=jax.ShapeDtypeStruct((n_b, t, 4 * LANES), F32),
        scratch_shapes=[pltpu.VMEM((4, LANES, LANES), BF16),
                        pltpu.VMEM((4, LANES, LANES), F32),
                        pltpu.VMEM((4, LANES, LANES), F32)],
        compiler_params=pltpu.CompilerParams(
            dimension_semantics=("arbitrary", "arbitrary", "arbitrary"),
            vmem_limit_bytes=VMEM_LIMIT),
        name="sb_prompt",
    )(qs, sbk, sbv, u)


def _final_kernel(x_ref, oa_ref, sza_ref, ob_ref, szb_ref, w_ref, y_ref):
    half = oa_ref.shape[1]
    ma = (oa_ref[...] * sza_ref[...]).astype(BF16)
    mb = (ob_ref[...] * szb_ref[...]).astype(BF16)
    y_ref[...] = x_ref[...] + _dot(ma, w_ref[0:half, :]) + _dot(mb, w_ref[half:2 * half, :])


def _final(x2, oa, sza, ob, szb, w_out_perm, tm):
    rows, d = x2.shape
    half = oa.shape[1]
    row = lambda i: (i, 0)
    return pl.pallas_call(
        _final_kernel,
        grid=(rows // tm,),
        in_specs=[pl.BlockSpec((tm, d), row), pl.BlockSpec((tm, half), row),
                  pl.BlockSpec((tm, half), row), pl.BlockSpec((tm, half), row),
                  pl.BlockSpec((tm, half), row), pl.BlockSpec((d, d), lambda i: (0, 0))],
        out_specs=pl.BlockSpec((tm, d), row),
        out_shape=jax.ShapeDtypeStruct((rows, d), F32),
        compiler_params=pltpu.CompilerParams(dimension_semantics=("arbitrary",),
                                             vmem_limit_bytes=VMEM_LIMIT),
        name="final",
    )(x2, oa, sza, ob, szb, w_out_perm)


Q_PAD = SUBLANES
Q_SHIFT = 3
assert 1 << Q_SHIFT == Q_PAD


def _row_ids(shape):
    r = lax.broadcasted_iota(jnp.int32, shape, 0)
    return r >> Q_SHIFT, r & (Q_PAD - 1)


def _head_slopes(shape):
    hidx, _ = _row_ids(shape)
    out = jnp.zeros(shape, F32)
    for h in range(NSA_HEADS):
        out = jnp.where(hidx == h, _slope(h), out)
    return out


def _stack_nsa_queries(q_ref):
    lane = lax.broadcasted_iota(jnp.int32, (Q_PAD, LANES), 1)
    low = lane < HEAD_DIM
    parts = []
    for g in range(NSA_KV_GROUPS):
        for c in range(HEADS_PER_GROUP):
            qc = q_ref[0, :, c * LANES:(c + 1) * LANES]
            keep = low if g == 0 else jnp.logical_not(low)
            parts.append(jnp.where(keep, qc, 0.0))
    return jnp.concatenate(parts, axis=0).astype(BF16)


def _cmp_sample_kernel(q_ref, ck_ref, cv_ref, selmap_ref, eall_ref, ocmp_ref, mexp_ref,
                       *, past_len, n_pick):
    n_rows = NSA_HEADS * Q_PAD
    half = n_rows // 2
    qa = _stack_nsa_queries(q_ref)
    n_c = ck_ref.shape[0]
    crow = lax.broadcasted_iota(jnp.int32, (n_rows, n_c), 1)
    cend = crow * CMP_STRIDE + (CMP_STRIDE - 1)
    cq = past_len + _row_ids((n_rows, n_c))[1]
    cmask = jnp.logical_and(cend <= cq, crow >= 1)
    slope = _head_slopes((n_rows, n_c))
    s = _dot_nt(qa, ck_ref[...]) - slope * (cq - cend).astype(F32)
    s = jnp.where(cmask, s, -BIG)
    m = jnp.max(s, axis=1, keepdims=True)
    e = jnp.where(cmask, jnp.exp(s - m), 0.0)
    p = e / jnp.maximum(jnp.sum(e, axis=1, keepdims=True), 1e-30)
    pb = p.astype(BF16)
    ocmp_ref[0, 0:half, :] = _dot(pb[0:half], cv_ref[:, 0:LANES])
    ocmp_ref[0, half:n_rows, :] = _dot(pb[half:n_rows], cv_ref[:, LANES:2 * LANES])
    pg = []
    for g in range(NSA_KV_GROUPS):
        acc = jnp.zeros((Q_PAD, n_c), F32)
        for c in range(HEADS_PER_GROUP):
            r0 = (g * HEADS_PER_GROUP + c) * Q_PAD
            acc = acc + p[r0:r0 + Q_PAD]
        pg.append(acc)
    imp = _dot_split(jnp.concatenate(pg, axis=0), selmap_ref[...])
    n_past_blocks = past_len // SEL_BLOCK
    lane = lax.broadcasted_iota(jnp.int32, imp.shape, 1)
    forced = jnp.logical_or(lane == 0, lane == n_past_blocks - 1)
    allowed = lane < n_past_blocks
    v = jnp.where(allowed, jnp.where(forced, BIG, imp), -BIG)
    sel = _pick_blocks(v, lane.astype(F32), n_pick)
    sel = jnp.where(allowed, sel, 0.0).astype(BF16)
    mexp_ref[0] = _dot(sel, eall_ref[...])


def _cmp_sample(q8, ck, cv, selmap, eall, past_len, n_pick):
    n_b = q8.shape[0]
    n_c = ck.shape[0] // n_b
    n_rows = NSA_HEADS * Q_PAD
    kern = functools.partial(_cmp_sample_kernel, past_len=past_len, n_pick=n_pick)
    return pl.pallas_call(
        kern,
        grid=(n_b,),
        in_specs=[pl.BlockSpec((1, Q_PAD, 4 * LANES), lambda b: (b, 0, 0)),
                  pl.BlockSpec((n_c, LANES), lambda b: (b, 0)),
                  pl.BlockSpec((n_c, 2 * LANES), lambda b: (b, 0)),
                  pl.BlockSpec((n_c, N_SEL_LANES), lambda b: (0, 0)),
                  pl.BlockSpec((N_SEL_LANES, past_len), lambda b: (0, 0))],
        out_specs=[pl.BlockSpec((1, n_rows, LANES), lambda b: (b, 0, 0)),
                   pl.BlockSpec((1, 2 * Q_PAD, past_len), lambda b: (b, 0, 0))],
        out_shape=[jax.ShapeDtypeStruct((n_b, n_rows, LANES), F32),
                   jax.ShapeDtypeStruct((n_b, 2 * Q_PAD, past_len), F32)],
        compiler_params=pltpu.CompilerParams(dimension_semantics=("arbitrary",),
                                             vmem_limit_bytes=VMEM_LIMIT),
        name="cmp_sample",
    )(q8, ck, cv, selmap, eall)


def _pad_keys(x):
    return jnp.concatenate([x, jnp.zeros((LANES - Q_PAD, x.shape[1]), F32)], axis=0)


def _attn_sample_kernel(*refs, past_len, dec_seq, n_pages_step):
    refs = refs[1:]
    (qa_ref, qs_ref, gate_ref, ocmp_ref, mexp_ref, slcn_ref, winn_ref, sbn_ref,
     state_ref, u_ref) = refs[:10]
    slc_pages = refs[10:10 + n_pages_step]
    sb_pages = refs[10 + n_pages_step:10 + 2 * n_pages_step]
    oa_ref, ob_ref = refs[10 + 2 * n_pages_step:12 + 2 * n_pages_step]
    qa_s, qb_s, m_s, acc_s, win_s, carry_s, accb_s = refs[12 + 2 * n_pages_step:]

    j = pl.program_id(1)
    n_steps = pl.num_programs(1)
    group = n_steps - 1 - j
    n_rows = NSA_HEADS * Q_PAD
    half = n_rows // 2
    lane = lax.broadcasted_iota(jnp.int32, (n_rows, LANES), 1)
    low = lane < HEAD_DIM
    t_q = _row_ids((n_rows, LANES))[1]
    qpos = past_len + t_q
    slope = _head_slopes((n_rows, LANES))
    u = u_ref[...]

    def slc_step(k, v, kpos, mask, channel_major):
        axis = 0 if channel_major else 1
        first = lax.broadcasted_iota(jnp.int32, v.shape, axis) < HEAD_DIM
        one = jnp.ones_like(v)
        v0 = jnp.where(first, v, one).astype(BF16)
        v1 = jnp.where(first, one, v).astype(BF16)
        qk = _dot if channel_major else _dot_nt
        pv = _dot_nt if channel_major else _dot
        s = qk(qa_s[...], k) - slope * (qpos - kpos).astype(F32)
        s = jnp.where(mask, s, -BIG)
        m_old = m_s[...]
        m_new = jnp.maximum(m_old, jnp.max(s, axis=1, keepdims=True))
        p = jnp.where(mask, jnp.exp(s - m_new), 0.0).astype(BF16)
        alpha = jnp.exp(m_old - m_new)
        acc = acc_s[...]
        acc_s[0:half, :] = alpha[0:half] * acc[0:half] + pv(p[0:half], v0)
        acc_s[half:n_rows, :] = alpha[half:n_rows] * acc[half:n_rows] + pv(p[half:n_rows], v1)
        m_s[...] = m_new

    def sb_step(k, v, causal, channel_major):
        z = (_dot if channel_major else _dot_nt)(qb_s[...], k)
        a, carry_new = _sb_tile(z, causal, carry_s[...], u)
        accb_s[...] = accb_s[...] + (_dot_nt if channel_major else _dot)(a.astype(BF16), v)
        carry_s[...] = jnp.broadcast_to(carry_new, (n_rows, LANES))

    @pl.when(j == 0)
    def _():
        qa_s[...] = _stack_nsa_queries(qa_ref)
        lane8 = lax.broadcasted_iota(jnp.int32, (Q_PAD, LANES), 1)
        low8 = lane8 < HEAD_DIM
        parts = []
        for k in range(SB_KV_HEADS):
            for r in range(SB_HEADS // SB_KV_HEADS):
                ch = 2 * (k // 2) + r
                qc = qs_ref[0, :, ch * LANES:(ch + 1) * LANES]
                keep = low8 if k % 2 == 0 else jnp.logical_not(low8)
                piece = jnp.where(keep, qc, 0.0)
                zero = jnp.zeros_like(piece)
                parts.append(jnp.concatenate([piece, zero] if k // 2 == 0 else [zero, piece], axis=1))
        qb_s[...] = jnp.concatenate(parts, axis=0).astype(BF16)
        m_s[...] = jnp.full((n_rows, LANES), -BIG, F32)
        acc_s[...] = jnp.zeros((n_rows, LANES), F32)
        carry_s[...] = jnp.zeros((n_rows, LANES), F32)
        accb_s[...] = jnp.zeros((n_rows, 2 * LANES), F32)

        new_valid = lane < dec_seq
        slcn = _pad_keys(slcn_ref[0])
        slc_step(slcn[:, 0:LANES].astype(BF16), slcn[:, LANES:2 * LANES], past_len + lane,
                 jnp.logical_and(new_valid, lane <= t_q), False)
        sbn = _pad_keys(sbn_ref[0])
        sb_step(sbn[:, 0:2 * LANES].astype(BF16), sbn[:, 2 * LANES:4 * LANES].astype(BF16),
                jnp.logical_and(new_valid, lane < t_q), False)

        n_state = state_ref.shape[2]
        winn = _pad_keys(winn_ref[0])
        n_k = n_state + LANES
        col = lax.broadcasted_iota(jnp.int32, (n_rows, n_k), 1)
        kpos = jnp.where(col < n_state, past_len - n_state + col, past_len + col - n_state)
        wq = past_len + _row_ids((n_rows, n_k))[1]
        rel = wq - kpos
        wmask = jnp.logical_and(jnp.logical_and(rel >= 0, rel < WINDOW), col < n_state + dec_seq)
        qa = qa_s[...]
        qk = jnp.concatenate([_dot(qa, state_ref[0, 0:LANES, :].astype(BF16)),
                              _dot_nt(qa, winn[:, 0:LANES].astype(BF16))], axis=1)
        s = qk - _head_slopes((n_rows, n_k)) * rel.astype(F32)
        s = jnp.where(wmask, s, -BIG)
        m = jnp.max(s, axis=1, keepdims=True)
        e = jnp.where(wmask, jnp.exp(s - m), 0.0)
        p = (e / jnp.maximum(jnp.sum(e, axis=1, keepdims=True), 1e-30)).astype(BF16)
        win_s[...] = (_dot_nt(p[:, 0:n_state], state_ref[0, LANES:2 * LANES, :].astype(BF16))
                      + _dot(p[:, n_state:n_k], winn[:, LANES:2 * LANES].astype(BF16)))

    for pi in range(n_pages_step - 1, -1, -1):
        page = group * n_pages_step + pi
        k0 = page * LANES
        sp = slc_pages[pi]
        msk = mexp_ref[0, :, pi * LANES:(pi + 1) * LANES]
        mask = jnp.concatenate([msk[0:Q_PAD]] * HEADS_PER_GROUP + [msk[Q_PAD:2 * Q_PAD]] * HEADS_PER_GROUP,
                               axis=0) > 0.5
        slc_step(sp[0, 0:LANES, :].astype(BF16), sp[0, LANES:2 * LANES, :], k0 + lane, mask, True)
        bp = sb_pages[pi]
        sb_step(bp[0, 0:2 * LANES, :].astype(BF16), bp[0, 2 * LANES:4 * LANES, :].astype(BF16), None, True)

    @pl.when(j == n_steps - 1)
    def _():
        o_slc = _normalise(acc_s[...])
        o_win = win_s[...]
        o_cmp = ocmp_ref[0]
        for c in range(HEADS_PER_GROUP):
            tot = []
            for g in range(NSA_KV_GROUPS):
                h = g * HEADS_PER_GROUP + c
                r0 = h * Q_PAD
                gc = gate_ref[0, :, h:h + 1]
                gs = gate_ref[0, :, NSA_HEADS + h:NSA_HEADS + h + 1]
                gw = gate_ref[0, :, 2 * NSA_HEADS + h:2 * NSA_HEADS + h + 1]
                tot.append(gc * o_cmp[r0:r0 + Q_PAD] + gs * o_slc[r0:r0 + Q_PAD] + gw * o_win[r0:r0 + Q_PAD])
            oa_ref[0, :, c * LANES:(c + 1) * LANES] = jnp.where(low[0:Q_PAD], tot[0], tot[1])
        accb = accb_s[...]
        for kp in range(SB_KV_HEADS // 2):
            for r in range(SB_HEADS // SB_KV_HEADS):
                ra = ((2 * kp) * 2 + r) * Q_PAD
                rb = ((2 * kp + 1) * 2 + r) * Q_PAD
                a0 = accb[ra:ra + Q_PAD, kp * LANES:(kp + 1) * LANES]
                a1 = accb[rb:rb + Q_PAD, kp * LANES:(kp + 1) * LANES]
                ch = 2 * kp + r
                ob_ref[0, :, ch * LANES:(ch + 1) * LANES] = jnp.where(low[0:Q_PAD], a0, a1)


def _attn_sample(table, qa8, qs8, gate8, ocmp, mexp, slcn, winn, sbn, state, u,
                 slc_pool, sb_pool, n_pages, pages_per_step, past_len, dec_seq):
    n_b = qa8.shape[0]
    n_rows = NSA_HEADS * Q_PAD
    nj = n_pages // pages_per_step
    n_state = state.shape[2]
    kern = functools.partial(_attn_sample_kernel, past_len=past_len, dec_seq=dec_seq,
                             n_pages_step=pages_per_step)
    per_b = lambda b, j, t: (b, 0, 0)

    def page_spec(channels, p):
        return pl.BlockSpec(
            (1, channels, LANES),
            lambda b, j, t: (t[b * n_pages + (nj - 1 - j) * pages_per_step + p], 0, 0))

    grid_spec = pltpu.PrefetchScalarGridSpec(
        num_scalar_prefetch=1,
        grid=(n_b, nj),
        in_specs=[pl.BlockSpec((1, Q_PAD, 4 * LANES), per_b),
                  pl.BlockSpec((1, Q_PAD, 4 * LANES), per_b),
                  pl.BlockSpec((1, Q_PAD, LANES), per_b),
                  pl.BlockSpec((1, n_rows, LANES), per_b),
                  pl.BlockSpec((1, 2 * Q_PAD, pages_per_step * LANES), lambda b, j, t: (b, 0, nj - 1 - j)),
                  pl.BlockSpec((1, Q_PAD, 2 * LANES), per_b),
                  pl.BlockSpec((1, Q_PAD, 2 * LANES), per_b),
                  pl.BlockSpec((1, Q_PAD, 4 * LANES), per_b),
                  pl.BlockSpec((1, 2 * LANES, n_state), per_b),
                  pl.BlockSpec((LANES, LANES), lambda b, j, t: (0, 0))]
        + [page_spec(2 * LANES, p) for p in range(pages_per_step)]
        + [page_spec(4 * LANES, p) for p in range(pages_per_step)],
        out_specs=[pl.BlockSpec((1, Q_PAD, 4 * LANES), per_b),
                   pl.BlockSpec((1, Q_PAD, 4 * LANES), per_b)],
        scratch_shapes=[pltpu.VMEM((n_rows, LANES), BF16),
                        pltpu.VMEM((n_rows, 2 * LANES), BF16),
                        pltpu.VMEM((n_rows, LANES), F32),
                        pltpu.VMEM((n_rows, LANES), F32),
                        pltpu.VMEM((n_rows, LANES), F32),
                        pltpu.VMEM((n_rows, LANES), F32),
                        pltpu.VMEM((n_rows, 2 * LANES), F32)],
    )
    return pl.pallas_call(
        kern,
        grid_spec=grid_spec,
        out_shape=[jax.ShapeDtypeStruct((n_b, Q_PAD, 4 * LANES), F32),
                   jax.ShapeDtypeStruct((n_b, Q_PAD, 4 * LANES), F32)],
        compiler_params=pltpu.CompilerParams(dimension_semantics=("arbitrary", "arbitrary"),
                                             vmem_limit_bytes=VMEM_LIMIT),
        name="attn_sample",
    )(table, qa8, qs8, gate8, ocmp, mexp, slcn, winn, sbn, state, u,
      *([slc_pool] * pages_per_step), *([sb_pool] * pages_per_step))


def _head_index(base, perm):
    return np.concatenate([np.arange(base + h * HEAD_DIM, base + (h + 1) * HEAD_DIM) for h in perm])


def _largest_divisor(n, cap):
    d = min(n, cap)
    while n % d:
        d -= 1
    return d


def kernel(x_prompt, x_sample, cache_cmp_kv, cache_slc_kv, cache_sb_kv, state_win_kv, page_table,
           norm_g, w_in, q_norm_g, k_norm_g, cmp_pe, cmp_w1, cmp_b1, cmp_w2, w_out):
    n_b, t, d = x_prompt.shape
    n_db, dec_seq, _ = x_sample.shape
    depth = w_in.shape[0]
    assert depth == 1, "single-layer step"
    n_pages = page_table.shape[1]
    page = cache_cmp_kv.shape[2]
    past_len = n_pages * page
    n_state = state_win_kv.shape[2]
    assert page == LANES and t % LANES == 0 and t // SEL_BLOCK <= N_SEL_LANES
    assert past_len // SEL_BLOCK <= N_SEL_LANES and dec_seq <= Q_PAD and dec_seq <= SEL_BLOCK
    assert n_state == WINDOW and past_len >= WINDOW and t >= WINDOW

    nsa_w = NSA_HEADS * HEAD_DIM
    kvw = NSA_KV_GROUPS * HEAD_DIM
    sb_w = SB_HEADS * HEAD_DIM
    sbkv_w = SB_KV_HEADS * HEAD_DIM
    o_kv = nsa_w
    o_gt = o_kv + 6 * kvw
    o_za = o_gt + N_BRANCH * NSA_HEADS
    o_qs = o_za + nsa_w
    o_kb = o_qs + sb_w
    o_zb = o_kb + 2 * sbkv_w
    w0 = w_in[0]
    cols = np.concatenate([
        _head_index(0, PERM_A), np.arange(o_kv, o_gt), _head_index(o_za, PERM_A),
        _head_index(o_qs, PERM_B), np.arange(o_kb, o_zb), _head_index(o_zb, PERM_B),
        np.arange(o_gt, o_za)])
    w_perm = jnp.pad(w0[:, cols], ((0, 0), (0, LANES - N_BRANCH * NSA_HEADS))).astype(BF16)
    rows_out = np.concatenate([_head_index(0, PERM_A), _head_index(nsa_w, PERM_B)])
    w_out_perm = w_out[0][rows_out, :].astype(BF16)

    seg = jnp.asarray(np.kron(np.eye(2), np.ones((HEAD_DIM, HEAD_DIM))), BF16)
    tile2 = lambda v: jnp.tile(v, 2).reshape(1, LANES)
    gq = tile2(q_norm_g[0])
    gkc, gks, gkw = tile2(k_norm_g[0, 0]), tile2(k_norm_g[0, 1]), tile2(k_norm_g[0, 2])
    g_in = norm_g[0].reshape(1, d)

    w1 = cmp_w1[0].reshape(2, 2, CMP_STRIDE, HEAD_DIM, CMP_HIDDEN)
    w1 = jnp.transpose(w1, (0, 2, 3, 1, 4))
    zeros = jnp.zeros_like(w1)
    wc = jnp.stack([jnp.concatenate([w1, zeros], axis=3), jnp.concatenate([zeros, w1], axis=3)], axis=2)
    wc = wc.reshape(2, CMP_STRIDE, LANES, 4 * CMP_HIDDEN).astype(BF16)
    w2 = cmp_w2[0]
    z2 = jnp.zeros_like(w2)
    w2p = jnp.stack([jnp.concatenate([w2, z2], axis=2), jnp.concatenate([z2, w2], axis=2)], axis=1).astype(BF16)
    cb = _cbias(cmp_pe[0].reshape(2, 1, CMP_LEN * HEAD_DIM), cmp_w1[0],
                cmp_b1[0].reshape(2, 1, CMP_HIDDEN))

    def selection_map(n_rows):
        r = np.arange(n_rows)[:, None]
        jb = np.arange(N_SEL_LANES)[None, :]
        start = (r - 1) * CMP_STRIDE
        ok = (r >= 1) & (start < jb * SEL_BLOCK + SEL_BLOCK) & (start + CMP_LEN > jb * SEL_BLOCK)
        return jnp.asarray(ok, BF16)

    u = jnp.asarray(np.arange(LANES)[:, None] > np.arange(LANES)[None, :], BF16)

    tm = _largest_divisor(n_b * t, 256)
    xp2 = x_prompt.reshape(n_b * t, d)
    (q16, cmpkv, slckv, winkv, slck, slcv, wink, winv, gates, sza, qs16, sbkv, sbk, sbv, szb) = _proj(
        xp2, g_in, w_perm, gq, gks, gkw, seg, tm)
    n_sub = t // CMP_STRIDE
    sub_w = CMP_STRIDE * 2 * kvw
    ck, cv = _compress_prompt(cmpkv.reshape(n_b * n_sub, sub_w), wc, cb, w2p, gkc, seg,
                              n_b, _largest_divisor(n_sub, 64))
    r3 = lambda a: a.reshape(n_b, t, a.shape[-1])
    oa = _nsa_prompt(r3(q16), r3(gates), ck.reshape(n_b, n_sub, LANES), cv.reshape(n_b, n_sub, 2 * LANES),
                     selection_map(n_sub), r3(slck), r3(slcv), r3(wink), r3(winv))
    ob = _sb_prompt(r3(qs16), r3(sbk), r3(sbv), u)
    y_prompt = _final(xp2, oa.reshape(n_b * t, nsa_w), sza, ob.reshape(n_b * t, sb_w), szb,
                      w_out_perm, tm).reshape(n_b, t, d)
    kv5 = lambda a, n, heads: a.reshape(1, n, -1, 2, heads, HEAD_DIM)
    cmp_p = kv5(cmpkv, n_b, NSA_KV_GROUPS)
    slc_p = kv5(slckv, n_b, NSA_KV_GROUPS)
    sb_p = kv5(sbkv, n_b, SB_KV_HEADS)
    win_p = kv5(winkv, n_b, NSA_KV_GROUPS)[:, :, t - min(WINDOW, t):]

    rows_s = n_db * dec_seq
    tms = _largest_divisor(rows_s, 256)
    xs2 = x_sample.reshape(rows_s, d)
    (q16s, cmpkv_s, slckv_s, winkv_s, _, _, _, _, gates_s, sza_s, qs16s, sbkv_s, _, _, szb_s) = _proj(
        xs2, g_in, w_perm, gq, gks, gkw, seg, tms)
    pages_per_step = _largest_divisor(n_pages, 8)
    table = page_table.reshape(-1).astype(jnp.int32)
    sub_per_page = page // CMP_STRIDE
    n_sub_s = n_pages * sub_per_page
    chan_major = lambda a: jnp.transpose(a, (0, 2, 3, 4, 1)).reshape(a.shape[0], -1, a.shape[1])
    cks, cvs = _compress_sample(table, chan_major(cache_cmp_kv[0]), wc, cb, w2p,
                                gkc, seg, n_db, n_pages, pages_per_step)
    pad8 = lambda a: jnp.pad(a.reshape(n_db, dec_seq, a.shape[-1]).astype(F32),
                             ((0, 0), (0, Q_PAD - dec_seq), (0, 0)))
    qa8, qs8, gate8 = pad8(q16s), pad8(qs16s), pad8(gates_s)
    n_blocks = -(-(past_len + dec_seq) // SEL_BLOCK)
    n_pick = min(SEL_TOPK, n_blocks) - 1
    eall = jnp.asarray(np.arange(N_SEL_LANES)[:, None] == (np.arange(past_len)[None, :] // SEL_BLOCK), BF16)
    ocmp, mexp = _cmp_sample(qa8, cks, cvs, selection_map(n_sub_s), eall, past_len, n_pick)
    oa8, ob8 = _attn_sample(table, qa8, qs8, gate8, ocmp, mexp, pad8(slckv_s), pad8(winkv_s), pad8(sbkv_s),
                            chan_major(state_win_kv[0]), u,
                            chan_major(cache_slc_kv[0]), chan_major(cache_sb_kv[0]),
                            n_pages, pages_per_step, past_len, dec_seq)
    y_sample = _final(xs2, oa8[:, :dec_seq].reshape(rows_s, nsa_w), sza_s,
                      ob8[:, :dec_seq].reshape(rows_s, sb_w), szb_s, w_out_perm, tms).reshape(n_db, dec_seq, d)
    kv5s = lambda a, heads: a.reshape(1, n_db, dec_seq, 2, heads, HEAD_DIM)
    cmp_s = kv5s(cmpkv_s, NSA_KV_GROUPS)
    slc_s = kv5s(slckv_s, NSA_KV_GROUPS)
    sb_s = kv5s(sbkv_s, SB_KV_HEADS)
    win_new = kv5s(winkv_s, NSA_KV_GROUPS)
    win_s = jnp.concatenate([state_win_kv[:, :, dec_seq:], win_new], axis=2)

    return (y_prompt, y_sample, cmp_p, cmp_s, slc_p, slc_s, sb_p, sb_s, win_p, win_s)
```

```python
import functools

import numpy as np
import jax
import jax.numpy as jnp
from jax import lax
from jax.experimental import pallas as pl
from jax.experimental.pallas import tpu as pltpu

F32 = jnp.float32
BF16 = jnp.bfloat16

HEAD_DIM = 64
NSA_HEADS = 8
SB_HEADS = 8
NSA_KV_GROUPS = 2
SB_KV_HEADS = 4
HEADS_PER_GROUP = NSA_HEADS // NSA_KV_GROUPS
N_BRANCH = 3
CMP_LEN = 32
CMP_STRIDE = 16
CMP_HIDDEN = 128
SEL_BLOCK = 64
SEL_TOPK = 16
WINDOW = 512
RMS_EPS = 1e-6
BIG = 1e30
PICKED = -3e38
SCALE = HEAD_DIM ** -0.5
SEL_SHIFT = 6
assert 1 << SEL_SHIFT == SEL_BLOCK

LANES = 128
SUBLANES = 8
N_SEL_LANES = 128
VMEM_LIMIT = 52 * 1024 * 1024
CMP_Q_ROWS = 512
CMP_SAMPLE_SEQS = 8
COMPRESS_PAGES = 32
ATTN_PAGES = 8

PERM_A = (0, 4, 1, 5, 2, 6, 3, 7)
PERM_B = (0, 2, 1, 3, 4, 6, 5, 7)


def _slope(h):
    return float(2.0 ** (-(h + 1)))


def _dot(a, b):
    return jnp.dot(a, b, preferred_element_type=F32)


def _dot_nt(a, b):
    return lax.dot_general(a, b, (((1,), (1,)), ((), ())), preferred_element_type=F32)


def _dot_split(x, m):
    hi = x.astype(BF16)
    lo = (x - hi.astype(F32)).astype(BF16)
    return _dot(hi, m) + _dot(lo, m)


def _softplus(z):
    return jnp.maximum(z, 0.0) + jnp.log(1.0 + jnp.exp(-jnp.abs(z)))


def _silu(z):
    return z * jax.nn.sigmoid(z)


def _proj_kernel(x_ref, g_ref, w_ref, gq_ref, gks_ref, gkw_ref, seg_ref,
                 q_ref, cmp_ref, slc_ref, win_ref, slck_ref, slcv_ref, wink_ref, winv_ref,
                 gate_ref, sza_ref, qs_ref, sb_ref, sbk_ref, sbv_ref, szb_ref):
    x = x_ref[...]
    ms = jnp.mean(x * x, axis=-1, keepdims=True)
    hn = ((x * lax.rsqrt(ms + RMS_EPS)) * g_ref[...]).astype(BF16)
    seg = seg_ref[...]
    lane = lax.broadcasted_iota(jnp.int32, (x.shape[0], LANES), 1)
    low = lane < HEAD_DIM

    def sec(a, n=LANES):
        return _dot(hn, w_ref[:, a:a + n])

    def headnorm(t, gain):
        ss = _dot_split(t * t, seg) * (1.0 / HEAD_DIM)
        return (t * lax.rsqrt(ss + RMS_EPS)) * gain

    def variants(v):
        one = jnp.ones_like(v)
        return jnp.where(low, v, one).astype(BF16), jnp.where(low, one, v).astype(BF16)

    for c in range(4):
        t = headnorm(sec(c * LANES), gq_ref[...])
        q_ref[:, c * LANES:(c + 1) * LANES] = (t * SCALE).astype(BF16)
    base = 4 * LANES
    cmp_ref[:, 0:LANES] = sec(base)
    cmp_ref[:, LANES:2 * LANES] = sec(base + LANES)
    for kv_out, kk_out, vv_out, gain_ref, off in (
            (slc_ref, slck_ref, slcv_ref, gks_ref, base + 2 * LANES),
            (win_ref, wink_ref, winv_ref, gkw_ref, base + 4 * LANES)):
        kn = headnorm(sec(off), gain_ref[...])
        v = sec(off + LANES)
        kv_out[:, 0:LANES] = kn
        kv_out[:, LANES:2 * LANES] = v
        kk_out[...] = kn.astype(BF16)
        v0, v1 = variants(v)
        vv_out[:, 0:LANES] = v0
        vv_out[:, LANES:2 * LANES] = v1
    base = 10 * LANES
    for c in range(4):
        sza_ref[:, c * LANES:(c + 1) * LANES] = _silu(sec(base + c * LANES))
    base = 14 * LANES
    for c in range(4):
        qs_ref[:, c * LANES:(c + 1) * LANES] = (sec(base + c * LANES) * SCALE).astype(BF16)
    base = 18 * LANES
    for c in range(2):
        kb = sec(base + c * LANES)
        vb = sec(base + (2 + c) * LANES)
        sb_ref[:, c * LANES:(c + 1) * LANES] = kb
        sb_ref[:, (2 + c) * LANES:(3 + c) * LANES] = vb
        sbk_ref[:, c * LANES:(c + 1) * LANES] = kb.astype(BF16)
        sbv_ref[:, c * LANES:(c + 1) * LANES] = vb.astype(BF16)
    base = 22 * LANES
    for c in range(4):
        szb_ref[:, c * LANES:(c + 1) * LANES] = _silu(sec(base + c * LANES))
    gate_ref[...] = jax.nn.sigmoid(sec(26 * LANES))


def _proj(x2, g, w_perm, gq, gks, gkw, seg, tm):
    rows, d = x2.shape
    wcols = w_perm.shape[1]
    row = lambda i: (i, 0)
    const = lambda i: (0, 0)
    widths = [(512, BF16), (256, F32), (256, F32), (256, F32), (128, BF16), (256, BF16),
              (128, BF16), (256, BF16), (128, F32), (512, F32), (512, BF16), (512, F32),
              (256, BF16), (256, BF16), (512, F32)]
    return pl.pallas_call(
        _proj_kernel,
        grid=(rows // tm,),
        in_specs=[pl.BlockSpec((tm, d), row), pl.BlockSpec((1, d), const),
                  pl.BlockSpec((d, wcols), const), pl.BlockSpec((1, LANES), const),
                  pl.BlockSpec((1, LANES), const), pl.BlockSpec((1, LANES), const),
                  pl.BlockSpec((LANES, LANES), const)],
        out_specs=[pl.BlockSpec((tm, w), row) for w, _ in widths],
        out_shape=[jax.ShapeDtypeStruct((rows, w), dt) for w, dt in widths],
        compiler_params=pltpu.CompilerParams(dimension_semantics=("arbitrary",),
                                             vmem_limit_bytes=VMEM_LIMIT),
        name="proj",
    )(x2, g, w_perm, gq, gks, gkw, seg)


def _cbias_kernel(pe_ref, w1_ref, b1_ref, o_ref):
    for kv in range(2):
        pe = jnp.broadcast_to(pe_ref[kv], (SUBLANES, pe_ref.shape[-1])).astype(BF16)
        o_ref[kv] = _dot(pe, w1_ref[kv].astype(BF16)) + b1_ref[kv]


def _cbias(pe_flat, w1, b1):
    return pl.pallas_call(
        _cbias_kernel,
        out_shape=jax.ShapeDtypeStruct((2, SUBLANES, CMP_HIDDEN), F32),
        name="cbias",
    )(pe_flat, w1, b1)


def _compress_kernel(*refs, n_in, has_table):
    if has_table:
        refs = refs[1:]
    x_refs = refs[:n_in]
    if has_table:
        perm_ref = refs[n_in]
        refs = refs[1:]
    wc_ref, cb_ref, w2_ref, gk_ref, seg_ref = refs[n_in:n_in + 5]
    ck_ref, cv_ref = refs[n_in + 5:n_in + 7]
    carry_ref = refs[n_in + 7]
    n_rows = ck_ref.shape[0]

    @pl.when(pl.program_id(1) == 0)
    def _():
        carry_ref[...] = jnp.zeros(carry_ref.shape, F32)

    row = lax.broadcasted_iota(jnp.int32, (n_rows, CMP_HIDDEN), 0)
    lane = lax.broadcasted_iota(jnp.int32, (n_rows, LANES), 1)
    low = lane < HEAD_DIM
    for kv in range(2):
        if has_table:
            perm = perm_ref[...]
            pages = [_dot_nt(perm, r[0, kv * LANES:(kv + 1) * LANES, :].astype(BF16)) for r in x_refs]
        acc = jnp.zeros((n_rows, 4 * CMP_HIDDEN), F32)
        for s in range(CMP_STRIDE):
            if has_table:
                nb = LANES // CMP_STRIDE
                xk = jnp.concatenate([pg[s * nb:(s + 1) * nb] for pg in pages], axis=0)
            else:
                a = s * 2 * LANES + kv * LANES
                xk = x_refs[0][:, a:a + LANES]
            acc = acc + _dot(xk.astype(BF16), wc_ref[kv, s])
        cb = cb_ref[kv, 0:1, :]
        out = jnp.zeros((n_rows, LANES), F32)
        for g in range(NSA_KV_GROUPS):
            a0 = acc[:, (2 * g) * CMP_HIDDEN:(2 * g + 1) * CMP_HIDDEN]
            a1 = acc[:, (2 * g + 1) * CMP_HIDDEN:(2 * g + 2) * CMP_HIDDEN]
            prev = carry_ref[kv, SUBLANES - 1:SUBLANES, (2 * g) * CMP_HIDDEN:(2 * g + 1) * CMP_HIDDEN]
            shifted = jnp.where(row == 0, prev, pltpu.roll(a0, 1, axis=0))
            h = shifted + a1 + cb
            out = out + _dot(_silu(h).astype(BF16), w2_ref[kv, g])
        carry_ref[kv] = acc[n_rows - SUBLANES:, :]
        if kv == 0:
            ss = _dot_split(out * out, seg_ref[...]) * (1.0 / HEAD_DIM)
            ck_ref[...] = ((out * lax.rsqrt(ss + RMS_EPS)) * gk_ref[...]).astype(BF16)
        else:
            one = jnp.ones_like(out)
            cv_ref[:, 0:LANES] = jnp.where(low, out, one).astype(BF16)
            cv_ref[:, LANES:2 * LANES] = jnp.where(low, one, out).astype(BF16)


def _compress_out_shape(n_out_rows):
    return [jax.ShapeDtypeStruct((n_out_rows, LANES), BF16),
            jax.ShapeDtypeStruct((n_out_rows, 2 * LANES), BF16)]


def _compress_weight_specs():
    return [pl.BlockSpec((2, CMP_STRIDE, LANES, 4 * CMP_HIDDEN), lambda *a: (0, 0, 0, 0)),
            pl.BlockSpec((2, SUBLANES, CMP_HIDDEN), lambda *a: (0, 0, 0)),
            pl.BlockSpec((2, NSA_KV_GROUPS, CMP_HIDDEN, LANES), lambda *a: (0, 0, 0, 0)),
            pl.BlockSpec((1, LANES), lambda *a: (0, 0)),
            pl.BlockSpec((LANES, LANES), lambda *a: (0, 0))]


def _compress_prompt(xsub, wc, cb, w2p, gk, seg, n_batch, rows_per_step):
    total = xsub.shape[0]
    per_b = total // n_batch
    nj = per_b // rows_per_step
    kern = functools.partial(_compress_kernel, n_in=1, has_table=False)
    return pl.pallas_call(
        kern,
        grid=(n_batch, nj),
        in_specs=[pl.BlockSpec((rows_per_step, xsub.shape[1]), lambda b, j: (b * nj + j, 0))]
        + _compress_weight_specs(),
        out_specs=[pl.BlockSpec((rows_per_step, LANES), lambda b, j: (b * nj + j, 0)),
                   pl.BlockSpec((rows_per_step, 2 * LANES), lambda b, j: (b * nj + j, 0))],
        out_shape=_compress_out_shape(total),
        scratch_shapes=[pltpu.VMEM((2, SUBLANES, 4 * CMP_HIDDEN), F32)],
        compiler_params=pltpu.CompilerParams(dimension_semantics=("arbitrary", "arbitrary"),
                                             vmem_limit_bytes=VMEM_LIMIT),
        name="compress_prompt",
    )(xsub, wc, cb, w2p, gk, seg)


def _compress_sample(table, pool_t, wc, cb, w2p, gk, seg, n_batch, n_pages, pages_per_step):
    page = pool_t.shape[2]
    sub_per_page = page // CMP_STRIDE
    nj = n_pages // pages_per_step
    rows_per_step = pages_per_step * sub_per_page
    total = n_batch * n_pages * sub_per_page
    kern = functools.partial(_compress_kernel, n_in=pages_per_step, has_table=True)
    rho = np.arange(page)
    perm = jnp.asarray(np.arange(page)[None, :] == (CMP_STRIDE * (rho % sub_per_page) + rho // sub_per_page)[:, None],
                       BF16)

    def page_spec(p):
        return pl.BlockSpec((1, pool_t.shape[1], page),
                            lambda b, j, t: (t[b * n_pages + j * pages_per_step + p], 0, 0))

    grid_spec = pltpu.PrefetchScalarGridSpec(
        num_scalar_prefetch=1,
        grid=(n_batch, nj),
        in_specs=[page_spec(p) for p in range(pages_per_step)]
        + [pl.BlockSpec((page, page), lambda b, j, t: (0, 0))] + _compress_weight_specs(),
        out_specs=[pl.BlockSpec((rows_per_step, LANES), lambda b, j, t: (b * nj + j, 0)),
                   pl.BlockSpec((rows_per_step, 2 * LANES), lambda b, j, t: (b * nj + j, 0))],
        scratch_shapes=[pltpu.VMEM((2, SUBLANES, 4 * CMP_HIDDEN), F32)],
    )
    return pl.pallas_call(
        kern,
        grid_spec=grid_spec,
        out_shape=_compress_out_shape(total),
        compiler_params=pltpu.CompilerParams(dimension_semantics=("arbitrary", "arbitrary"),
                                             vmem_limit_bytes=VMEM_LIMIT),
        name="compress_sample",
    )(table, *([pool_t] * pages_per_step), perm, wc, cb, w2p, gk, seg)


def _pick_blocks(v_s, sel_s, lane_f, n_pick):
    sel_s[...] = jnp.zeros(sel_s.shape, F32)

    def body(_, c):
        v = v_s[...]
        mx = jnp.max(v, axis=1, keepdims=True)
        cand = jnp.where(v == mx, lane_f, float(2 * N_SEL_LANES))
        idx = jnp.min(cand, axis=1, keepdims=True)
        hit = lane_f == idx
        sel_s[...] = jnp.where(hit, 1.0, sel_s[...])
        v_s[...] = jnp.where(hit, PICKED, v)
        return c

    lax.fori_loop(0, n_pick, body, 0)


def _cmp_probs(qh, ck, slope, crel, cmask):
    s = _dot_nt(qh, ck) - slope * crel
    s = jnp.where(cmask, s, -BIG)
    m = jnp.max(s, axis=1, keepdims=True)
    e = jnp.where(cmask, jnp.exp(s - m), 0.0)
    return e / jnp.maximum(jnp.sum(e, axis=1, keepdims=True), 1e-30)


def _normalise(acc):
    return acc / jnp.maximum(pltpu.roll(acc, HEAD_DIM, axis=1), 1e-30)


def _sb_weights(z, causal, carry, u):
    sp = _softplus(z)
    l1m = -sp if causal is None else jnp.where(causal, -sp, 0.0)
    a = jnp.exp((z - sp) + _dot_split(l1m, u) + carry)
    if causal is not None:
        a = jnp.where(causal, a, 0.0)
    return a, jnp.sum(l1m, axis=1, keepdims=True)


def _reverse_pairs(first, lo, tile):
    tile(first, True)
    n_rest = first - lo
    odd = n_rest & 1

    @pl.when(odd == 1)
    def _():
        tile(first - 1, False)

    top = first - 1 - odd

    def body(t, c):
        kb = top - 2 * t
        tile(kb, False)
        tile(kb - 1, False)
        return c

    lax.fori_loop(0, n_rest >> 1, body, 0)


def _cmp_prompt_kernel(q_ref, gate_ref, ck_ref, cv_ref, selmap_ref, oc_ref, sel_ref, v_s, sel_s):
    nq = q_ref.shape[1]
    q0 = pl.program_id(1) * nq
    n_c = ck_ref.shape[1]
    lane = lax.broadcasted_iota(jnp.int32, (nq, LANES), 1)
    low = lane < HEAD_DIM
    crow = lax.broadcasted_iota(jnp.int32, (nq, n_c), 1)
    cend = crow * CMP_STRIDE + (CMP_STRIDE - 1)
    cq = q0 + lax.broadcasted_iota(jnp.int32, (nq, n_c), 0)
    cmask = jnp.logical_and(cend <= cq, crow >= 1)
    crel = (cq - cend).astype(F32)
    ck = ck_ref[0]
    cur = (q0 + lax.broadcasted_iota(jnp.int32, (nq, LANES), 0)) >> SEL_SHIFT
    forced = jnp.logical_or(lane == 0, jnp.logical_or(lane == cur, lane == cur - 1))
    allowed = lane <= cur
    outs = [[None] * HEADS_PER_GROUP for _ in range(NSA_KV_GROUPS)]
    for g in range(NSA_KV_GROUPS):
        keep = low if g == 0 else jnp.logical_not(low)
        imp = jnp.zeros((nq, N_SEL_LANES), F32)
        for c in range(HEADS_PER_GROUP):
            h = g * HEADS_PER_GROUP + c
            qc = q_ref[0, :, c * LANES:(c + 1) * LANES]
            p = _cmp_probs(jnp.where(keep, qc, jnp.zeros_like(qc)), ck, _slope(h), crel, cmask)
            oc = _dot(p.astype(BF16), cv_ref[0, :, g * LANES:(g + 1) * LANES])
            outs[g][c] = oc * gate_ref[0, :, h:h + 1]
            imp = imp + _dot_split(p, selmap_ref[...])
        v_s[g * nq:(g + 1) * nq, :] = jnp.where(forced, BIG, jnp.where(allowed, imp, -BIG))
    for c in range(HEADS_PER_GROUP):
        oc_ref[0, :, c * LANES:(c + 1) * LANES] = jnp.where(low, outs[0][c], outs[1][c])
    lane_f = lax.broadcasted_iota(jnp.int32, v_s.shape, 1).astype(F32)
    _pick_blocks(v_s, sel_s, lane_f, SEL_TOPK)
    for g in range(NSA_KV_GROUPS):
        sel_ref[0, g] = jnp.where(allowed, sel_s[g * nq:(g + 1) * nq, :], 0.0).astype(BF16)


def _cmp_prompt(q, gates, ck, cv, selmap, nq):
    n_b, t, _ = q.shape
    n_c = ck.shape[1]
    per_q = lambda b, i: (b, i, 0)
    per_b = lambda b, i: (b, 0, 0)
    return pl.pallas_call(
        _cmp_prompt_kernel,
        grid=(n_b, t // nq),
        in_specs=[pl.BlockSpec((1, nq, 4 * LANES), per_q),
                  pl.BlockSpec((1, nq, LANES), per_q),
                  pl.BlockSpec((1, n_c, LANES), per_b),
                  pl.BlockSpec((1, n_c, 2 * LANES), per_b),
                  pl.BlockSpec((n_c, N_SEL_LANES), lambda b, i: (0, 0))],
        out_specs=[pl.BlockSpec((1, nq, 4 * LANES), per_q),
                   pl.BlockSpec((1, NSA_KV_GROUPS, nq, N_SEL_LANES), lambda b, i: (b, 0, i, 0))],
        out_shape=[jax.ShapeDtypeStruct((n_b, t, 4 * LANES), F32),
                   jax.ShapeDtypeStruct((n_b, NSA_KV_GROUPS, t, N_SEL_LANES), BF16)],
        scratch_shapes=[pltpu.VMEM((NSA_KV_GROUPS * nq, N_SEL_LANES), F32),
                        pltpu.VMEM((NSA_KV_GROUPS * nq, N_SEL_LANES), F32)],
        compiler_params=pltpu.CompilerParams(dimension_semantics=("arbitrary", "arbitrary"),
                                             vmem_limit_bytes=VMEM_LIMIT),
        name="cmp_prompt",
    )(q, gates, ck, cv, selmap)


def _slcwin_prompt_kernel(q_ref, gate_ref, oc_ref, sel_ref, sk_ref, sv_ref, wk_ref, wv_ref, o_ref,
                          qm_s, m_s, acc_s, tot_s):
    i = pl.program_id(1)
    nq = LANES
    lane = lax.broadcasted_iota(jnp.int32, (nq, LANES), 1)
    row = lax.broadcasted_iota(jnp.int32, (nq, LANES), 0)
    low = lane < HEAD_DIM
    qpos = i * nq + row

    for g in range(NSA_KV_GROUPS):
        keep = low if g == 0 else jnp.logical_not(low)
        for c in range(HEADS_PER_GROUP):
            qc = q_ref[0, :, c * LANES:(c + 1) * LANES]
            r0 = (g * HEADS_PER_GROUP + c) * nq
            qm_s[r0:r0 + nq, :] = jnp.where(keep, qc, jnp.zeros_like(qc))

    def run_branch(k_ref, v_ref, banded, lo, gate_base, first_branch):
        m_s[...] = jnp.full(m_s.shape, -BIG, F32)
        acc_s[...] = jnp.zeros(acc_s.shape, F32)

        def tile(kb, diag):
            k0 = pl.multiple_of(kb * LANES, LANES)
            kt = k_ref[0, pl.ds(k0, LANES), :]
            rel = qpos - (k0 + lane)
            relf = rel.astype(F32)
            s = _dot_nt(qm_s[...], kt)
            if banded:
                band = jnp.where(jnp.logical_and(rel >= 0, rel < WINDOW), 0.0, -BIG)
            else:
                blk = (LANES // SEL_BLOCK) * kb + (lane >> SEL_SHIFT)
                expand = jnp.where(row == blk, 1.0, 0.0).astype(BF16)
            for g in range(NSA_KV_GROUPS):
                if banded:
                    bias = band
                else:
                    bias = (_dot(sel_ref[0, g], expand) - 1.0) * BIG
                    if diag:
                        bias = jnp.where(rel >= 0, bias, -BIG)
                ps, alphas = [], []
                for c in range(HEADS_PER_GROUP):
                    h = g * HEADS_PER_GROUP + c
                    r0 = h * nq
                    sh = (s[r0:r0 + nq] - _slope(h) * relf) + bias
                    m_old = m_s[r0:r0 + nq, :]
                    m_new = jnp.maximum(m_old, jnp.max(sh, axis=1, keepdims=True))
                    ps.append(jnp.exp(sh - m_new).astype(BF16))
                    alphas.append(jnp.exp(m_old - m_new))
                    m_s[r0:r0 + nq, :] = m_new
                g0 = g * HEADS_PER_GROUP * nq
                g1 = (g + 1) * HEADS_PER_GROUP * nq
                vt = v_ref[0, pl.ds(k0, LANES), g * LANES:(g + 1) * LANES]
                pv = _dot(jnp.concatenate(ps, axis=0), vt)
                acc_s[g0:g1, :] = jnp.concatenate(alphas, axis=0) * acc_s[g0:g1, :] + pv

        _reverse_pairs(i, lo, tile)
        for h in range(NSA_HEADS):
            r0 = h * nq
            o = _normalise(acc_s[r0:r0 + nq, :]) * gate_ref[0, :, gate_base + h:gate_base + h + 1]
            tot_s[r0:r0 + nq, :] = o if first_branch else tot_s[r0:r0 + nq, :] + o

    run_branch(sk_ref, sv_ref, False, 0, NSA_HEADS, True)
    run_branch(wk_ref, wv_ref, True, jnp.maximum(i - WINDOW // LANES, 0), 2 * NSA_HEADS, False)

    for c in range(HEADS_PER_GROUP):
        a = tot_s[c * nq:(c + 1) * nq, :]
        b = tot_s[(HEADS_PER_GROUP + c) * nq:(HEADS_PER_GROUP + c + 1) * nq, :]
        o_ref[0, :, c * LANES:(c + 1) * LANES] = oc_ref[0, :, c * LANES:(c + 1) * LANES] + jnp.where(low, a, b)


def _slcwin_prompt(q, gates, oc, sel, sk, sv, wk, wv):
    n_b, t, _ = q.shape
    nqb = t // LANES
    n_rows = NSA_HEADS * LANES
    per_q = lambda b, i: (b, i, 0)
    per_b = lambda b, i: (b, 0, 0)
    return pl.pallas_call(
        _slcwin_prompt_kernel,
        grid=(n_b, nqb),
        in_specs=[pl.BlockSpec((1, LANES, 4 * LANES), per_q),
                  pl.BlockSpec((1, LANES, LANES), per_q),
                  pl.BlockSpec((1, LANES, 4 * LANES), per_q),
                  pl.BlockSpec((1, NSA_KV_GROUPS, LANES, N_SEL_LANES), lambda b, i: (b, 0, i, 0)),
                  pl.BlockSpec((1, t, LANES), per_b),
                  pl.BlockSpec((1, t, 2 * LANES), per_b),
                  pl.BlockSpec((1, t, LANES), per_b),
                  pl.BlockSpec((1, t, 2 * LANES), per_b)],
        out_specs=pl.BlockSpec((1, LANES, 4 * LANES), per_q),
        out_shape=jax.ShapeDtypeStruct((n_b, t, 4 * LANES), F32),
        scratch_shapes=[pltpu.VMEM((n_rows, LANES), BF16),
                        pltpu.VMEM((n_rows, LANES), F32),
                        pltpu.VMEM((n_rows, LANES), F32),
                        pltpu.VMEM((n_rows, LANES), F32)],
        compiler_params=pltpu.CompilerParams(dimension_semantics=("arbitrary", "arbitrary"),
                                             vmem_limit_bytes=VMEM_LIMIT),
        name="slcwin_prompt",
    )(q, gates, oc, sel, sk, sv, wk, wv)


def _sb_prompt_kernel(q_ref, k_ref, v_ref, u_ref, o_ref, qm_s, carry_s, acc_s):
    i = pl.program_id(1)
    nq = LANES
    n_pairs = SB_KV_HEADS // 2
    rows = 4 * nq
    lane = lax.broadcasted_iota(jnp.int32, (rows, LANES), 1)
    qrow = lax.broadcasted_iota(jnp.int32, (rows, LANES), 0) & (nq - 1)
    low = lax.broadcasted_iota(jnp.int32, (nq, LANES), 1) < HEAD_DIM
    qpos = i * nq + qrow
    for kp in range(n_pairs):
        for r in range(2):
            ch = 2 * kp + r
            qc = q_ref[0, :, ch * LANES:(ch + 1) * LANES]
            zero = jnp.zeros_like(qc)
            qm_s[kp, (2 * r) * nq:(2 * r + 1) * nq, :] = jnp.where(low, qc, zero)
            qm_s[kp, (2 * r + 1) * nq:(2 * r + 2) * nq, :] = jnp.where(low, zero, qc)
    carry_s[...] = jnp.zeros(carry_s.shape, F32)
    acc_s[...] = jnp.zeros(acc_s.shape, F32)
    u = u_ref[...]

    def tile(kb, diag):
        k0 = pl.multiple_of(kb * LANES, LANES)
        causal = (k0 + lane) < qpos if diag else None
        for kp in range(n_pairs):
            kt = k_ref[0, pl.ds(k0, LANES), kp * LANES:(kp + 1) * LANES]
            vt = v_ref[0, pl.ds(k0, LANES), kp * LANES:(kp + 1) * LANES]
            z = _dot_nt(qm_s[kp], kt)
            a, tile_sum = _sb_weights(z, causal, carry_s[kp], u)
            acc_s[kp] = acc_s[kp] + _dot(a.astype(BF16), vt)
            carry_s[kp] = carry_s[kp] + tile_sum

    _reverse_pairs(i, 0, tile)
    for kp in range(n_pairs):
        for r in range(2):
            ch = 2 * kp + r
            o_ref[0, :, ch * LANES:(ch + 1) * LANES] = jnp.where(
                low, acc_s[kp, (2 * r) * nq:(2 * r + 1) * nq, :], acc_s[kp, (2 * r + 1) * nq:(2 * r + 2) * nq, :])


def _sb_prompt(qs, sbk, sbv, u):
    n_b, t, _ = qs.shape
    nqb = t // LANES
    n_pairs = SB_KV_HEADS // 2
    per_q = lambda b, i: (b, i, 0)
    per_b = lambda b, i: (b, 0, 0)
    return pl.pallas_call(
        _sb_prompt_kernel,
        grid=(n_b, nqb),
        in_specs=[pl.BlockSpec((1, LANES, 4 * LANES), per_q),
                  pl.BlockSpec((1, t, n_pairs * LANES), per_b),
                  pl.BlockSpec((1, t, n_pairs * LANES), per_b),
                  pl.BlockSpec((LANES, LANES), lambda b, i: (0, 0))],
        out_specs=pl.BlockSpec((1, LANES, 4 * LANES), per_q),
        out_shape=jax.ShapeDtypeStruct((n_b, t, 4 * LANES), F32),
        scratch_shapes=[pltpu.VMEM((n_pairs, 4 * LANES, LANES), BF16),
                        pltpu.VMEM((n_pairs, 4 * LANES, LANES), F32),
                        pltpu.VMEM((n_pairs, 4 * LANES, LANES), F32)],
        compiler_params=pltpu.CompilerParams(dimension_semantics=("arbitrary", "arbitrary"),
                                             vmem_limit_bytes=VMEM_LIMIT),
        name="sb_prompt",
    )(qs, sbk, sbv, u)


def _final_kernel(x_ref, oa_ref, sza_ref, ob_ref, szb_ref, w_ref, y_ref):
    half = oa_ref.shape[1]
    ma = (oa_ref[...] * sza_ref[...]).astype(BF16)
    mb = (ob_ref[...] * szb_ref[...]).astype(BF16)
    y_ref[...] = x_ref[...] + _dot(ma, w_ref[0:half, :]) + _dot(mb, w_ref[half:2 * half, :])


def _final(x2, oa, sza, ob, szb, w_out_perm, tm):
    rows, d = x2.shape
    half = oa.shape[1]
    row = lambda i: (i, 0)
    return pl.pallas_call(
        _final_kernel,
        grid=(rows // tm,),
        in_specs=[pl.BlockSpec((tm, d), row), pl.BlockSpec((tm, half), row),
                  pl.BlockSpec((tm, half), row), pl.BlockSpec((tm, half), row),
                  pl.BlockSpec((tm, half), row), pl.BlockSpec((d, d), lambda i: (0, 0))],
        out_specs=pl.BlockSpec((tm, d), row),
        out_shape=jax.ShapeDtypeStruct((rows, d), F32),
        compiler_params=pltpu.CompilerParams(dimension_semantics=("arbitrary",),
                                             vmem_limit_bytes=VMEM_LIMIT),
        name="final",
    )(x2, oa, sza, ob, szb, w_out_perm)


Q_PAD = SUBLANES
Q_SHIFT = 3
assert 1 << Q_SHIFT == Q_PAD
SAMPLE_ROWS = NSA_HEADS * Q_PAD


def _row_ids(shape):
    r = lax.broadcasted_iota(jnp.int32, shape, 0)
    return r >> Q_SHIFT, r & (Q_PAD - 1)


def _head_slopes(shape):
    hidx, _ = _row_ids(shape)
    out = jnp.zeros(shape, F32)
    for h in range(NSA_HEADS):
        out = jnp.where(hidx == h, _slope(h), out)
    return out


def _stack_nsa_queries(q):
    lane = lax.broadcasted_iota(jnp.int32, (Q_PAD, LANES), 1)
    low = lane < HEAD_DIM
    parts = []
    for g in range(NSA_KV_GROUPS):
        keep = low if g == 0 else jnp.logical_not(low)
        for c in range(HEADS_PER_GROUP):
            parts.append(jnp.where(keep, q[:, c * LANES:(c + 1) * LANES], 0.0))
    return jnp.concatenate(parts, axis=0).astype(BF16)


def _cmp_sample_kernel(q_ref, ck_ref, cv_ref, selmap_ref, eall_ref, ocmp_ref, mexp_ref, v_s, sel_s,
                       *, past_len, n_pick):
    n_seq = q_ref.shape[0]
    n_rows = SAMPLE_ROWS
    half = n_rows // 2
    n_c = ck_ref.shape[1]
    crow = lax.broadcasted_iota(jnp.int32, (n_rows, n_c), 1)
    cend = crow * CMP_STRIDE + (CMP_STRIDE - 1)
    cq = past_len + _row_ids((n_rows, n_c))[1]
    cmask = jnp.logical_and(cend <= cq, crow >= 1)
    slope = _head_slopes((n_rows, n_c))
    crel = (cq - cend).astype(F32)
    n_past_blocks = past_len // SEL_BLOCK
    lane = lax.broadcasted_iota(jnp.int32, (NSA_KV_GROUPS * Q_PAD, N_SEL_LANES), 1)
    forced = jnp.logical_or(lane == 0, lane == n_past_blocks - 1)
    allowed = lane < n_past_blocks
    rows_seq = NSA_KV_GROUPS * Q_PAD
    for b in range(n_seq):
        p = _cmp_probs(_stack_nsa_queries(q_ref[b]), ck_ref[b], slope, crel, cmask)
        pb = p.astype(BF16)
        ocmp_ref[b, 0:half, :] = _dot(pb[0:half], cv_ref[b, :, 0:LANES])
        ocmp_ref[b, half:n_rows, :] = _dot(pb[half:n_rows], cv_ref[b, :, LANES:2 * LANES])
        pg = []
        for g in range(NSA_KV_GROUPS):
            acc = jnp.zeros((Q_PAD, n_c), F32)
            for c in range(HEADS_PER_GROUP):
                r0 = (g * HEADS_PER_GROUP + c) * Q_PAD
                acc = acc + p[r0:r0 + Q_PAD]
            pg.append(acc)
        imp = _dot_split(jnp.concatenate(pg, axis=0), selmap_ref[...])
        v_s[b * rows_seq:(b + 1) * rows_seq, :] = jnp.where(allowed, jnp.where(forced, BIG, imp), -BIG)
    lane_f = lax.broadcasted_iota(jnp.int32, v_s.shape, 1).astype(F32)
    _pick_blocks(v_s, sel_s, lane_f, n_pick)
    allowed_all = lax.broadcasted_iota(jnp.int32, v_s.shape, 1) < n_past_blocks
    sel = jnp.where(allowed_all, sel_s[...], 0.0).astype(BF16)
    mexp = _dot(sel, eall_ref[...])
    for b in range(n_seq):
        mexp_ref[b] = mexp[b * rows_seq:(b + 1) * rows_seq]


def _cmp_sample(q8, ck, cv, selmap, eall, past_len, n_pick, n_seq):
    n_b = q8.shape[0]
    n_c = ck.shape[1]
    rows_seq = NSA_KV_GROUPS * Q_PAD
    kern = functools.partial(_cmp_sample_kernel, past_len=past_len, n_pick=n_pick)
    blk = lambda b: (b, 0, 0)
    return pl.pallas_call(
        kern,
        grid=(n_b // n_seq,),
        in_specs=[pl.BlockSpec((n_seq, Q_PAD, 4 * LANES), blk),
                  pl.BlockSpec((n_seq, n_c, LANES), blk),
                  pl.BlockSpec((n_seq, n_c, 2 * LANES), blk),
                  pl.BlockSpec((n_c, N_SEL_LANES), lambda b: (0, 0)),
                  pl.BlockSpec((N_SEL_LANES, past_len), lambda b: (0, 0))],
        out_specs=[pl.BlockSpec((n_seq, SAMPLE_ROWS, LANES), blk),
                   pl.BlockSpec((n_seq, rows_seq, past_len), blk)],
        out_shape=[jax.ShapeDtypeStruct((n_b, SAMPLE_ROWS, LANES), F32),
                   jax.ShapeDtypeStruct((n_b, rows_seq, past_len), F32)],
        scratch_shapes=[pltpu.VMEM((n_seq * rows_seq, N_SEL_LANES), F32),
                        pltpu.VMEM((n_seq * rows_seq, N_SEL_LANES), F32)],
        compiler_params=pltpu.CompilerParams(dimension_semantics=("arbitrary",),
                                             vmem_limit_bytes=VMEM_LIMIT),
        name="cmp_sample",
    )(q8, ck, cv, selmap, eall)


def _pad_keys(x):
    return jnp.concatenate([x, jnp.zeros((LANES - Q_PAD, x.shape[1]), F32)], axis=0)


def _value_variants(v, axis):
    first = lax.broadcasted_iota(jnp.int32, v.shape, axis) < HEAD_DIM
    one = jnp.ones_like(v)
    return jnp.where(first, v, one).astype(BF16), jnp.where(first, one, v).astype(BF16)


def _attn_sample_kernel(*refs, past_len, dec_seq, n_pages_step):
    refs = refs[1:]
    (qa_ref, qs_ref, gate_ref, ocmp_ref, mexp_ref, slcn_ref, winn_ref, sbn_ref,
     state_ref, u_ref) = refs[:10]
    slc_pages = refs[10:10 + n_pages_step]
    sb_pages = refs[10 + n_pages_step:10 + 2 * n_pages_step]
    oa_ref, ob_ref = refs[10 + 2 * n_pages_step:12 + 2 * n_pages_step]
    qa_s, qb_s, m_s, acc_s, win_s, carry_s, accb_s = refs[12 + 2 * n_pages_step:]

    j = pl.program_id(1)
    n_steps = pl.num_programs(1)
    group = n_steps - 1 - j
    n_rows = SAMPLE_ROWS
    half = n_rows // 2
    lane = lax.broadcasted_iota(jnp.int32, (n_rows, LANES), 1)
    low = lane < HEAD_DIM
    t_q = _row_ids((n_rows, LANES))[1]
    qpos = past_len + t_q
    slope = _head_slopes((n_rows, LANES))
    u = u_ref[...]

    @pl.when(j == 0)
    def _():
        qa = _stack_nsa_queries(qa_ref[0])
        qa_s[...] = qa
        lane8 = lax.broadcasted_iota(jnp.int32, (Q_PAD, LANES), 1)
        low8 = lane8 < HEAD_DIM
        parts = []
        for k in range(SB_KV_HEADS):
            for r in range(SB_HEADS // SB_KV_HEADS):
                ch = 2 * (k // 2) + r
                qc = qs_ref[0, :, ch * LANES:(ch + 1) * LANES]
                keep = low8 if k % 2 == 0 else jnp.logical_not(low8)
                piece = jnp.where(keep, qc, 0.0)
                zero = jnp.zeros_like(piece)
                parts.append(jnp.concatenate([piece, zero] if k // 2 == 0 else [zero, piece], axis=1))
        qb = jnp.concatenate(parts, axis=0).astype(BF16)
        qb_s[...] = qb

        new_valid = lane < dec_seq
        slcn = _pad_keys(slcn_ref[0])
        smask = jnp.logical_and(new_valid, lane <= t_q)
        s = _dot_nt(qa, slcn[:, 0:LANES].astype(BF16)) - slope * (t_q - lane).astype(F32)
        s = jnp.where(smask, s, -BIG)
        m0 = jnp.max(s, axis=1, keepdims=True)
        p = jnp.where(smask, jnp.exp(s - m0), 0.0).astype(BF16)
        v0, v1 = _value_variants(slcn[:, LANES:2 * LANES], 1)
        m_s[...] = jnp.broadcast_to(m0, (n_rows, LANES))
        acc_s[0:half, :] = _dot(p[0:half], v0)
        acc_s[half:n_rows, :] = _dot(p[half:n_rows], v1)

        sbn = _pad_keys(sbn_ref[0])
        z = _dot_nt(qb, sbn[:, 0:2 * LANES].astype(BF16))
        a, tile_sum = _sb_weights(z, jnp.logical_and(new_valid, lane < t_q), 0.0, u)
        accb_s[...] = _dot(a.astype(BF16), sbn[:, 2 * LANES:4 * LANES].astype(BF16))
        carry_s[...] = jnp.broadcast_to(tile_sum, (n_rows, LANES))

        n_state = state_ref.shape[2]
        winn = _pad_keys(winn_ref[0])
        n_k = n_state + LANES
        col = lax.broadcasted_iota(jnp.int32, (n_rows, n_k), 1)
        kpos = jnp.where(col < n_state, past_len - n_state + col, past_len + col - n_state)
        rel = past_len + _row_ids((n_rows, n_k))[1] - kpos
        wmask = jnp.logical_and(jnp.logical_and(rel >= 0, rel < WINDOW), col < n_state + dec_seq)
        qk = jnp.concatenate([_dot(qa, state_ref[0, 0:LANES, :].astype(BF16)),
                              _dot_nt(qa, winn[:, 0:LANES].astype(BF16))], axis=1)
        s = qk - _head_slopes((n_rows, n_k)) * rel.astype(F32)
        s = jnp.where(wmask, s, -BIG)
        m = jnp.max(s, axis=1, keepdims=True)
        e = jnp.where(wmask, jnp.exp(s - m), 0.0)
        p = (e / jnp.maximum(jnp.sum(e, axis=1, keepdims=True), 1e-30)).astype(BF16)
        win_s[...] = (_dot_nt(p[:, 0:n_state], state_ref[0, LANES:2 * LANES, :].astype(BF16))
                      + _dot(p[:, n_state:n_k], winn[:, LANES:2 * LANES].astype(BF16)))

    qa = qa_s[...]
    qb = qb_s[...]
    k_base = group * (n_pages_step * LANES)
    ss = []
    for pi in range(n_pages_step):
        msk = mexp_ref[0, :, pi * LANES:(pi + 1) * LANES]
        bias16 = (msk - 1.0) * BIG
        bias = jnp.concatenate([bias16[0:Q_PAD]] * HEADS_PER_GROUP + [bias16[Q_PAD:2 * Q_PAD]] * HEADS_PER_GROUP,
                               axis=0)
        kpos = k_base + pi * LANES + lane
        qk = _dot(qa, slc_pages[pi][0, 0:LANES, :].astype(BF16))
        ss.append((qk - slope * (qpos - kpos).astype(F32)) + bias)
    m_old = m_s[...]
    m_new = m_old
    for s in ss:
        m_new = jnp.maximum(m_new, jnp.max(s, axis=1, keepdims=True))
    pv0 = jnp.zeros((half, LANES), F32)
    pv1 = jnp.zeros((half, LANES), F32)
    for pi in range(n_pages_step):
        p = jnp.exp(ss[pi] - m_new).astype(BF16)
        v0, v1 = _value_variants(slc_pages[pi][0, LANES:2 * LANES, :], 0)
        pv0 = pv0 + _dot_nt(p[0:half], v0)
        pv1 = pv1 + _dot_nt(p[half:n_rows], v1)
    alpha = jnp.exp(m_old - m_new)
    acc_s[0:half, :] = alpha[0:half] * acc_s[0:half, :] + pv0
    acc_s[half:n_rows, :] = alpha[half:n_rows] * acc_s[half:n_rows, :] + pv1
    m_s[...] = m_new

    zs, sps, sums = [], [], []
    for pi in range(n_pages_step):
        z = _dot(qb, sb_pages[pi][0, 0:2 * LANES, :].astype(BF16))
        sp = _softplus(z)
        zs.append(z)
        sps.append(sp)
        sums.append(jnp.sum(-sp, axis=1, keepdims=True))
    carry = carry_s[...]
    accb = accb_s[...]
    for pi in range(n_pages_step - 1, -1, -1):
        a = jnp.exp((zs[pi] - sps[pi]) + _dot_split(-sps[pi], u) + carry)
        accb = accb + _dot_nt(a.astype(BF16), sb_pages[pi][0, 2 * LANES:4 * LANES, :].astype(BF16))
        carry = carry + sums[pi]
    accb_s[...] = accb
    carry_s[...] = carry

    @pl.when(j == n_steps - 1)
    def _():
        o_slc = _normalise(acc_s[...])
        o_win = win_s[...]
        o_cmp = ocmp_ref[0]
        for c in range(HEADS_PER_GROUP):
            tot = []
            for g in range(NSA_KV_GROUPS):
                h = g * HEADS_PER_GROUP + c
                r0 = h * Q_PAD
                gc = gate_ref[0, :, h:h + 1]
                gs = gate_ref[0, :, NSA_HEADS + h:NSA_HEADS + h + 1]
                gw = gate_ref[0, :, 2 * NSA_HEADS + h:2 * NSA_HEADS + h + 1]
                tot.append(gc * o_cmp[r0:r0 + Q_PAD] + gs * o_slc[r0:r0 + Q_PAD] + gw * o_win[r0:r0 + Q_PAD])
            oa_ref[0, :, c * LANES:(c + 1) * LANES] = jnp.where(low[0:Q_PAD], tot[0], tot[1])
        acc_b = accb_s[...]
        for kp in range(SB_KV_HEADS // 2):
            for r in range(SB_HEADS // SB_KV_HEADS):
                ra = ((2 * kp) * 2 + r) * Q_PAD
                rb = ((2 * kp + 1) * 2 + r) * Q_PAD
                a0 = acc_b[ra:ra + Q_PAD, kp * LANES:(kp + 1) * LANES]
                a1 = acc_b[rb:rb + Q_PAD, kp * LANES:(kp + 1) * LANES]
                ch = 2 * kp + r
                ob_ref[0, :, ch * LANES:(ch + 1) * LANES] = jnp.where(low[0:Q_PAD], a0, a1)


def _attn_sample(table, qa8, qs8, gate8, ocmp, mexp, slcn, winn, sbn, state, u,
                 slc_pool, sb_pool, n_pages, pages_per_step, past_len, dec_seq):
    n_b = qa8.shape[0]
    n_rows = SAMPLE_ROWS
    nj = n_pages // pages_per_step
    n_state = state.shape[2]
    kern = functools.partial(_attn_sample_kernel, past_len=past_len, dec_seq=dec_seq,
                             n_pages_step=pages_per_step)
    per_b = lambda b, j, t: (b, 0, 0)

    def page_spec(channels, p):
        return pl.BlockSpec(
            (1, channels, LANES),
            lambda b, j, t: (t[b * n_pages + (nj - 1 - j) * pages_per_step + p], 0, 0))

    grid_spec = pltpu.PrefetchScalarGridSpec(
        num_scalar_prefetch=1,
        grid=(n_b, nj),
        in_specs=[pl.BlockSpec((1, Q_PAD, 4 * LANES), per_b),
                  pl.BlockSpec((1, Q_PAD, 4 * LANES), per_b),
                  pl.BlockSpec((1, Q_PAD, LANES), per_b),
                  pl.BlockSpec((1, n_rows, LANES), per_b),
                  pl.BlockSpec((1, 2 * Q_PAD, pages_per_step * LANES), lambda b, j, t: (b, 0, nj - 1 - j)),
                  pl.BlockSpec((1, Q_PAD, 2 * LANES), per_b),
                  pl.BlockSpec((1, Q_PAD, 2 * LANES), per_b),
                  pl.BlockSpec((1, Q_PAD, 4 * LANES), per_b),
                  pl.BlockSpec((1, 2 * LANES, n_state), per_b),
                  pl.BlockSpec((LANES, LANES), lambda b, j, t: (0, 0))]
        + [page_spec(2 * LANES, p) for p in range(pages_per_step)]
        + [page_spec(4 * LANES, p) for p in range(pages_per_step)],
        out_specs=[pl.BlockSpec((1, Q_PAD, 4 * LANES), per_b),
                   pl.BlockSpec((1, Q_PAD, 4 * LANES), per_b)],
        scratch_shapes=[pltpu.VMEM((n_rows, LANES), BF16),
                        pltpu.VMEM((n_rows, 2 * LANES), BF16),
                        pltpu.VMEM((n_rows, LANES), F32),
                        pltpu.VMEM((n_rows, LANES), F32),
                        pltpu.VMEM((n_rows, LANES), F32),
                        pltpu.VMEM((n_rows, LANES), F32),
                        pltpu.VMEM((n_rows, 2 * LANES), F32)],
    )
    return pl.pallas_call(
        kern,
        grid_spec=grid_spec,
        out_shape=[jax.ShapeDtypeStruct((n_b, Q_PAD, 4 * LANES), F32),
                   jax.ShapeDtypeStruct((n_b, Q_PAD, 4 * LANES), F32)],
        compiler_params=pltpu.CompilerParams(dimension_semantics=("arbitrary", "arbitrary"),
                                             vmem_limit_bytes=VMEM_LIMIT),
        name="attn_sample",
    )(table, qa8, qs8, gate8, ocmp, mexp, slcn, winn, sbn, state, u,
      *([slc_pool] * pages_per_step), *([sb_pool] * pages_per_step))


def _head_index(base, perm):
    return np.concatenate([np.arange(base + h * HEAD_DIM, base + (h + 1) * HEAD_DIM) for h in perm])


def _largest_divisor(n, cap):
    d = min(n, cap)
    while n % d:
        d -= 1
    return d


def kernel(x_prompt, x_sample, cache_cmp_kv, cache_slc_kv, cache_sb_kv, state_win_kv, page_table,
           norm_g, w_in, q_norm_g, k_norm_g, cmp_pe, cmp_w1, cmp_b1, cmp_w2, w_out):
    n_b, t, d = x_prompt.shape
    n_db, dec_seq, _ = x_sample.shape
    depth = w_in.shape[0]
    assert depth == 1, "single-layer step"
    n_pages = page_table.shape[1]
    page = cache_cmp_kv.shape[2]
    past_len = n_pages * page
    n_state = state_win_kv.shape[2]
    assert page == LANES and t % LANES == 0 and t // SEL_BLOCK <= N_SEL_LANES
    assert past_len // SEL_BLOCK <= N_SEL_LANES and dec_seq <= Q_PAD and dec_seq <= SEL_BLOCK
    assert n_state == WINDOW and past_len >= WINDOW and t >= WINDOW

    nsa_w = NSA_HEADS * HEAD_DIM
    kvw = NSA_KV_GROUPS * HEAD_DIM
    sb_w = SB_HEADS * HEAD_DIM
    sbkv_w = SB_KV_HEADS * HEAD_DIM
    o_kv = nsa_w
    o_gt = o_kv + 6 * kvw
    o_za = o_gt + N_BRANCH * NSA_HEADS
    o_qs = o_za + nsa_w
    o_kb = o_qs + sb_w
    o_zb = o_kb + 2 * sbkv_w
    w0 = w_in[0]
    cols = np.concatenate([
        _head_index(0, PERM_A), np.arange(o_kv, o_gt), _head_index(o_za, PERM_A),
        _head_index(o_qs, PERM_B), np.arange(o_kb, o_zb), _head_index(o_zb, PERM_B),
        np.arange(o_gt, o_za)])
    w_perm = jnp.pad(w0[:, cols], ((0, 0), (0, LANES - N_BRANCH * NSA_HEADS))).astype(BF16)
    rows_out = np.concatenate([_head_index(0, PERM_A), _head_index(nsa_w, PERM_B)])
    w_out_perm = w_out[0][rows_out, :].astype(BF16)

    seg = jnp.asarray(np.kron(np.eye(2), np.ones((HEAD_DIM, HEAD_DIM))), BF16)
    tile2 = lambda v: jnp.tile(v, 2).reshape(1, LANES)
    gq = tile2(q_norm_g[0])
    gkc, gks, gkw = tile2(k_norm_g[0, 0]), tile2(k_norm_g[0, 1]), tile2(k_norm_g[0, 2])
    g_in = norm_g[0].reshape(1, d)

    w1 = cmp_w1[0].reshape(2, 2, CMP_STRIDE, HEAD_DIM, CMP_HIDDEN)
    w1 = jnp.transpose(w1, (0, 2, 3, 1, 4))
    zeros = jnp.zeros_like(w1)
    wc = jnp.stack([jnp.concatenate([w1, zeros], axis=3), jnp.concatenate([zeros, w1], axis=3)], axis=2)
    wc = wc.reshape(2, CMP_STRIDE, LANES, 4 * CMP_HIDDEN).astype(BF16)
    w2 = cmp_w2[0]
    z2 = jnp.zeros_like(w2)
    w2p = jnp.stack([jnp.concatenate([w2, z2], axis=2), jnp.concatenate([z2, w2], axis=2)], axis=1).astype(BF16)
    cb = _cbias(cmp_pe[0].reshape(2, 1, CMP_LEN * HEAD_DIM), cmp_w1[0],
                cmp_b1[0].reshape(2, 1, CMP_HIDDEN))

    def selection_map(n_rows):
        r = np.arange(n_rows)[:, None]
        jb = np.arange(N_SEL_LANES)[None, :]
        start = (r - 1) * CMP_STRIDE
        ok = (r >= 1) & (start < jb * SEL_BLOCK + SEL_BLOCK) & (start + CMP_LEN > jb * SEL_BLOCK)
        return jnp.asarray(ok, BF16)

    u = jnp.asarray(np.arange(LANES)[:, None] > np.arange(LANES)[None, :], BF16)

    tm = _largest_divisor(n_b * t, 256)
    xp2 = x_prompt.reshape(n_b * t, d)
    (q16, cmpkv, slckv, winkv, slck, slcv, wink, winv, gates, sza, qs16, sbkv, sbk, sbv, szb) = _proj(
        xp2, g_in, w_perm, gq, gks, gkw, seg, tm)
    n_sub = t // CMP_STRIDE
    sub_w = CMP_STRIDE * 2 * kvw
    ck, cv = _compress_prompt(cmpkv.reshape(n_b * n_sub, sub_w), wc, cb, w2p, gkc, seg,
                              n_b, _largest_divisor(n_sub, 64))
    r3 = lambda a: a.reshape(n_b, t, a.shape[-1])
    oc, sel = _cmp_prompt(r3(q16), r3(gates), ck.reshape(n_b, n_sub, LANES), cv.reshape(n_b, n_sub, 2 * LANES),
                          selection_map(n_sub), _largest_divisor(t, CMP_Q_ROWS))
    oa = _slcwin_prompt(r3(q16), r3(gates), oc, sel, r3(slck), r3(slcv), r3(wink), r3(winv))
    ob = _sb_prompt(r3(qs16), r3(sbk), r3(sbv), u)
    y_prompt = _final(xp2, oa.reshape(n_b * t, nsa_w), sza, ob.reshape(n_b * t, sb_w), szb,
                      w_out_perm, tm).reshape(n_b, t, d)
    kv5 = lambda a, n, heads: a.reshape(1, n, -1, 2, heads, HEAD_DIM)
    cmp_p = kv5(cmpkv, n_b, NSA_KV_GROUPS)
    slc_p = kv5(slckv, n_b, NSA_KV_GROUPS)
    sb_p = kv5(sbkv, n_b, SB_KV_HEADS)
    win_p = kv5(winkv, n_b, NSA_KV_GROUPS)[:, :, t - min(WINDOW, t):]

    rows_s = n_db * dec_seq
    tms = _largest_divisor(rows_s, 256)
    xs2 = x_sample.reshape(rows_s, d)
    (q16s, cmpkv_s, slckv_s, winkv_s, _, _, _, _, gates_s, sza_s, qs16s, sbkv_s, _, _, szb_s) = _proj(
        xs2, g_in, w_perm, gq, gks, gkw, seg, tms)
    table = page_table.reshape(-1).astype(jnp.int32)
    n_sub_s = n_pages * (page // CMP_STRIDE)
    chan_major = lambda a: jnp.transpose(a, (0, 2, 3, 4, 1)).reshape(a.shape[0], -1, a.shape[1])
    cks, cvs = _compress_sample(table, chan_major(cache_cmp_kv[0]), wc, cb, w2p, gkc, seg,
                                n_db, n_pages, _largest_divisor(n_pages, COMPRESS_PAGES))
    pad8 = lambda a: jnp.pad(a.reshape(n_db, dec_seq, a.shape[-1]).astype(F32),
                             ((0, 0), (0, Q_PAD - dec_seq), (0, 0)))
    qa8, qs8, gate8 = pad8(q16s), pad8(qs16s), pad8(gates_s)
    n_blocks = -(-(past_len + dec_seq) // SEL_BLOCK)
    n_pick = min(SEL_TOPK, n_blocks) - 1
    eall = jnp.asarray(np.arange(N_SEL_LANES)[:, None] == (np.arange(past_len)[None, :] // SEL_BLOCK), BF16)
    ocmp, mexp = _cmp_sample(qa8, cks.reshape(n_db, n_sub_s, LANES), cvs.reshape(n_db, n_sub_s, 2 * LANES),
                             selection_map(n_sub_s), eall, past_len, n_pick,
                             _largest_divisor(n_db, CMP_SAMPLE_SEQS))
    oa8, ob8 = _attn_sample(table, qa8, qs8, gate8, ocmp, mexp, pad8(slckv_s), pad8(winkv_s), pad8(sbkv_s),
                            chan_major(state_win_kv[0]), u,
                            chan_major(cache_slc_kv[0]), chan_major(cache_sb_kv[0]),
                            n_pages, _largest_divisor(n_pages, ATTN_PAGES), past_len, dec_seq)
    y_sample = _final(xs2, oa8[:, :dec_seq].reshape(rows_s, nsa_w), sza_s,
                      ob8[:, :dec_seq].reshape(rows_s, sb_w), szb_s, w_out_perm, tms).reshape(n_db, dec_seq, d)
    kv5s = lambda a, heads: a.reshape(1, n_db, dec_seq, 2, heads, HEAD_DIM)
    cmp_s = kv5s(cmpkv_s, NSA_KV_GROUPS)
    slc_s = kv5s(slckv_s, NSA_KV_GROUPS)
    sb_s = kv5s(sbkv_s, SB_KV_HEADS)
    win_new = kv5s(winkv_s, NSA_KV_GROUPS)
    win_s = jnp.concatenate([state_win_kv[:, :, dec_seq:], win_new], axis=2)

    return (y_prompt, y_sample, cmp_p, cmp_s, slc_p, slc_s, sb_p, sb_s, win_p, win_s)
```

```python
import functools

import numpy as np
import jax
import jax.numpy as jnp
from jax import lax
from jax.experimental import pallas as pl
from jax.experimental.pallas import tpu as pltpu

F32 = jnp.float32
BF16 = jnp.bfloat16

HEAD_DIM = 64
NSA_HEADS = 8
SB_HEADS = 8
NSA_KV_GROUPS = 2
SB_KV_HEADS = 4
HEADS_PER_GROUP = NSA_HEADS // NSA_KV_GROUPS
N_BRANCH = 3
CMP_LEN = 32
CMP_STRIDE = 16
CMP_HIDDEN = 128
SEL_BLOCK = 64
SEL_TOPK = 16
WINDOW = 512
RMS_EPS = 1e-6
BIG = 1e30
PICKED = -3e38
EXP_UNDERFLOW = -104.0
SCALE = HEAD_DIM ** -0.5
SEL_SHIFT = 6
assert 1 << SEL_SHIFT == SEL_BLOCK

LANES = 128
SUBLANES = 8
N_SEL_LANES = 128
VMEM_LIMIT = 52 * 1024 * 1024
CMP_Q_ROWS = 512
CMP_SAMPLE_SEQS = 8
COMPRESS_PAGES = 32
ATTN_PAGES = 8

PERM_A = (0, 4, 1, 5, 2, 6, 3, 7)
PERM_B = (0, 2, 1, 3, 4, 6, 5, 7)


def _slope(h):
    return float(2.0 ** (-(h + 1)))


def _dot(a, b):
    return jnp.dot(a, b, preferred_element_type=F32)


def _dot_nt(a, b):
    return lax.dot_general(a, b, (((1,), (1,)), ((), ())), preferred_element_type=F32)


def _dot_split(x, m):
    hi = x.astype(BF16)
    lo = (x - hi.astype(F32)).astype(BF16)
    return _dot(hi, m) + _dot(lo, m)


def _softplus(z):
    return jnp.maximum(z, 0.0) + jnp.log(1.0 + jnp.exp(-jnp.abs(z)))


def _silu(z):
    return z * jax.nn.sigmoid(z)


def _proj_kernel(x_ref, g_ref, w_ref, gq_ref, gks_ref, gkw_ref, seg_ref,
                 q_ref, cmp_ref, slc_ref, win_ref, slck_ref, slcv_ref, wink_ref, winv_ref,
                 gate_ref, sza_ref, qs_ref, sb_ref, sbk_ref, sbv_ref, szb_ref):
    x = x_ref[...]
    ms = jnp.mean(x * x, axis=-1, keepdims=True)
    hn = ((x * lax.rsqrt(ms + RMS_EPS)) * g_ref[...]).astype(BF16)
    seg = seg_ref[...]
    lane = lax.broadcasted_iota(jnp.int32, (x.shape[0], LANES), 1)
    low = lane < HEAD_DIM

    def sec(a, n=LANES):
        return _dot(hn, w_ref[:, a:a + n])

    def headnorm(t, gain):
        ss = _dot_split(t * t, seg) * (1.0 / HEAD_DIM)
        return (t * lax.rsqrt(ss + RMS_EPS)) * gain

    def variants(v):
        one = jnp.ones_like(v)
        return jnp.where(low, v, one).astype(BF16), jnp.where(low, one, v).astype(BF16)

    for c in range(4):
        t = headnorm(sec(c * LANES), gq_ref[...])
        q_ref[:, c * LANES:(c + 1) * LANES] = (t * SCALE).astype(BF16)
    base = 4 * LANES
    cmp_ref[:, 0:LANES] = sec(base)
    cmp_ref[:, LANES:2 * LANES] = sec(base + LANES)
    for kv_out, kk_out, vv_out, gain_ref, off in (
            (slc_ref, slck_ref, slcv_ref, gks_ref, base + 2 * LANES),
            (win_ref, wink_ref, winv_ref, gkw_ref, base + 4 * LANES)):
        kn = headnorm(sec(off), gain_ref[...])
        v = sec(off + LANES)
        kv_out[:, 0:LANES] = kn
        kv_out[:, LANES:2 * LANES] = v
        kk_out[...] = kn.astype(BF16)
        v0, v1 = variants(v)
        vv_out[:, 0:LANES] = v0
        vv_out[:, LANES:2 * LANES] = v1
    base = 10 * LANES
    for c in range(4):
        sza_ref[:, c * LANES:(c + 1) * LANES] = _silu(sec(base + c * LANES))
    base = 14 * LANES
    for c in range(4):
        qs_ref[:, c * LANES:(c + 1) * LANES] = (sec(base + c * LANES) * SCALE).astype(BF16)
    base = 18 * LANES
    for c in range(2):
        kb = sec(base + c * LANES)
        vb = sec(base + (2 + c) * LANES)
        sb_ref[:, c * LANES:(c + 1) * LANES] = kb
        sb_ref[:, (2 + c) * LANES:(3 + c) * LANES] = vb
        sbk_ref[:, c * LANES:(c + 1) * LANES] = kb.astype(BF16)
        sbv_ref[:, c * LANES:(c + 1) * LANES] = vb.astype(BF16)
    base = 22 * LANES
    for c in range(4):
        szb_ref[:, c * LANES:(c + 1) * LANES] = _silu(sec(base + c * LANES))
    gate_ref[...] = jax.nn.sigmoid(sec(26 * LANES))


def _proj(x2, g, w_perm, gq, gks, gkw, seg, tm):
    rows, d = x2.shape
    wcols = w_perm.shape[1]
    row = lambda i: (i, 0)
    const = lambda i: (0, 0)
    widths = [(512, BF16), (256, F32), (256, F32), (256, F32), (128, BF16), (256, BF16),
              (128, BF16), (256, BF16), (128, F32), (512, F32), (512, BF16), (512, F32),
              (256, BF16), (256, BF16), (512, F32)]
    return pl.pallas_call(
        _proj_kernel,
        grid=(rows // tm,),
        in_specs=[pl.BlockSpec((tm, d), row), pl.BlockSpec((1, d), const),
                  pl.BlockSpec((d, wcols), const), pl.BlockSpec((1, LANES), const),
                  pl.BlockSpec((1, LANES), const), pl.BlockSpec((1, LANES), const),
                  pl.BlockSpec((LANES, LANES), const)],
        out_specs=[pl.BlockSpec((tm, w), row) for w, _ in widths],
        out_shape=[jax.ShapeDtypeStruct((rows, w), dt) for w, dt in widths],
        compiler_params=pltpu.CompilerParams(dimension_semantics=("arbitrary",),
                                             vmem_limit_bytes=VMEM_LIMIT),
        name="proj",
    )(x2, g, w_perm, gq, gks, gkw, seg)


def _cbias_kernel(pe_ref, w1_ref, b1_ref, o_ref):
    for kv in range(2):
        pe = jnp.broadcast_to(pe_ref[kv], (SUBLANES, pe_ref.shape[-1])).astype(BF16)
        o_ref[kv] = _dot(pe, w1_ref[kv].astype(BF16)) + b1_ref[kv]


def _cbias(pe_flat, w1, b1):
    return pl.pallas_call(
        _cbias_kernel,
        out_shape=jax.ShapeDtypeStruct((2, SUBLANES, CMP_HIDDEN), F32),
        name="cbias",
    )(pe_flat, w1, b1)


def _compress_kernel(*refs, n_in, has_table):
    if has_table:
        refs = refs[1:]
    x_refs = refs[:n_in]
    if has_table:
        perm_ref = refs[n_in]
        refs = refs[1:]
    wc_ref, cb_ref, w2_ref, gk_ref, seg_ref = refs[n_in:n_in + 5]
    ck_ref, cv_ref = refs[n_in + 5:n_in + 7]
    carry_ref = refs[n_in + 7]
    n_rows = ck_ref.shape[0]

    @pl.when(pl.program_id(1) == 0)
    def _():
        carry_ref[...] = jnp.zeros(carry_ref.shape, F32)

    row = lax.broadcasted_iota(jnp.int32, (n_rows, CMP_HIDDEN), 0)
    lane = lax.broadcasted_iota(jnp.int32, (n_rows, LANES), 1)
    low = lane < HEAD_DIM
    for kv in range(2):
        if has_table:
            perm = perm_ref[...]
            pages = [_dot_nt(perm, r[0, kv * LANES:(kv + 1) * LANES, :].astype(BF16)) for r in x_refs]
        def sub_rows(s):
            if has_table:
                nb = LANES // CMP_STRIDE
                return jnp.concatenate([pg[s * nb:(s + 1) * nb] for pg in pages], axis=0)
            a = s * 2 * LANES + kv * LANES
            return x_refs[0][:, a:a + LANES]

        acc = jnp.zeros((n_rows, 4 * CMP_HIDDEN), F32)
        for s2 in range(CMP_STRIDE // 2):
            xk = jnp.concatenate([sub_rows(2 * s2), sub_rows(2 * s2 + 1)], axis=1)
            acc = acc + _dot(xk.astype(BF16), wc_ref[kv, s2])
        cb = cb_ref[kv, 0:1, :]
        out = jnp.zeros((n_rows, LANES), F32)
        for g in range(NSA_KV_GROUPS):
            a0 = acc[:, (2 * g) * CMP_HIDDEN:(2 * g + 1) * CMP_HIDDEN]
            a1 = acc[:, (2 * g + 1) * CMP_HIDDEN:(2 * g + 2) * CMP_HIDDEN]
            prev = carry_ref[kv, SUBLANES - 1:SUBLANES, (2 * g) * CMP_HIDDEN:(2 * g + 1) * CMP_HIDDEN]
            shifted = jnp.where(row == 0, prev, pltpu.roll(a0, 1, axis=0))
            h = shifted + a1 + cb
            out = out + _dot(_silu(h).astype(BF16), w2_ref[kv, g])
        carry_ref[kv] = acc[n_rows - SUBLANES:, :]
        if kv == 0:
            ss = _dot_split(out * out, seg_ref[...]) * (1.0 / HEAD_DIM)
            ck_ref[...] = ((out * lax.rsqrt(ss + RMS_EPS)) * gk_ref[...]).astype(BF16)
        else:
            one = jnp.ones_like(out)
            cv_ref[:, 0:LANES] = jnp.where(low, out, one).astype(BF16)
            cv_ref[:, LANES:2 * LANES] = jnp.where(low, one, out).astype(BF16)


def _compress_out_shape(n_out_rows):
    return [jax.ShapeDtypeStruct((n_out_rows, LANES), BF16),
            jax.ShapeDtypeStruct((n_out_rows, 2 * LANES), BF16)]


def _compress_weight_specs():
    return [pl.BlockSpec((2, CMP_STRIDE // 2, 2 * LANES, 4 * CMP_HIDDEN), lambda *a: (0, 0, 0, 0)),
            pl.BlockSpec((2, SUBLANES, CMP_HIDDEN), lambda *a: (0, 0, 0)),
            pl.BlockSpec((2, NSA_KV_GROUPS, CMP_HIDDEN, LANES), lambda *a: (0, 0, 0, 0)),
            pl.BlockSpec((1, LANES), lambda *a: (0, 0)),
            pl.BlockSpec((LANES, LANES), lambda *a: (0, 0))]


def _compress_prompt(xsub, wc, cb, w2p, gk, seg, n_batch, rows_per_step):
    total = xsub.shape[0]
    per_b = total // n_batch
    nj = per_b // rows_per_step
    kern = functools.partial(_compress_kernel, n_in=1, has_table=False)
    return pl.pallas_call(
        kern,
        grid=(n_batch, nj),
        in_specs=[pl.BlockSpec((rows_per_step, xsub.shape[1]), lambda b, j: (b * nj + j, 0))]
        + _compress_weight_specs(),
        out_specs=[pl.BlockSpec((rows_per_step, LANES), lambda b, j: (b * nj + j, 0)),
                   pl.BlockSpec((rows_per_step, 2 * LANES), lambda b, j: (b * nj + j, 0))],
        out_shape=_compress_out_shape(total),
        scratch_shapes=[pltpu.VMEM((2, SUBLANES, 4 * CMP_HIDDEN), F32)],
        compiler_params=pltpu.CompilerParams(dimension_semantics=("arbitrary", "arbitrary"),
                                             vmem_limit_bytes=VMEM_LIMIT),
        name="compress_prompt",
    )(xsub, wc, cb, w2p, gk, seg)


def _compress_sample(table, pool_t, wc, cb, w2p, gk, seg, n_batch, n_pages, pages_per_step):
    page = pool_t.shape[2]
    sub_per_page = page // CMP_STRIDE
    nj = n_pages // pages_per_step
    rows_per_step = pages_per_step * sub_per_page
    total = n_batch * n_pages * sub_per_page
    kern = functools.partial(_compress_kernel, n_in=pages_per_step, has_table=True)
    rho = np.arange(page)
    perm = jnp.asarray(np.arange(page)[None, :] == (CMP_STRIDE * (rho % sub_per_page) + rho // sub_per_page)[:, None],
                       BF16)

    def page_spec(p):
        return pl.BlockSpec((1, pool_t.shape[1], page),
                            lambda b, j, t: (t[b * n_pages + j * pages_per_step + p], 0, 0))

    grid_spec = pltpu.PrefetchScalarGridSpec(
        num_scalar_prefetch=1,
        grid=(n_batch, nj),
        in_specs=[page_spec(p) for p in range(pages_per_step)]
        + [pl.BlockSpec((page, page), lambda b, j, t: (0, 0))] + _compress_weight_specs(),
        out_specs=[pl.BlockSpec((rows_per_step, LANES), lambda b, j, t: (b * nj + j, 0)),
                   pl.BlockSpec((rows_per_step, 2 * LANES), lambda b, j, t: (b * nj + j, 0))],
        scratch_shapes=[pltpu.VMEM((2, SUBLANES, 4 * CMP_HIDDEN), F32)],
    )
    return pl.pallas_call(
        kern,
        grid_spec=grid_spec,
        out_shape=_compress_out_shape(total),
        compiler_params=pltpu.CompilerParams(dimension_semantics=("arbitrary", "arbitrary"),
                                             vmem_limit_bytes=VMEM_LIMIT),
        name="compress_sample",
    )(table, *([pool_t] * pages_per_step), perm, wc, cb, w2p, gk, seg)


def _pick_blocks(v_s, sel_s, lane_f, n_pick):
    sel_s[...] = jnp.zeros(sel_s.shape, F32)

    def body(_, c):
        v = v_s[...]
        mx = jnp.max(v, axis=1, keepdims=True)
        cand = jnp.where(v == mx, lane_f, float(2 * N_SEL_LANES))
        idx = jnp.min(cand, axis=1, keepdims=True)
        hit = lane_f == idx
        sel_s[...] = jnp.where(hit, 1.0, sel_s[...])
        v_s[...] = jnp.where(hit, PICKED, v)
        return c

    lax.fori_loop(0, n_pick, body, 0)


def _cmp_probs(qh, ck, slope, crel, cmask):
    s = _dot_nt(qh, ck) - slope * crel
    s = jnp.where(cmask, s, -BIG)
    m = jnp.max(s, axis=1, keepdims=True)
    e = jnp.where(cmask, jnp.exp(s - m), 0.0)
    return e / jnp.maximum(jnp.sum(e, axis=1, keepdims=True), 1e-30)


def _normalise(acc):
    return acc / jnp.maximum(pltpu.roll(acc, HEAD_DIM, axis=1), 1e-30)


def _sb_weights(z, causal, carry, u):
    sp = _softplus(z)
    l1m = -sp if causal is None else jnp.where(causal, -sp, 0.0)
    a = jnp.exp((z - sp) + _dot_split(l1m, u) + carry)
    if causal is not None:
        a = jnp.where(causal, a, 0.0)
    return a, jnp.sum(l1m, axis=1, keepdims=True)


def _tile_sweep(n_tiles, kb_of, tile, live=None):
    tile(kb_of(0), True)
    n_rest = n_tiles - 1
    odd = n_rest & 1

    @pl.when(odd == 1)
    def _():
        tile(kb_of(1), False)

    start = 1 + odd
    n_trips = n_rest >> 1

    def run(t):
        tile(kb_of(start + 2 * t), False)
        tile(kb_of(start + 2 * t + 1), False)

    if live is None:
        def body(t, c):
            run(t)
            return c

        lax.fori_loop(0, n_trips, body, 0)
    else:
        def cond(state):
            t, go = state
            return jnp.logical_and(t < n_trips, go)

        def wbody(state):
            run(state[0])
            return state[0] + 1, live()

        lax.while_loop(cond, wbody, (jnp.int32(0), live()))


def _cmp_prompt_kernel(q_ref, gate_ref, ck_ref, cv_ref, selmap_ref, oc_ref, sel_ref, v_s, sel_s):
    nq = q_ref.shape[1]
    q0 = pl.program_id(1) * nq
    n_c = ck_ref.shape[1]
    lane = lax.broadcasted_iota(jnp.int32, (nq, LANES), 1)
    low = lane < HEAD_DIM
    crow = lax.broadcasted_iota(jnp.int32, (nq, n_c), 1)
    cend = crow * CMP_STRIDE + (CMP_STRIDE - 1)
    cq = q0 + lax.broadcasted_iota(jnp.int32, (nq, n_c), 0)
    cmask = jnp.logical_and(cend <= cq, crow >= 1)
    crel = (cq - cend).astype(F32)
    ck = ck_ref[0]
    cur = (q0 + lax.broadcasted_iota(jnp.int32, (nq, LANES), 0)) >> SEL_SHIFT
    forced = jnp.logical_or(lane == 0, jnp.logical_or(lane == cur, lane == cur - 1))
    allowed = lane <= cur
    outs = [[None] * HEADS_PER_GROUP for _ in range(NSA_KV_GROUPS)]
    for g in range(NSA_KV_GROUPS):
        keep = low if g == 0 else jnp.logical_not(low)
        imp = jnp.zeros((nq, N_SEL_LANES), F32)
        for c in range(HEADS_PER_GROUP):
            h = g * HEADS_PER_GROUP + c
            qc = q_ref[0, :, c * LANES:(c + 1) * LANES]
            p = _cmp_probs(jnp.where(keep, qc, jnp.zeros_like(qc)), ck, _slope(h), crel, cmask)
            oc = _dot(p.astype(BF16), cv_ref[0, :, g * LANES:(g + 1) * LANES])
            outs[g][c] = oc * gate_ref[0, :, h:h + 1]
            imp = imp + _dot_split(p, selmap_ref[...])
        v_s[g * nq:(g + 1) * nq, :] = jnp.where(forced, BIG, jnp.where(allowed, imp, -BIG))
    for c in range(HEADS_PER_GROUP):
        oc_ref[0, :, c * LANES:(c + 1) * LANES] = jnp.where(low, outs[0][c], outs[1][c])
    lane_f = lax.broadcasted_iota(jnp.int32, v_s.shape, 1).astype(F32)
    _pick_blocks(v_s, sel_s, lane_f, SEL_TOPK)
    for g in range(NSA_KV_GROUPS):
        sel_ref[0, g] = jnp.where(allowed, sel_s[g * nq:(g + 1) * nq, :], 0.0).astype(BF16)


def _cmp_prompt(q, gates, ck, cv, selmap, nq):
    n_b, t, _ = q.shape
    n_c = ck.shape[1]
    per_q = lambda b, i: (b, i, 0)
    per_b = lambda b, i: (b, 0, 0)
    return pl.pallas_call(
        _cmp_prompt_kernel,
        grid=(n_b, t // nq),
        in_specs=[pl.BlockSpec((1, nq, 4 * LANES), per_q),
                  pl.BlockSpec((1, nq, LANES), per_q),
                  pl.BlockSpec((1, n_c, LANES), per_b),
                  pl.BlockSpec((1, n_c, 2 * LANES), per_b),
                  pl.BlockSpec((n_c, N_SEL_LANES), lambda b, i: (0, 0))],
        out_specs=[pl.BlockSpec((1, nq, 4 * LANES), per_q),
                   pl.BlockSpec((1, NSA_KV_GROUPS, nq, N_SEL_LANES), lambda b, i: (b, 0, i, 0))],
        out_shape=[jax.ShapeDtypeStruct((n_b, t, 4 * LANES), F32),
                   jax.ShapeDtypeStruct((n_b, NSA_KV_GROUPS, t, N_SEL_LANES), BF16)],
        scratch_shapes=[pltpu.VMEM((NSA_KV_GROUPS * nq, N_SEL_LANES), F32),
                        pltpu.VMEM((NSA_KV_GROUPS * nq, N_SEL_LANES), F32)],
        compiler_params=pltpu.CompilerParams(dimension_semantics=("arbitrary", "arbitrary"),
                                             vmem_limit_bytes=VMEM_LIMIT),
        name="cmp_prompt",
    )(q, gates, ck, cv, selmap)


def _slcwin_prompt_kernel(tiles_ref, count_ref, q_ref, gate_ref, oc_ref, sel_ref, sk_ref, sv_ref, wk_ref, wv_ref,
                          o_ref, qm_s, m_s, acc_s, tot_s):
    i = pl.program_id(1)
    step = pl.program_id(0) * pl.num_programs(1) + i
    tiles_per_step = sk_ref.shape[1] // LANES
    nq = LANES
    lane = lax.broadcasted_iota(jnp.int32, (nq, LANES), 1)
    row = lax.broadcasted_iota(jnp.int32, (nq, LANES), 0)
    low = lane < HEAD_DIM
    qpos = i * nq + row

    for g in range(NSA_KV_GROUPS):
        keep = low if g == 0 else jnp.logical_not(low)
        for c in range(HEADS_PER_GROUP):
            qc = q_ref[0, :, c * LANES:(c + 1) * LANES]
            r0 = (g * HEADS_PER_GROUP + c) * nq
            qm_s[r0:r0 + nq, :] = jnp.where(keep, qc, jnp.zeros_like(qc))

    def run_branch(k_ref, v_ref, banded, n_tiles, kb_of, gate_base, first_branch):
        m_s[...] = jnp.full(m_s.shape, -BIG, F32)
        acc_s[...] = jnp.zeros(acc_s.shape, F32)

        def tile(kb, diag):
            k0 = pl.multiple_of(kb * LANES, LANES)
            kt = k_ref[0, pl.ds(k0, LANES), :]
            rel = qpos - (k0 + lane)
            relf = rel.astype(F32)
            s = _dot_nt(qm_s[...], kt)
            if banded:
                band = jnp.where(jnp.logical_and(rel >= 0, rel < WINDOW), 0.0, -BIG)
            else:
                blk = (LANES // SEL_BLOCK) * kb + (lane >> SEL_SHIFT)
                expand = jnp.where(row == blk, 1.0, 0.0).astype(BF16)
            for g in range(NSA_KV_GROUPS):
                if banded:
                    bias = band
                else:
                    bias = (_dot(sel_ref[0, g], expand) - 1.0) * BIG
                    if diag:
                        bias = jnp.where(rel >= 0, bias, -BIG)
                ps, alphas = [], []
                for c in range(HEADS_PER_GROUP):
                    h = g * HEADS_PER_GROUP + c
                    r0 = h * nq
                    sh = (s[r0:r0 + nq] - _slope(h) * relf) + bias
                    m_old = m_s[r0:r0 + nq, :]
                    m_new = jnp.maximum(m_old, jnp.max(sh, axis=1, keepdims=True))
                    ps.append(jnp.exp(sh - m_new).astype(BF16))
                    alphas.append(jnp.exp(m_old - m_new))
                    m_s[r0:r0 + nq, :] = m_new
                g0 = g * HEADS_PER_GROUP * nq
                g1 = (g + 1) * HEADS_PER_GROUP * nq
                vt = v_ref[0, pl.ds(k0, LANES), g * LANES:(g + 1) * LANES]
                pv = _dot(jnp.concatenate(ps, axis=0), vt)
                acc_s[g0:g1, :] = jnp.concatenate(alphas, axis=0) * acc_s[g0:g1, :] + pv

        _tile_sweep(n_tiles, kb_of, tile)
        for h in range(NSA_HEADS):
            r0 = h * nq
            o = _normalise(acc_s[r0:r0 + nq, :]) * gate_ref[0, :, gate_base + h:gate_base + h + 1]
            tot_s[r0:r0 + nq, :] = o if first_branch else tot_s[r0:r0 + nq, :] + o

    run_branch(sk_ref, sv_ref, False, count_ref[step], lambda t: tiles_ref[step * tiles_per_step + t],
               NSA_HEADS, True)
    run_branch(wk_ref, wv_ref, True, jnp.minimum(i, WINDOW // LANES) + 1, lambda t: i - t,
               2 * NSA_HEADS, False)

    for c in range(HEADS_PER_GROUP):
        a = tot_s[c * nq:(c + 1) * nq, :]
        b = tot_s[(HEADS_PER_GROUP + c) * nq:(HEADS_PER_GROUP + c + 1) * nq, :]
        o_ref[0, :, c * LANES:(c + 1) * LANES] = oc_ref[0, :, c * LANES:(c + 1) * LANES] + jnp.where(low, a, b)


def _tile_schedule(sel):
    n_b, _, t, n_blk = sel.shape
    nqb = t // LANES
    per_tile = LANES // SEL_BLOCK
    used = jnp.max(sel.reshape(n_b, NSA_KV_GROUPS, nqb, LANES, n_blk // per_tile, per_tile), axis=(1, 3, 5)) > 0
    kb = jnp.arange(n_blk // per_tile, dtype=jnp.int32)
    used = jnp.logical_and(used[:, :, :nqb], kb[None, None, :nqb] <= jnp.arange(nqb, dtype=jnp.int32)[None, :, None])
    order = -jnp.sort(-jnp.where(used, kb[None, None, :nqb], -1), axis=-1)
    return jnp.maximum(order, 0).reshape(-1).astype(jnp.int32), jnp.sum(used, axis=-1).reshape(-1).astype(jnp.int32)


def _slcwin_prompt(q, gates, oc, sel, sk, sv, wk, wv):
    n_b, t, _ = q.shape
    nqb = t // LANES
    n_rows = NSA_HEADS * LANES
    tiles, counts = _tile_schedule(sel)
    per_q = lambda b, i, *_: (b, i, 0)
    per_b = lambda b, i, *_: (b, 0, 0)
    grid_spec = pltpu.PrefetchScalarGridSpec(
        num_scalar_prefetch=2,
        grid=(n_b, nqb),
        in_specs=[pl.BlockSpec((1, LANES, 4 * LANES), per_q),
                  pl.BlockSpec((1, LANES, LANES), per_q),
                  pl.BlockSpec((1, LANES, 4 * LANES), per_q),
                  pl.BlockSpec((1, NSA_KV_GROUPS, LANES, N_SEL_LANES), lambda b, i, *_: (b, 0, i, 0)),
                  pl.BlockSpec((1, t, LANES), per_b),
                  pl.BlockSpec((1, t, 2 * LANES), per_b),
                  pl.BlockSpec((1, t, LANES), per_b),
                  pl.BlockSpec((1, t, 2 * LANES), per_b)],
        out_specs=pl.BlockSpec((1, LANES, 4 * LANES), per_q),
        scratch_shapes=[pltpu.VMEM((n_rows, LANES), BF16),
                        pltpu.VMEM((n_rows, LANES), F32),
                        pltpu.VMEM((n_rows, LANES), F32),
                        pltpu.VMEM((n_rows, LANES), F32)],
    )
    return pl.pallas_call(
        _slcwin_prompt_kernel,
        grid_spec=grid_spec,
        out_shape=jax.ShapeDtypeStruct((n_b, t, 4 * LANES), F32),
        compiler_params=pltpu.CompilerParams(dimension_semantics=("arbitrary", "arbitrary"),
                                             vmem_limit_bytes=VMEM_LIMIT),
        name="slcwin_prompt",
    )(tiles, counts, q, gates, oc, sel, sk, sv, wk, wv)


def _sb_prompt_kernel(q_ref, k_ref, v_ref, u_ref, o_ref, qm_s, carry_s, acc_s):
    i = pl.program_id(1)
    nq = LANES
    n_pairs = SB_KV_HEADS // 2
    rows = 4 * nq
    lane = lax.broadcasted_iota(jnp.int32, (rows, LANES), 1)
    qrow = lax.broadcasted_iota(jnp.int32, (rows, LANES), 0) & (nq - 1)
    low = lax.broadcasted_iota(jnp.int32, (nq, LANES), 1) < HEAD_DIM
    qpos = i * nq + qrow
    for kp in range(n_pairs):
        for r in range(2):
            ch = 2 * kp + r
            qc = q_ref[0, :, ch * LANES:(ch + 1) * LANES]
            zero = jnp.zeros_like(qc)
            qm_s[kp, (2 * r) * nq:(2 * r + 1) * nq, :] = jnp.where(low, qc, zero)
            qm_s[kp, (2 * r + 1) * nq:(2 * r + 2) * nq, :] = jnp.where(low, zero, qc)
    carry_s[...] = jnp.zeros(carry_s.shape, F32)
    acc_s[...] = jnp.zeros(acc_s.shape, F32)
    u = u_ref[...]

    def tile(kb, diag):
        k0 = pl.multiple_of(kb * LANES, LANES)
        causal = (k0 + lane) < qpos if diag else None
        for kp in range(n_pairs):
            kt = k_ref[0, pl.ds(k0, LANES), kp * LANES:(kp + 1) * LANES]
            vt = v_ref[0, pl.ds(k0, LANES), kp * LANES:(kp + 1) * LANES]
            z = _dot_nt(qm_s[kp], kt)
            a, tile_sum = _sb_weights(z, causal, carry_s[kp], u)
            acc_s[kp] = acc_s[kp] + _dot(a.astype(BF16), vt)
            carry_s[kp] = carry_s[kp] + tile_sum

    def live():
        c = carry_s[0]
        for kp in range(1, n_pairs):
            c = jnp.maximum(c, carry_s[kp])
        return jnp.max(c) > EXP_UNDERFLOW

    _tile_sweep(i + 1, lambda t: i - t, tile, live)
    for kp in range(n_pairs):
        for r in range(2):
            ch = 2 * kp + r
            o_ref[0, :, ch * LANES:(ch + 1) * LANES] = jnp.where(
                low, acc_s[kp, (2 * r) * nq:(2 * r + 1) * nq, :], acc_s[kp, (2 * r + 1) * nq:(2 * r + 2) * nq, :])


def _sb_prompt(qs, sbk, sbv, u):
    n_b, t, _ = qs.shape
    nqb = t // LANES
    n_pairs = SB_KV_HEADS // 2
    per_q = lambda b, i: (b, i, 0)
    per_b = lambda b, i: (b, 0, 0)
    return pl.pallas_call(
        _sb_prompt_kernel,
        grid=(n_b, nqb),
        in_specs=[pl.BlockSpec((1, LANES, 4 * LANES), per_q),
                  pl.BlockSpec((1, t, n_pairs * LANES), per_b),
                  pl.BlockSpec((1, t, n_pairs * LANES), per_b),
                  pl.BlockSpec((LANES, LANES), lambda b, i: (0, 0))],
        out_specs=pl.BlockSpec((1, LANES, 4 * LANES), per_q),
        out_shape=jax.ShapeDtypeStruct((n_b, t, 4 * LANES), F32),
        scratch_shapes=[pltpu.VMEM((n_pairs, 4 * LANES, LANES), BF16),
                        pltpu.VMEM((n_pairs, 4 * LANES, LANES), F32),
                        pltpu.VMEM((n_pairs, 4 * LANES, LANES), F32)],
        compiler_params=pltpu.CompilerParams(dimension_semantics=("arbitrary", "arbitrary"),
                                             vmem_limit_bytes=VMEM_LIMIT),
        name="sb_prompt",
    )(qs, sbk, sbv, u)


def _final_kernel(x_ref, oa_ref, sza_ref, ob_ref, szb_ref, w_ref, y_ref):
    half = oa_ref.shape[1]
    ma = (oa_ref[...] * sza_ref[...]).astype(BF16)
    mb = (ob_ref[...] * szb_ref[...]).astype(BF16)
    y_ref[...] = x_ref[...] + _dot(ma, w_ref[0:half, :]) + _dot(mb, w_ref[half:2 * half, :])


def _final(x2, oa, sza, ob, szb, w_out_perm, tm):
    rows, d = x2.shape
    half = oa.shape[1]
    row = lambda i: (i, 0)
    return pl.pallas_call(
        _final_kernel,
        grid=(rows // tm,),
        in_specs=[pl.BlockSpec((tm, d), row), pl.BlockSpec((tm, half), row),
                  pl.BlockSpec((tm, half), row), pl.BlockSpec((tm, half), row),
                  pl.BlockSpec((tm, half), row), pl.BlockSpec((d, d), lambda i: (0, 0))],
        out_specs=pl.BlockSpec((tm, d), row),
        out_shape=jax.ShapeDtypeStruct((rows, d), F32),
        compiler_params=pltpu.CompilerParams(dimension_semantics=("arbitrary",),
                                             vmem_limit_bytes=VMEM_LIMIT),
        name="final",
    )(x2, oa, sza, ob, szb, w_out_perm)


Q_PAD = SUBLANES
Q_SHIFT = 3
assert 1 << Q_SHIFT == Q_PAD
SAMPLE_ROWS = NSA_HEADS * Q_PAD


def _row_ids(shape):
    r = lax.broadcasted_iota(jnp.int32, shape, 0)
    return r >> Q_SHIFT, r & (Q_PAD - 1)


def _head_slopes(shape):
    hidx, _ = _row_ids(shape)
    out = jnp.zeros(shape, F32)
    for h in range(NSA_HEADS):
        out = jnp.where(hidx == h, _slope(h), out)
    return out


def _stack_nsa_queries(q):
    lane = lax.broadcasted_iota(jnp.int32, (Q_PAD, LANES), 1)
    low = lane < HEAD_DIM
    parts = []
    for g in range(NSA_KV_GROUPS):
        keep = low if g == 0 else jnp.logical_not(low)
        for c in range(HEADS_PER_GROUP):
            parts.append(jnp.where(keep, q[:, c * LANES:(c + 1) * LANES], 0.0))
    return jnp.concatenate(parts, axis=0).astype(BF16)


def _cmp_sample_kernel(q_ref, ck_ref, cv_ref, selmap_ref, eall_ref, ocmp_ref, mexp_ref, v_s, sel_s,
                       *, past_len, n_pick):
    n_seq = q_ref.shape[0]
    n_rows = SAMPLE_ROWS
    half = n_rows // 2
    n_c = ck_ref.shape[1]
    crow = lax.broadcasted_iota(jnp.int32, (n_rows, n_c), 1)
    cend = crow * CMP_STRIDE + (CMP_STRIDE - 1)
    cq = past_len + _row_ids((n_rows, n_c))[1]
    cmask = jnp.logical_and(cend <= cq, crow >= 1)
    slope = _head_slopes((n_rows, n_c))
    crel = (cq - cend).astype(F32)
    n_past_blocks = past_len // SEL_BLOCK
    lane = lax.broadcasted_iota(jnp.int32, (NSA_KV_GROUPS * Q_PAD, N_SEL_LANES), 1)
    forced = jnp.logical_or(lane == 0, lane == n_past_blocks - 1)
    allowed = lane < n_past_blocks
    rows_seq = NSA_KV_GROUPS * Q_PAD
    for b in range(n_seq):
        p = _cmp_probs(_stack_nsa_queries(q_ref[b]), ck_ref[b], slope, crel, cmask)
        pb = p.astype(BF16)
        ocmp_ref[b, 0:half, :] = _dot(pb[0:half], cv_ref[b, :, 0:LANES])
        ocmp_ref[b, half:n_rows, :] = _dot(pb[half:n_rows], cv_ref[b, :, LANES:2 * LANES])
        pg = []
        for g in range(NSA_KV_GROUPS):
            acc = jnp.zeros((Q_PAD, n_c), F32)
            for c in range(HEADS_PER_GROUP):
                r0 = (g * HEADS_PER_GROUP + c) * Q_PAD
                acc = acc + p[r0:r0 + Q_PAD]
            pg.append(acc)
        imp = _dot_split(jnp.concatenate(pg, axis=0), selmap_ref[...])
        v_s[b * rows_seq:(b + 1) * rows_seq, :] = jnp.where(allowed, jnp.where(forced, BIG, imp), -BIG)
    lane_f = lax.broadcasted_iota(jnp.int32, v_s.shape, 1).astype(F32)
    _pick_blocks(v_s, sel_s, lane_f, n_pick)
    allowed_all = lax.broadcasted_iota(jnp.int32, v_s.shape, 1) < n_past_blocks
    sel = jnp.where(allowed_all, sel_s[...], 0.0).astype(BF16)
    mexp = _dot(sel, eall_ref[...])
    for b in range(n_seq):
        mexp_ref[b] = mexp[b * rows_seq:(b + 1) * rows_seq]


def _cmp_sample(q8, ck, cv, selmap, eall, past_len, n_pick, n_seq):
    n_b = q8.shape[0]
    n_c = ck.shape[1]
    rows_seq = NSA_KV_GROUPS * Q_PAD
    kern = functools.partial(_cmp_sample_kernel, past_len=past_len, n_pick=n_pick)
    blk = lambda b: (b, 0, 0)
    return pl.pallas_call(
        kern,
        grid=(n_b // n_seq,),
        in_specs=[pl.BlockSpec((n_seq, Q_PAD, 4 * LANES), blk),
                  pl.BlockSpec((n_seq, n_c, LANES), blk),
                  pl.BlockSpec((n_seq, n_c, 2 * LANES), blk),
                  pl.BlockSpec((n_c, N_SEL_LANES), lambda b: (0, 0)),
                  pl.BlockSpec((N_SEL_LANES, past_len), lambda b: (0, 0))],
        out_specs=[pl.BlockSpec((n_seq, SAMPLE_ROWS, LANES), blk),
                   pl.BlockSpec((n_seq, rows_seq, past_len), blk)],
        out_shape=[jax.ShapeDtypeStruct((n_b, SAMPLE_ROWS, LANES), F32),
                   jax.ShapeDtypeStruct((n_b, rows_seq, past_len), F32)],
        scratch_shapes=[pltpu.VMEM((n_seq * rows_seq, N_SEL_LANES), F32),
                        pltpu.VMEM((n_seq * rows_seq, N_SEL_LANES), F32)],
        compiler_params=pltpu.CompilerParams(dimension_semantics=("arbitrary",),
                                             vmem_limit_bytes=VMEM_LIMIT),
        name="cmp_sample",
    )(q8, ck, cv, selmap, eall)


def _pad_keys(x):
    return jnp.concatenate([x, jnp.zeros((LANES - Q_PAD, x.shape[1]), F32)], axis=0)


def _value_variants(v, axis):
    first = lax.broadcasted_iota(jnp.int32, v.shape, axis) < HEAD_DIM
    one = jnp.ones_like(v)
    return jnp.where(first, v, one).astype(BF16), jnp.where(first, one, v).astype(BF16)


def _attn_sample_kernel(*refs, past_len, dec_seq, n_pages_step):
    refs = refs[1:]
    (qa_ref, qs_ref, gate_ref, ocmp_ref, mexp_ref, slcn_ref, winn_ref, sbn_ref,
     state_ref, u_ref) = refs[:10]
    slc_pages = refs[10:10 + n_pages_step]
    sb_pages = refs[10 + n_pages_step:10 + 2 * n_pages_step]
    oa_ref, ob_ref = refs[10 + 2 * n_pages_step:12 + 2 * n_pages_step]
    qa_s, qb_s, m_s, acc_s, win_s, carry_s, accb_s = refs[12 + 2 * n_pages_step:]

    j = pl.program_id(1)
    n_steps = pl.num_programs(1)
    group = n_steps - 1 - j
    n_rows = SAMPLE_ROWS
    half = n_rows // 2
    lane = lax.broadcasted_iota(jnp.int32, (n_rows, LANES), 1)
    low = lane < HEAD_DIM
    t_q = _row_ids((n_rows, LANES))[1]
    qpos = past_len + t_q
    slope = _head_slopes((n_rows, LANES))
    u = u_ref[...]

    @pl.when(j == 0)
    def _():
        qa = _stack_nsa_queries(qa_ref[0])
        qa_s[...] = qa
        lane8 = lax.broadcasted_iota(jnp.int32, (Q_PAD, LANES), 1)
        low8 = lane8 < HEAD_DIM
        parts = []
        for k in range(SB_KV_HEADS):
            for r in range(SB_HEADS // SB_KV_HEADS):
                ch = 2 * (k // 2) + r
                qc = qs_ref[0, :, ch * LANES:(ch + 1) * LANES]
                keep = low8 if k % 2 == 0 else jnp.logical_not(low8)
                piece = jnp.where(keep, qc, 0.0)
                zero = jnp.zeros_like(piece)
                parts.append(jnp.concatenate([piece, zero] if k // 2 == 0 else [zero, piece], axis=1))
        qb = jnp.concatenate(parts, axis=0).astype(BF16)
        qb_s[...] = qb

        new_valid = lane < dec_seq
        slcn = _pad_keys(slcn_ref[0])
        smask = jnp.logical_and(new_valid, lane <= t_q)
        s = _dot_nt(qa, slcn[:, 0:LANES].astype(BF16)) - slope * (t_q - lane).astype(F32)
        s = jnp.where(smask, s, -BIG)
        m0 = jnp.max(s, axis=1, keepdims=True)
        p = jnp.where(smask, jnp.exp(s - m0), 0.0).astype(BF16)
        v0, v1 = _value_variants(slcn[:, LANES:2 * LANES], 1)
        m_s[...] = jnp.broadcast_to(m0, (n_rows, LANES))
        acc_s[0:half, :] = _dot(p[0:half], v0)
        acc_s[half:n_rows, :] = _dot(p[half:n_rows], v1)

        sbn = _pad_keys(sbn_ref[0])
        z = _dot_nt(qb, sbn[:, 0:2 * LANES].astype(BF16))
        a, tile_sum = _sb_weights(z, jnp.logical_and(new_valid, lane < t_q), 0.0, u)
        accb_s[...] = _dot(a.astype(BF16), sbn[:, 2 * LANES:4 * LANES].astype(BF16))
        carry_s[...] = jnp.broadcast_to(tile_sum, (n_rows, LANES))

        n_state = state_ref.shape[2]
        winn = _pad_keys(winn_ref[0])
        n_k = n_state + LANES
        col = lax.broadcasted_iota(jnp.int32, (n_rows, n_k), 1)
        kpos = jnp.where(col < n_state, past_len - n_state + col, past_len + col - n_state)
        rel = past_len + _row_ids((n_rows, n_k))[1] - kpos
        wmask = jnp.logical_and(jnp.logical_and(rel >= 0, rel < WINDOW), col < n_state + dec_seq)
        qk = jnp.concatenate([_dot(qa, state_ref[0, 0:LANES, :].astype(BF16)),
                              _dot_nt(qa, winn[:, 0:LANES].astype(BF16))], axis=1)
        s = qk - _head_slopes((n_rows, n_k)) * rel.astype(F32)
        s = jnp.where(wmask, s, -BIG)
        m = jnp.max(s, axis=1, keepdims=True)
        e = jnp.where(wmask, jnp.exp(s - m), 0.0)
        p = (e / jnp.maximum(jnp.sum(e, axis=1, keepdims=True), 1e-30)).astype(BF16)
        win_s[...] = (_dot_nt(p[:, 0:n_state], state_ref[0, LANES:2 * LANES, :].astype(BF16))
                      + _dot(p[:, n_state:n_k], winn[:, LANES:2 * LANES].astype(BF16)))

    k_base = group * (n_pages_step * LANES)

    @pl.when(jnp.max(mexp_ref[0]) > 0.5)
    def _():
        qa = qa_s[...]
        ss = []
        for pi in range(n_pages_step):
            msk = mexp_ref[0, :, pi * LANES:(pi + 1) * LANES]
            bias16 = (msk - 1.0) * BIG
            bias = jnp.concatenate(
                [bias16[0:Q_PAD]] * HEADS_PER_GROUP + [bias16[Q_PAD:2 * Q_PAD]] * HEADS_PER_GROUP, axis=0)
            kpos = k_base + pi * LANES + lane
            qk = _dot(qa, slc_pages[pi][0, 0:LANES, :].astype(BF16))
            ss.append((qk - slope * (qpos - kpos).astype(F32)) + bias)
        m_old = m_s[...]
        m_new = m_old
        for s in ss:
            m_new = jnp.maximum(m_new, jnp.max(s, axis=1, keepdims=True))
        pv0 = jnp.zeros((half, LANES), F32)
        pv1 = jnp.zeros((half, LANES), F32)
        for pi in range(n_pages_step):
            p = jnp.exp(ss[pi] - m_new).astype(BF16)
            v0, v1 = _value_variants(slc_pages[pi][0, LANES:2 * LANES, :], 0)
            pv0 = pv0 + _dot_nt(p[0:half], v0)
            pv1 = pv1 + _dot_nt(p[half:n_rows], v1)
        alpha = jnp.exp(m_old - m_new)
        acc_s[0:half, :] = alpha[0:half] * acc_s[0:half, :] + pv0
        acc_s[half:n_rows, :] = alpha[half:n_rows] * acc_s[half:n_rows, :] + pv1
        m_s[...] = m_new

    @pl.when(jnp.max(carry_s[...]) > EXP_UNDERFLOW)
    def _():
        qb = qb_s[...]
        zs, sps, sums = [], [], []
        for pi in range(n_pages_step):
            z = _dot(qb, sb_pages[pi][0, 0:2 * LANES, :].astype(BF16))
            sp = _softplus(z)
            zs.append(z)
            sps.append(sp)
            sums.append(jnp.sum(-sp, axis=1, keepdims=True))
        carry = carry_s[...]
        accb = accb_s[...]
        for pi in range(n_pages_step - 1, -1, -1):
            a = jnp.exp((zs[pi] - sps[pi]) + _dot_split(-sps[pi], u) + carry)
            accb = accb + _dot_nt(a.astype(BF16), sb_pages[pi][0, 2 * LANES:4 * LANES, :].astype(BF16))
            carry = carry + sums[pi]
        accb_s[...] = accb
        carry_s[...] = carry

    @pl.when(j == n_steps - 1)
    def _():
        o_slc = _normalise(acc_s[...])
        o_win = win_s[...]
        o_cmp = ocmp_ref[0]
        for c in range(HEADS_PER_GROUP):
            tot = []
            for g in range(NSA_KV_GROUPS):
                h = g * HEADS_PER_GROUP + c
                r0 = h * Q_PAD
                gc = gate_ref[0, :, h:h + 1]
                gs = gate_ref[0, :, NSA_HEADS + h:NSA_HEADS + h + 1]
                gw = gate_ref[0, :, 2 * NSA_HEADS + h:2 * NSA_HEADS + h + 1]
                tot.append(gc * o_cmp[r0:r0 + Q_PAD] + gs * o_slc[r0:r0 + Q_PAD] + gw * o_win[r0:r0 + Q_PAD])
            oa_ref[0, :, c * LANES:(c + 1) * LANES] = jnp.where(low[0:Q_PAD], tot[0], tot[1])
        acc_b = accb_s[...]
        for kp in range(SB_KV_HEADS // 2):
            for r in range(SB_HEADS // SB_KV_HEADS):
                ra = ((2 * kp) * 2 + r) * Q_PAD
                rb = ((2 * kp + 1) * 2 + r) * Q_PAD
                a0 = acc_b[ra:ra + Q_PAD, kp * LANES:(kp + 1) * LANES]
                a1 = acc_b[rb:rb + Q_PAD, kp * LANES:(kp + 1) * LANES]
                ch = 2 * kp + r
                ob_ref[0, :, ch * LANES:(ch + 1) * LANES] = jnp.where(low[0:Q_PAD], a0, a1)


def _attn_sample(table, qa8, qs8, gate8, ocmp, mexp, slcn, winn, sbn, state, u,
                 slc_pool, sb_pool, n_pages, pages_per_step, past_len, dec_seq):
    n_b = qa8.shape[0]
    n_rows = SAMPLE_ROWS
    nj = n_pages // pages_per_step
    n_state = state.shape[2]
    kern = functools.partial(_attn_sample_kernel, past_len=past_len, dec_seq=dec_seq,
                             n_pages_step=pages_per_step)
    per_b = lambda b, j, t: (b, 0, 0)

    def page_spec(channels, p):
        return pl.BlockSpec(
            (1, channels, LANES),
            lambda b, j, t: (t[b * n_pages + (nj - 1 - j) * pages_per_step + p], 0, 0))

    grid_spec = pltpu.PrefetchScalarGridSpec(
        num_scalar_prefetch=1,
        grid=(n_b, nj),
        in_specs=[pl.BlockSpec((1, Q_PAD, 4 * LANES), per_b),
                  pl.BlockSpec((1, Q_PAD, 4 * LANES), per_b),
                  pl.BlockSpec((1, Q_PAD, LANES), per_b),
                  pl.BlockSpec((1, n_rows, LANES), per_b),
                  pl.BlockSpec((1, 2 * Q_PAD, pages_per_step * LANES), lambda b, j, t: (b, 0, nj - 1 - j)),
                  pl.BlockSpec((1, Q_PAD, 2 * LANES), per_b),
                  pl.BlockSpec((1, Q_PAD, 2 * LANES), per_b),
                  pl.BlockSpec((1, Q_PAD, 4 * LANES), per_b),
                  pl.BlockSpec((1, 2 * LANES, n_state), per_b),
                  pl.BlockSpec((LANES, LANES), lambda b, j, t: (0, 0))]
        + [page_spec(2 * LANES, p) for p in range(pages_per_step)]
        + [page_spec(4 * LANES, p) for p in range(pages_per_step)],
        out_specs=[pl.BlockSpec((1, Q_PAD, 4 * LANES), per_b),
                   pl.BlockSpec((1, Q_PAD, 4 * LANES), per_b)],
        scratch_shapes=[pltpu.VMEM((n_rows, LANES), BF16),
                        pltpu.VMEM((n_rows, 2 * LANES), BF16),
                        pltpu.VMEM((n_rows, LANES), F32),
                        pltpu.VMEM((n_rows, LANES), F32),
                        pltpu.VMEM((n_rows, LANES), F32),
                        pltpu.VMEM((n_rows, LANES), F32),
                        pltpu.VMEM((n_rows, 2 * LANES), F32)],
    )
    return pl.pallas_call(
        kern,
        grid_spec=grid_spec,
        out_shape=[jax.ShapeDtypeStruct((n_b, Q_PAD, 4 * LANES), F32),
                   jax.ShapeDtypeStruct((n_b, Q_PAD, 4 * LANES), F32)],
        compiler_params=pltpu.CompilerParams(dimension_semantics=("arbitrary", "arbitrary"),
                                             vmem_limit_bytes=VMEM_LIMIT),
        name="attn_sample",
    )(table, qa8, qs8, gate8, ocmp, mexp, slcn, winn, sbn, state, u,
      *([slc_pool] * pages_per_step), *([sb_pool] * pages_per_step))


def _head_index(base, perm):
    return np.concatenate([np.arange(base + h * HEAD_DIM, base + (h + 1) * HEAD_DIM) for h in perm])


def _largest_divisor(n, cap):
    d = min(n, cap)
    while n % d:
        d -= 1
    return d


def kernel(x_prompt, x_sample, cache_cmp_kv, cache_slc_kv, cache_sb_kv, state_win_kv, page_table,
           norm_g, w_in, q_norm_g, k_norm_g, cmp_pe, cmp_w1, cmp_b1, cmp_w2, w_out):
    n_b, t, d = x_prompt.shape
    n_db, dec_seq, _ = x_sample.shape
    depth = w_in.shape[0]
    assert depth == 1, "single-layer step"
    n_pages = page_table.shape[1]
    page = cache_cmp_kv.shape[2]
    past_len = n_pages * page
    n_state = state_win_kv.shape[2]
    assert page == LANES and t % LANES == 0 and t // SEL_BLOCK <= N_SEL_LANES
    assert past_len // SEL_BLOCK <= N_SEL_LANES and dec_seq <= Q_PAD and dec_seq <= SEL_BLOCK
    assert n_state == WINDOW and past_len >= WINDOW and t >= WINDOW

    nsa_w = NSA_HEADS * HEAD_DIM
    kvw = NSA_KV_GROUPS * HEAD_DIM
    sb_w = SB_HEADS * HEAD_DIM
    sbkv_w = SB_KV_HEADS * HEAD_DIM
    o_kv = nsa_w
    o_gt = o_kv + 6 * kvw
    o_za = o_gt + N_BRANCH * NSA_HEADS
    o_qs = o_za + nsa_w
    o_kb = o_qs + sb_w
    o_zb = o_kb + 2 * sbkv_w
    w0 = w_in[0]
    cols = np.concatenate([
        _head_index(0, PERM_A), np.arange(o_kv, o_gt), _head_index(o_za, PERM_A),
        _head_index(o_qs, PERM_B), np.arange(o_kb, o_zb), _head_index(o_zb, PERM_B),
        np.arange(o_gt, o_za)])
    w_perm = jnp.pad(w0[:, cols], ((0, 0), (0, LANES - N_BRANCH * NSA_HEADS))).astype(BF16)
    rows_out = np.concatenate([_head_index(0, PERM_A), _head_index(nsa_w, PERM_B)])
    w_out_perm = w_out[0][rows_out, :].astype(BF16)

    seg = jnp.asarray(np.kron(np.eye(2), np.ones((HEAD_DIM, HEAD_DIM))), BF16)
    tile2 = lambda v: jnp.tile(v, 2).reshape(1, LANES)
    gq = tile2(q_norm_g[0])
    gkc, gks, gkw = tile2(k_norm_g[0, 0]), tile2(k_norm_g[0, 1]), tile2(k_norm_g[0, 2])
    g_in = norm_g[0].reshape(1, d)

    w1 = cmp_w1[0].reshape(2, 2, CMP_STRIDE, HEAD_DIM, CMP_HIDDEN)
    w1 = jnp.transpose(w1, (0, 2, 3, 1, 4))
    zeros = jnp.zeros_like(w1)
    wc = jnp.stack([jnp.concatenate([w1, zeros], axis=3), jnp.concatenate([zeros, w1], axis=3)], axis=2)
    wc = wc.reshape(2, CMP_STRIDE // 2, 2 * LANES, 4 * CMP_HIDDEN).astype(BF16)
    w2 = cmp_w2[0]
    z2 = jnp.zeros_like(w2)
    w2p = jnp.stack([jnp.concatenate([w2, z2], axis=2), jnp.concatenate([z2, w2], axis=2)], axis=1).astype(BF16)
    cb = _cbias(cmp_pe[0].reshape(2, 1, CMP_LEN * HEAD_DIM), cmp_w1[0],
                cmp_b1[0].reshape(2, 1, CMP_HIDDEN))

    def selection_map(n_rows):
        r = np.arange(n_rows)[:, None]
        jb = np.arange(N_SEL_LANES)[None, :]
        start = (r - 1) * CMP_STRIDE
        ok = (r >= 1) & (start < jb * SEL_BLOCK + SEL_BLOCK) & (start + CMP_LEN > jb * SEL_BLOCK)
        return jnp.asarray(ok, BF16)

    u = jnp.asarray(np.arange(LANES)[:, None] > np.arange(LANES)[None, :], BF16)

    tm = _largest_divisor(n_b * t, 256)
    xp2 = x_prompt.reshape(n_b * t, d)
    (q16, cmpkv, slckv, winkv, slck, slcv, wink, winv, gates, sza, qs16, sbkv, sbk, sbv, szb) = _proj(
        xp2, g_in, w_perm, gq, gks, gkw, seg, tm)
    n_sub = t // CMP_STRIDE
    sub_w = CMP_STRIDE * 2 * kvw
    ck, cv = _compress_prompt(cmpkv.reshape(n_b * n_sub, sub_w), wc, cb, w2p, gkc, seg,
                              n_b, _largest_divisor(n_sub, 64))
    r3 = lambda a: a.reshape(n_b, t, a.shape[-1])
    oc, sel = _cmp_prompt(r3(q16), r3(gates), ck.reshape(n_b, n_sub, LANES), cv.reshape(n_b, n_sub, 2 * LANES),
                          selection_map(n_sub), _largest_divisor(t, CMP_Q_ROWS))
    oa = _slcwin_prompt(r3(q16), r3(gates), oc, sel, r3(slck), r3(slcv), r3(wink), r3(winv))
    ob = _sb_prompt(r3(qs16), r3(sbk), r3(sbv), u)
    y_prompt = _final(xp2, oa.reshape(n_b * t, nsa_w), sza, ob.reshape(n_b * t, sb_w), szb,
                      w_out_perm, tm).reshape(n_b, t, d)
    kv5 = lambda a, n, heads: a.reshape(1, n, -1, 2, heads, HEAD_DIM)
    cmp_p = kv5(cmpkv, n_b, NSA_KV_GROUPS)
    slc_p = kv5(slckv, n_b, NSA_KV_GROUPS)
    sb_p = kv5(sbkv, n_b, SB_KV_HEADS)
    win_p = kv5(winkv, n_b, NSA_KV_GROUPS)[:, :, t - min(WINDOW, t):]

    rows_s = n_db * dec_seq
    tms = _largest_divisor(rows_s, 256)
    xs2 = x_sample.reshape(rows_s, d)
    (q16s, cmpkv_s, slckv_s, winkv_s, _, _, _, _, gates_s, sza_s, qs16s, sbkv_s, _, _, szb_s) = _proj(
        xs2, g_in, w_perm, gq, gks, gkw, seg, tms)
    table = page_table.reshape(-1).astype(jnp.int32)
    n_sub_s = n_pages * (page // CMP_STRIDE)
    chan_major = lambda a: jnp.transpose(a, (0, 2, 3, 4, 1)).reshape(a.shape[0], -1, a.shape[1])
    cks, cvs = _compress_sample(table, chan_major(cache_cmp_kv[0]), wc, cb, w2p, gkc, seg,
                                n_db, n_pages, _largest_divisor(n_pages, COMPRESS_PAGES))
    pad8 = lambda a: jnp.pad(a.reshape(n_db, dec_seq, a.shape[-1]).astype(F32),
                             ((0, 0), (0, Q_PAD - dec_seq), (0, 0)))
    qa8, qs8, gate8 = pad8(q16s), pad8(qs16s), pad8(gates_s)
    n_blocks = -(-(past_len + dec_seq) // SEL_BLOCK)
    n_pick = min(SEL_TOPK, n_blocks) - 1
    eall = jnp.asarray(np.arange(N_SEL_LANES)[:, None] == (np.arange(past_len)[None, :] // SEL_BLOCK), BF16)
    ocmp, mexp = _cmp_sample(qa8, cks.reshape(n_db, n_sub_s, LANES), cvs.reshape(n_db, n_sub_s, 2 * LANES),
                             selection_map(n_sub_s), eall, past_len, n_pick,
                             _largest_divisor(n_db, CMP_SAMPLE_SEQS))
    oa8, ob8 = _attn_sample(table, qa8, qs8, gate8, ocmp, mexp, pad8(slckv_s), pad8(winkv_s), pad8(sbkv_s),
                            chan_major(state_win_kv[0]), u,
                            chan_major(cache_slc_kv[0]), chan_major(cache_sb_kv[0]),
                            n_pages, _largest_divisor(n_pages, ATTN_PAGES), past_len, dec_seq)
    y_sample = _final(xs2, oa8[:, :dec_seq].reshape(rows_s, nsa_w), sza_s,
                      ob8[:, :dec_seq].reshape(rows_s, sb_w), szb_s, w_out_perm, tms).reshape(n_db, dec_seq, d)
    kv5s = lambda a, heads: a.reshape(1, n_db, dec_seq, 2, heads, HEAD_DIM)
    cmp_s = kv5s(cmpkv_s, NSA_KV_GROUPS)
    slc_s = kv5s(slckv_s, NSA_KV_GROUPS)
    sb_s = kv5s(sbkv_s, SB_KV_HEADS)
    win_new = kv5s(winkv_s, NSA_KV_GROUPS)
    win_s = jnp.concatenate([state_win_kv[:, :, dec_seq:], win_new], axis=2)

    return (y_prompt, y_sample, cmp_p, cmp_s, slc_p, slc_s, sb_p, sb_s, win_p, win_s)
```

```python
import functools

import numpy as np
import jax
import jax.numpy as jnp
from jax import lax
from jax.experimental import pallas as pl
from jax.experimental.pallas import tpu as pltpu

F32 = jnp.float32
BF16 = jnp.bfloat16

HEAD_DIM = 64
NSA_HEADS = 8
SB_HEADS = 8
NSA_KV_GROUPS = 2
SB_KV_HEADS = 4
HEADS_PER_GROUP = NSA_HEADS // NSA_KV_GROUPS
N_BRANCH = 3
CMP_LEN = 32
CMP_STRIDE = 16
CMP_HIDDEN = 128
SEL_BLOCK = 64
SEL_TOPK = 16
WINDOW = 512
RMS_EPS = 1e-6
BIG = 1e30
PICKED = -3e38
EXP_UNDERFLOW = -104.0
SCALE = HEAD_DIM ** -0.5
SEL_SHIFT = 6
assert 1 << SEL_SHIFT == SEL_BLOCK

LANES = 128
SUBLANES = 8
N_SEL_LANES = 128
VMEM_LIMIT = 52 * 1024 * 1024
CMP_Q_ROWS = 512
CMP_SAMPLE_SEQS = 8
COMPRESS_PAGES = 32
ATTN_PAGES = 8

PERM_A = (0, 4, 1, 5, 2, 6, 3, 7)
PERM_B = (0, 2, 1, 3, 4, 6, 5, 7)


def _slope(h):
    return float(2.0 ** (-(h + 1)))


def _dot(a, b):
    return jnp.dot(a, b, preferred_element_type=F32)


def _dot_nt(a, b):
    return lax.dot_general(a, b, (((1,), (1,)), ((), ())), preferred_element_type=F32)


def _dot_split(x, m):
    hi = x.astype(BF16)
    lo = (x - hi.astype(F32)).astype(BF16)
    return _dot(hi, m) + _dot(lo, m)


def _softplus(z):
    return jnp.maximum(z, 0.0) + jnp.log(1.0 + jnp.exp(-jnp.abs(z)))


def _silu(z):
    return z * jax.nn.sigmoid(z)


def _proj_kernel(x_ref, g_ref, w_ref, gq_ref, gks_ref, gkw_ref, seg_ref,
                 q_ref, cmp_ref, slc_ref, win_ref, slck_ref, slcv_ref, wink_ref, winv_ref,
                 gate_ref, sza_ref, qs_ref, sb_ref, sbk_ref, sbv_ref, szb_ref):
    x = x_ref[...]
    ms = jnp.mean(x * x, axis=-1, keepdims=True)
    hn = ((x * lax.rsqrt(ms + RMS_EPS)) * g_ref[...]).astype(BF16)
    seg = seg_ref[...]
    lane = lax.broadcasted_iota(jnp.int32, (x.shape[0], LANES), 1)
    low = lane < HEAD_DIM

    def sec(a, n=LANES):
        return _dot(hn, w_ref[:, a:a + n])

    def headnorm(t, gain):
        ss = _dot_split(t * t, seg) * (1.0 / HEAD_DIM)
        return (t * lax.rsqrt(ss + RMS_EPS)) * gain

    def variants(v):
        one = jnp.ones_like(v)
        return jnp.where(low, v, one).astype(BF16), jnp.where(low, one, v).astype(BF16)

    for c in range(4):
        t = headnorm(sec(c * LANES), gq_ref[...])
        q_ref[:, c * LANES:(c + 1) * LANES] = (t * SCALE).astype(BF16)
    base = 4 * LANES
    cmp_ref[:, 0:LANES] = sec(base)
    cmp_ref[:, LANES:2 * LANES] = sec(base + LANES)
    for kv_out, kk_out, vv_out, gain_ref, off in (
            (slc_ref, slck_ref, slcv_ref, gks_ref, base + 2 * LANES),
            (win_ref, wink_ref, winv_ref, gkw_ref, base + 4 * LANES)):
        kn = headnorm(sec(off), gain_ref[...])
        v = sec(off + LANES)
        kv_out[:, 0:LANES] = kn
        kv_out[:, LANES:2 * LANES] = v
        kk_out[...] = kn.astype(BF16)
        v0, v1 = variants(v)
        vv_out[:, 0:LANES] = v0
        vv_out[:, LANES:2 * LANES] = v1
    base = 10 * LANES
    for c in range(4):
        sza_ref[:, c * LANES:(c + 1) * LANES] = _silu(sec(base + c * LANES))
    base = 14 * LANES
    for c in range(4):
        qs_ref[:, c * LANES:(c + 1) * LANES] = (sec(base + c * LANES) * SCALE).astype(BF16)
    base = 18 * LANES
    for c in range(2):
        kb = sec(base + c * LANES)
        vb = sec(base + (2 + c) * LANES)
        sb_ref[:, c * LANES:(c + 1) * LANES] = kb
        sb_ref[:, (2 + c) * LANES:(3 + c) * LANES] = vb
        sbk_ref[:, c * LANES:(c + 1) * LANES] = kb.astype(BF16)
        sbv_ref[:, c * LANES:(c + 1) * LANES] = vb.astype(BF16)
    base = 22 * LANES
    for c in range(4):
        szb_ref[:, c * LANES:(c + 1) * LANES] = _silu(sec(base + c * LANES))
    gate_ref[...] = jax.nn.sigmoid(sec(26 * LANES))


def _proj(x2, g, w_perm, gq, gks, gkw, seg, tm):
    rows, d = x2.shape
    wcols = w_perm.shape[1]
    row = lambda i: (i, 0)
    const = lambda i: (0, 0)
    widths = [(512, BF16), (256, F32), (256, F32), (256, F32), (128, BF16), (256, BF16),
              (128, BF16), (256, BF16), (128, F32), (512, F32), (512, BF16), (512, F32),
              (256, BF16), (256, BF16), (512, F32)]
    return pl.pallas_call(
        _proj_kernel,
        grid=(rows // tm,),
        in_specs=[pl.BlockSpec((tm, d), row), pl.BlockSpec((1, d), const),
                  pl.BlockSpec((d, wcols), const), pl.BlockSpec((1, LANES), const),
                  pl.BlockSpec((1, LANES), const), pl.BlockSpec((1, LANES), const),
                  pl.BlockSpec((LANES, LANES), const)],
        out_specs=[pl.BlockSpec((tm, w), row) for w, _ in widths],
        out_shape=[jax.ShapeDtypeStruct((rows, w), dt) for w, dt in widths],
        compiler_params=pltpu.CompilerParams(dimension_semantics=("arbitrary",),
                                             vmem_limit_bytes=VMEM_LIMIT),
        name="proj",
    )(x2, g, w_perm, gq, gks, gkw, seg)


def _cbias_kernel(pe_ref, w1_ref, b1_ref, o_ref):
    for kv in range(2):
        pe = jnp.broadcast_to(pe_ref[kv], (SUBLANES, pe_ref.shape[-1])).astype(BF16)
        o_ref[kv] = _dot(pe, w1_ref[kv].astype(BF16)) + b1_ref[kv]


def _cbias(pe_flat, w1, b1):
    return pl.pallas_call(
        _cbias_kernel,
        out_shape=jax.ShapeDtypeStruct((2, SUBLANES, CMP_HIDDEN), F32),
        name="cbias",
    )(pe_flat, w1, b1)


def _compress_kernel(*refs, n_in, has_table):
    if has_table:
        refs = refs[1:]
    x_refs = refs[:n_in]
    if has_table:
        perm_ref = refs[n_in]
        refs = refs[1:]
    wc_ref, cb_ref, w2_ref, gk_ref, seg_ref = refs[n_in:n_in + 5]
    ck_ref, cv_ref = refs[n_in + 5:n_in + 7]
    carry_ref = refs[n_in + 7]
    n_rows = ck_ref.shape[0]

    @pl.when(pl.program_id(1) == 0)
    def _():
        carry_ref[...] = jnp.zeros(carry_ref.shape, F32)

    row = lax.broadcasted_iota(jnp.int32, (n_rows, CMP_HIDDEN), 0)
    lane = lax.broadcasted_iota(jnp.int32, (n_rows, LANES), 1)
    low = lane < HEAD_DIM
    for kv in range(2):
        if has_table:
            perm = perm_ref[...]
            pages = [_dot_nt(perm, r[0, kv * LANES:(kv + 1) * LANES, :].astype(BF16)) for r in x_refs]
        def sub_rows(s):
            if has_table:
                nb = LANES // CMP_STRIDE
                return jnp.concatenate([pg[s * nb:(s + 1) * nb] for pg in pages], axis=0)
            a = s * 2 * LANES + kv * LANES
            return x_refs[0][:, a:a + LANES]

        acc = jnp.zeros((n_rows, 4 * CMP_HIDDEN), F32)
        for s2 in range(CMP_STRIDE // 2):
            xk = jnp.concatenate([sub_rows(2 * s2), sub_rows(2 * s2 + 1)], axis=1)
            acc = acc + _dot(xk.astype(BF16), wc_ref[kv, s2])
        cb = cb_ref[kv, 0:1, :]
        out = jnp.zeros((n_rows, LANES), F32)
        for g in range(NSA_KV_GROUPS):
            a0 = acc[:, (2 * g) * CMP_HIDDEN:(2 * g + 1) * CMP_HIDDEN]
            a1 = acc[:, (2 * g + 1) * CMP_HIDDEN:(2 * g + 2) * CMP_HIDDEN]
            prev = carry_ref[kv, SUBLANES - 1:SUBLANES, (2 * g) * CMP_HIDDEN:(2 * g + 1) * CMP_HIDDEN]
            shifted = jnp.where(row == 0, prev, pltpu.roll(a0, 1, axis=0))
            h = shifted + a1 + cb
            out = out + _dot(_silu(h).astype(BF16), w2_ref[kv, g])
        carry_ref[kv] = acc[n_rows - SUBLANES:, :]
        if kv == 0:
            ss = _dot_split(out * out, seg_ref[...]) * (1.0 / HEAD_DIM)
            ck_ref[...] = ((out * lax.rsqrt(ss + RMS_EPS)) * gk_ref[...]).astype(BF16)
        else:
            one = jnp.ones_like(out)
            cv_ref[:, 0:LANES] = jnp.where(low, out, one).astype(BF16)
            cv_ref[:, LANES:2 * LANES] = jnp.where(low, one, out).astype(BF16)


def _compress_out_shape(n_out_rows):
    return [jax.ShapeDtypeStruct((n_out_rows, LANES), BF16),
            jax.ShapeDtypeStruct((n_out_rows, 2 * LANES), BF16)]


def _compress_weight_specs():
    return [pl.BlockSpec((2, CMP_STRIDE // 2, 2 * LANES, 4 * CMP_HIDDEN), lambda *a: (0, 0, 0, 0)),
            pl.BlockSpec((2, SUBLANES, CMP_HIDDEN), lambda *a: (0, 0, 0)),
            pl.BlockSpec((2, NSA_KV_GROUPS, CMP_HIDDEN, LANES), lambda *a: (0, 0, 0, 0)),
            pl.BlockSpec((1, LANES), lambda *a: (0, 0)),
            pl.BlockSpec((LANES, LANES), lambda *a: (0, 0))]


def _compress_prompt(xsub, wc, cb, w2p, gk, seg, n_batch, rows_per_step):
    total = xsub.shape[0]
    per_b = total // n_batch
    nj = per_b // rows_per_step
    kern = functools.partial(_compress_kernel, n_in=1, has_table=False)
    return pl.pallas_call(
        kern,
        grid=(n_batch, nj),
        in_specs=[pl.BlockSpec((rows_per_step, xsub.shape[1]), lambda b, j: (b * nj + j, 0))]
        + _compress_weight_specs(),
        out_specs=[pl.BlockSpec((rows_per_step, LANES), lambda b, j: (b * nj + j, 0)),
                   pl.BlockSpec((rows_per_step, 2 * LANES), lambda b, j: (b * nj + j, 0))],
        out_shape=_compress_out_shape(total),
        scratch_shapes=[pltpu.VMEM((2, SUBLANES, 4 * CMP_HIDDEN), F32)],
        compiler_params=pltpu.CompilerParams(dimension_semantics=("arbitrary", "arbitrary"),
                                             vmem_limit_bytes=VMEM_LIMIT),
        name="compress_prompt",
    )(xsub, wc, cb, w2p, gk, seg)


def _compress_sample(table, pool_t, wc, cb, w2p, gk, seg, n_batch, n_pages, pages_per_step):
    page = pool_t.shape[2]
    sub_per_page = page // CMP_STRIDE
    nj = n_pages // pages_per_step
    rows_per_step = pages_per_step * sub_per_page
    total = n_batch * n_pages * sub_per_page
    kern = functools.partial(_compress_kernel, n_in=pages_per_step, has_table=True)
    rho = np.arange(page)
    perm = jnp.asarray(np.arange(page)[None, :] == (CMP_STRIDE * (rho % sub_per_page) + rho // sub_per_page)[:, None],
                       BF16)

    def page_spec(p):
        return pl.BlockSpec((1, pool_t.shape[1], page),
                            lambda b, j, t: (t[b * n_pages + j * pages_per_step + p], 0, 0))

    grid_spec = pltpu.PrefetchScalarGridSpec(
        num_scalar_prefetch=1,
        grid=(n_batch, nj),
        in_specs=[page_spec(p) for p in range(pages_per_step)]
        + [pl.BlockSpec((page, page), lambda b, j, t: (0, 0))] + _compress_weight_specs(),
        out_specs=[pl.BlockSpec((rows_per_step, LANES), lambda b, j, t: (b * nj + j, 0)),
                   pl.BlockSpec((rows_per_step, 2 * LANES), lambda b, j, t: (b * nj + j, 0))],
        scratch_shapes=[pltpu.VMEM((2, SUBLANES, 4 * CMP_HIDDEN), F32)],
    )
    return pl.pallas_call(
        kern,
        grid_spec=grid_spec,
        out_shape=_compress_out_shape(total),
        compiler_params=pltpu.CompilerParams(dimension_semantics=("arbitrary", "arbitrary"),
                                             vmem_limit_bytes=VMEM_LIMIT),
        name="compress_sample",
    )(table, *([pool_t] * pages_per_step), perm, wc, cb, w2p, gk, seg)


def _pick_blocks(v_s, sel_s, lane_f, n_pick):
    sel_s[...] = jnp.zeros(sel_s.shape, F32)

    def body(_, c):
        v = v_s[...]
        mx = jnp.max(v, axis=1, keepdims=True)
        cand = jnp.where(v == mx, lane_f, float(2 * N_SEL_LANES))
        idx = jnp.min(cand, axis=1, keepdims=True)
        hit = lane_f == idx
        sel_s[...] = jnp.where(hit, 1.0, sel_s[...])
        v_s[...] = jnp.where(hit, PICKED, v)
        return c

    lax.fori_loop(0, n_pick, body, 0)


def _cmp_probs(qh, ck, slope, crel, cmask):
    s = _dot_nt(qh, ck) - slope * crel
    s = jnp.where(cmask, s, -BIG)
    m = jnp.max(s, axis=1, keepdims=True)
    e = jnp.where(cmask, jnp.exp(s - m), 0.0)
    return e / jnp.maximum(jnp.sum(e, axis=1, keepdims=True), 1e-30)


def _normalise(acc):
    return acc / jnp.maximum(pltpu.roll(acc, HEAD_DIM, axis=1), 1e-30)


def _sb_weights(z, causal, carry, u):
    sp = _softplus(z)
    l1m = -sp if causal is None else jnp.where(causal, -sp, 0.0)
    a = jnp.exp((z - sp) + _dot_split(l1m, u) + carry)
    if causal is not None:
        a = jnp.where(causal, a, 0.0)
    return a, jnp.sum(l1m, axis=1, keepdims=True)


def _tile_sweep(n_tiles, kb_of, tile, live=None):
    tile(kb_of(0), True)
    n_rest = n_tiles - 1
    odd = n_rest & 1

    @pl.when(odd == 1)
    def _():
        tile(kb_of(1), False)

    start = 1 + odd
    n_trips = n_rest >> 1

    def run(t):
        tile(kb_of(start + 2 * t), False)
        tile(kb_of(start + 2 * t + 1), False)

    if live is None:
        def body(t, c):
            run(t)
            return c

        lax.fori_loop(0, n_trips, body, 0)
    else:
        def cond(state):
            t, go = state
            return jnp.logical_and(t < n_trips, go)

        def wbody(state):
            run(state[0])
            return state[0] + 1, live()

        lax.while_loop(cond, wbody, (jnp.int32(0), live()))


def _cmp_prompt_kernel(q_ref, gate_ref, ck_ref, cv_ref, selmap_ref, oc_ref, sel_ref, v_s, sel_s):
    nq = q_ref.shape[1]
    q0 = pl.program_id(1) * nq
    n_c = ck_ref.shape[1]
    lane = lax.broadcasted_iota(jnp.int32, (nq, LANES), 1)
    low = lane < HEAD_DIM
    crow = lax.broadcasted_iota(jnp.int32, (nq, n_c), 1)
    cend = crow * CMP_STRIDE + (CMP_STRIDE - 1)
    cq = q0 + lax.broadcasted_iota(jnp.int32, (nq, n_c), 0)
    cmask = jnp.logical_and(cend <= cq, crow >= 1)
    crel = (cq - cend).astype(F32)
    ck = ck_ref[0]
    cur = (q0 + lax.broadcasted_iota(jnp.int32, (nq, LANES), 0)) >> SEL_SHIFT
    forced = jnp.logical_or(lane == 0, jnp.logical_or(lane == cur, lane == cur - 1))
    allowed = lane <= cur
    outs = [[None] * HEADS_PER_GROUP for _ in range(NSA_KV_GROUPS)]
    for g in range(NSA_KV_GROUPS):
        keep = low if g == 0 else jnp.logical_not(low)
        imp = jnp.zeros((nq, N_SEL_LANES), F32)
        for c in range(HEADS_PER_GROUP):
            h = g * HEADS_PER_GROUP + c
            qc = q_ref[0, :, c * LANES:(c + 1) * LANES]
            p = _cmp_probs(jnp.where(keep, qc, jnp.zeros_like(qc)), ck, _slope(h), crel, cmask)
            oc = _dot(p.astype(BF16), cv_ref[0, :, g * LANES:(g + 1) * LANES])
            outs[g][c] = oc * gate_ref[0, :, h:h + 1]
            imp = imp + _dot_split(p, selmap_ref[...])
        v_s[g * nq:(g + 1) * nq, :] = jnp.where(forced, BIG, jnp.where(allowed, imp, -BIG))
    for c in range(HEADS_PER_GROUP):
        oc_ref[0, :, c * LANES:(c + 1) * LANES] = jnp.where(low, outs[0][c], outs[1][c])
    lane_f = lax.broadcasted_iota(jnp.int32, v_s.shape, 1).astype(F32)
    _pick_blocks(v_s, sel_s, lane_f, SEL_TOPK)
    for g in range(NSA_KV_GROUPS):
        sel_ref[0, g] = jnp.where(allowed, sel_s[g * nq:(g + 1) * nq, :], 0.0).astype(BF16)


def _cmp_prompt(q, gates, ck, cv, selmap, nq):
    n_b, t, _ = q.shape
    n_c = ck.shape[1]
    per_q = lambda b, i: (b, i, 0)
    per_b = lambda b, i: (b, 0, 0)
    return pl.pallas_call(
        _cmp_prompt_kernel,
        grid=(n_b, t // nq),
        in_specs=[pl.BlockSpec((1, nq, 4 * LANES), per_q),
                  pl.BlockSpec((1, nq, LANES), per_q),
                  pl.BlockSpec((1, n_c, LANES), per_b),
                  pl.BlockSpec((1, n_c, 2 * LANES), per_b),
                  pl.BlockSpec((n_c, N_SEL_LANES), lambda b, i: (0, 0))],
        out_specs=[pl.BlockSpec((1, nq, 4 * LANES), per_q),
                   pl.BlockSpec((1, NSA_KV_GROUPS, nq, N_SEL_LANES), lambda b, i: (b, 0, i, 0))],
        out_shape=[jax.ShapeDtypeStruct((n_b, t, 4 * LANES), F32),
                   jax.ShapeDtypeStruct((n_b, NSA_KV_GROUPS, t, N_SEL_LANES), BF16)],
        scratch_shapes=[pltpu.VMEM((NSA_KV_GROUPS * nq, N_SEL_LANES), F32),
                        pltpu.VMEM((NSA_KV_GROUPS * nq, N_SEL_LANES), F32)],
        compiler_params=pltpu.CompilerParams(dimension_semantics=("arbitrary", "arbitrary"),
                                             vmem_limit_bytes=VMEM_LIMIT),
        name="cmp_prompt",
    )(q, gates, ck, cv, selmap)


def _slcwin_prompt_kernel(tiles_ref, count_ref, q_ref, gate_ref, oc_ref, sel_ref, sk_ref, sv_ref, wk_ref, wv_ref,
                          o_ref, qm_s, m_s, acc_s, tot_s):
    i = pl.program_id(1)
    step = pl.program_id(0) * pl.num_programs(1) + i
    tiles_per_step = sk_ref.shape[1] // LANES
    nq = LANES
    lane = lax.broadcasted_iota(jnp.int32, (nq, LANES), 1)
    row = lax.broadcasted_iota(jnp.int32, (nq, LANES), 0)
    low = lane < HEAD_DIM
    qpos = i * nq + row

    for g in range(NSA_KV_GROUPS):
        keep = low if g == 0 else jnp.logical_not(low)
        for c in range(HEADS_PER_GROUP):
            qc = q_ref[0, :, c * LANES:(c + 1) * LANES]
            r0 = (g * HEADS_PER_GROUP + c) * nq
            qm_s[r0:r0 + nq, :] = jnp.where(keep, qc, jnp.zeros_like(qc))

    def run_branch(k_ref, v_ref, banded, n_tiles, kb_of, gate_base, first_branch):
        m_s[...] = jnp.full(m_s.shape, -BIG, F32)
        acc_s[...] = jnp.zeros(acc_s.shape, F32)

        def tile(kb, diag):
            k0 = pl.multiple_of(kb * LANES, LANES)
            kt = k_ref[0, pl.ds(k0, LANES), :]
            rel = qpos - (k0 + lane)
            relf = rel.astype(F32)
            s = _dot_nt(qm_s[...], kt)
            if banded:
                band = jnp.where(jnp.logical_and(rel >= 0, rel < WINDOW), 0.0, -BIG)
            else:
                blk = (LANES // SEL_BLOCK) * kb + (lane >> SEL_SHIFT)
                expand = jnp.where(row == blk, 1.0, 0.0).astype(BF16)
            for g in range(NSA_KV_GROUPS):
                if banded:
                    bias = band
                else:
                    bias = (_dot(sel_ref[0, g], expand) - 1.0) * BIG
                    if diag:
                        bias = jnp.where(rel >= 0, bias, -BIG)
                ps, alphas = [], []
                for c in range(HEADS_PER_GROUP):
                    h = g * HEADS_PER_GROUP + c
                    r0 = h * nq
                    sh = (s[r0:r0 + nq] - _slope(h) * relf) + bias
                    m_old = m_s[r0:r0 + nq, :]
                    m_new = jnp.maximum(m_old, jnp.max(sh, axis=1, keepdims=True))
                    ps.append(jnp.exp(sh - m_new).astype(BF16))
                    alphas.append(jnp.exp(m_old - m_new))
                    m_s[r0:r0 + nq, :] = m_new
                g0 = g * HEADS_PER_GROUP * nq
                g1 = (g + 1) * HEADS_PER_GROUP * nq
                vt = v_ref[0, pl.ds(k0, LANES), g * LANES:(g + 1) * LANES]
                pv = _dot(jnp.concatenate(ps, axis=0), vt)
                acc_s[g0:g1, :] = jnp.concatenate(alphas, axis=0) * acc_s[g0:g1, :] + pv

        _tile_sweep(n_tiles, kb_of, tile)
        for h in range(NSA_HEADS):
            r0 = h * nq
            o = _normalise(acc_s[r0:r0 + nq, :]) * gate_ref[0, :, gate_base + h:gate_base + h + 1]
            tot_s[r0:r0 + nq, :] = o if first_branch else tot_s[r0:r0 + nq, :] + o

    run_branch(sk_ref, sv_ref, False, count_ref[step], lambda t: tiles_ref[step * tiles_per_step + t],
               NSA_HEADS, True)
    run_branch(wk_ref, wv_ref, True, jnp.minimum(i, WINDOW // LANES) + 1, lambda t: i - t,
               2 * NSA_HEADS, False)

    for c in range(HEADS_PER_GROUP):
        a = tot_s[c * nq:(c + 1) * nq, :]
        b = tot_s[(HEADS_PER_GROUP + c) * nq:(HEADS_PER_GROUP + c + 1) * nq, :]
        o_ref[0, :, c * LANES:(c + 1) * LANES] = oc_ref[0, :, c * LANES:(c + 1) * LANES] + jnp.where(low, a, b)


def _tile_schedule(sel):
    n_b, _, t, n_blk = sel.shape
    nqb = t // LANES
    per_tile = LANES // SEL_BLOCK
    used = jnp.max(sel.reshape(n_b, NSA_KV_GROUPS, nqb, LANES, n_blk // per_tile, per_tile), axis=(1, 3, 5)) > 0
    kb = jnp.arange(n_blk // per_tile, dtype=jnp.int32)
    used = jnp.logical_and(used[:, :, :nqb], kb[None, None, :nqb] <= jnp.arange(nqb, dtype=jnp.int32)[None, :, None])
    order = -jnp.sort(-jnp.where(used, kb[None, None, :nqb], -1), axis=-1)
    return jnp.maximum(order, 0).reshape(-1).astype(jnp.int32), jnp.sum(used, axis=-1).reshape(-1).astype(jnp.int32)


def _slcwin_prompt(q, gates, oc, sel, sk, sv, wk, wv):
    n_b, t, _ = q.shape
    nqb = t // LANES
    n_rows = NSA_HEADS * LANES
    tiles, counts = _tile_schedule(sel)
    per_q = lambda b, i, *_: (b, i, 0)
    per_b = lambda b, i, *_: (b, 0, 0)
    grid_spec = pltpu.PrefetchScalarGridSpec(
        num_scalar_prefetch=2,
        grid=(n_b, nqb),
        in_specs=[pl.BlockSpec((1, LANES, 4 * LANES), per_q),
                  pl.BlockSpec((1, LANES, LANES), per_q),
                  pl.BlockSpec((1, LANES, 4 * LANES), per_q),
                  pl.BlockSpec((1, NSA_KV_GROUPS, LANES, N_SEL_LANES), lambda b, i, *_: (b, 0, i, 0)),
                  pl.BlockSpec((1, t, LANES), per_b),
                  pl.BlockSpec((1, t, 2 * LANES), per_b),
                  pl.BlockSpec((1, t, LANES), per_b),
                  pl.BlockSpec((1, t, 2 * LANES), per_b)],
        out_specs=pl.BlockSpec((1, LANES, 4 * LANES), per_q),
        scratch_shapes=[pltpu.VMEM((n_rows, LANES), BF16),
                        pltpu.VMEM((n_rows, LANES), F32),
                        pltpu.VMEM((n_rows, LANES), F32),
                        pltpu.VMEM((n_rows, LANES), F32)],
    )
    return pl.pallas_call(
        _slcwin_prompt_kernel,
        grid_spec=grid_spec,
        out_shape=jax.ShapeDtypeStruct((n_b, t, 4 * LANES), F32),
        compiler_params=pltpu.CompilerParams(dimension_semantics=("arbitrary", "arbitrary"),
                                             vmem_limit_bytes=VMEM_LIMIT),
        name="slcwin_prompt",
    )(tiles, counts, q, gates, oc, sel, sk, sv, wk, wv)


def _sb_prompt_kernel(q_ref, k_ref, v_ref, u_ref, o_ref, qm_s, carry_s, acc_s):
    i = pl.program_id(1)
    nq = LANES
    n_pairs = SB_KV_HEADS // 2
    rows = 4 * nq
    lane = lax.broadcasted_iota(jnp.int32, (rows, LANES), 1)
    qrow = lax.broadcasted_iota(jnp.int32, (rows, LANES), 0) & (nq - 1)
    low = lax.broadcasted_iota(jnp.int32, (nq, LANES), 1) < HEAD_DIM
    qpos = i * nq + qrow
    for kp in range(n_pairs):
        for r in range(2):
            ch = 2 * kp + r
            qc = q_ref[0, :, ch * LANES:(ch + 1) * LANES]
            zero = jnp.zeros_like(qc)
            qm_s[kp, (2 * r) * nq:(2 * r + 1) * nq, :] = jnp.where(low, qc, zero)
            qm_s[kp, (2 * r + 1) * nq:(2 * r + 2) * nq, :] = jnp.where(low, zero, qc)
    carry_s[...] = jnp.zeros(carry_s.shape, F32)
    acc_s[...] = jnp.zeros(acc_s.shape, F32)
    u = u_ref[...]

    def tile(kb, diag):
        k0 = pl.multiple_of(kb * LANES, LANES)
        causal = (k0 + lane) < qpos if diag else None
        for kp in range(n_pairs):
            kt = k_ref[0, pl.ds(k0, LANES), kp * LANES:(kp + 1) * LANES]
            vt = v_ref[0, pl.ds(k0, LANES), kp * LANES:(kp + 1) * LANES]
            z = _dot_nt(qm_s[kp], kt)
            a, tile_sum = _sb_weights(z, causal, carry_s[kp], u)
            acc_s[kp] = acc_s[kp] + _dot(a.astype(BF16), vt)
            carry_s[kp] = carry_s[kp] + tile_sum

    def live():
        c = carry_s[0]
        for kp in range(1, n_pairs):
            c = jnp.maximum(c, carry_s[kp])
        return jnp.max(c) > EXP_UNDERFLOW

    _tile_sweep(i + 1, lambda t: i - t, tile, live)
    for kp in range(n_pairs):
        for r in range(2):
            ch = 2 * kp + r
            o_ref[0, :, ch * LANES:(ch + 1) * LANES] = jnp.where(
                low, acc_s[kp, (2 * r) * nq:(2 * r + 1) * nq, :], acc_s[kp, (2 * r + 1) * nq:(2 * r + 2) * nq, :])


def _sb_prompt(qs, sbk, sbv, u):
    n_b, t, _ = qs.shape
    nqb = t // LANES
    n_pairs = SB_KV_HEADS // 2
    per_q = lambda b, i: (b, i, 0)
    per_b = lambda b, i: (b, 0, 0)
    return pl.pallas_call(
        _sb_prompt_kernel,
        grid=(n_b, nqb),
        in_specs=[pl.BlockSpec((1, LANES, 4 * LANES), per_q),
                  pl.BlockSpec((1, t, n_pairs * LANES), per_b),
                  pl.BlockSpec((1, t, n_pairs * LANES), per_b),
                  pl.BlockSpec((LANES, LANES), lambda b, i: (0, 0))],
        out_specs=pl.BlockSpec((1, LANES, 4 * LANES), per_q),
        out_shape=jax.ShapeDtypeStruct((n_b, t, 4 * LANES), F32),
        scratch_shapes=[pltpu.VMEM((n_pairs, 4 * LANES, LANES), BF16),
                        pltpu.VMEM((n_pairs, 4 * LANES, LANES), F32),
                        pltpu.VMEM((n_pairs, 4 * LANES, LANES), F32)],
        compiler_params=pltpu.CompilerParams(dimension_semantics=("arbitrary", "arbitrary"),
                                             vmem_limit_bytes=VMEM_LIMIT),
        name="sb_prompt",
    )(qs, sbk, sbv, u)


def _final_kernel(x_ref, oa_ref, sza_ref, ob_ref, szb_ref, w_ref, y_ref):
    half = oa_ref.shape[1]
    ma = (oa_ref[...] * sza_ref[...]).astype(BF16)
    mb = (ob_ref[...] * szb_ref[...]).astype(BF16)
    y_ref[...] = x_ref[...] + _dot(ma, w_ref[0:half, :]) + _dot(mb, w_ref[half:2 * half, :])


def _final(x2, oa, sza, ob, szb, w_out_perm, tm):
    rows, d = x2.shape
    half = oa.shape[1]
    row = lambda i: (i, 0)
    return pl.pallas_call(
        _final_kernel,
        grid=(rows // tm,),
        in_specs=[pl.BlockSpec((tm, d), row), pl.BlockSpec((tm, half), row),
                  pl.BlockSpec((tm, half), row), pl.BlockSpec((tm, half), row),
                  pl.BlockSpec((tm, half), row), pl.BlockSpec((d, d), lambda i: (0, 0))],
        out_specs=pl.BlockSpec((tm, d), row),
        out_shape=jax.ShapeDtypeStruct((rows, d), F32),
        compiler_params=pltpu.CompilerParams(dimension_semantics=("arbitrary",),
                                             vmem_limit_bytes=VMEM_LIMIT),
        name="final",
    )(x2, oa, sza, ob, szb, w_out_perm)


Q_PAD = SUBLANES
Q_SHIFT = 3
assert 1 << Q_SHIFT == Q_PAD
SAMPLE_ROWS = NSA_HEADS * Q_PAD


def _row_ids(shape):
    r = lax.broadcasted_iota(jnp.int32, shape, 0)
    return r >> Q_SHIFT, r & (Q_PAD - 1)


def _head_slopes(shape):
    hidx, _ = _row_ids(shape)
    out = jnp.zeros(shape, F32)
    for h in range(NSA_HEADS):
        out = jnp.where(hidx == h, _slope(h), out)
    return out


def _stack_nsa_queries(q):
    lane = lax.broadcasted_iota(jnp.int32, (Q_PAD, LANES), 1)
    low = lane < HEAD_DIM
    parts = []
    for g in range(NSA_KV_GROUPS):
        keep = low if g == 0 else jnp.logical_not(low)
        for c in range(HEADS_PER_GROUP):
            parts.append(jnp.where(keep, q[:, c * LANES:(c + 1) * LANES], 0.0))
    return jnp.concatenate(parts, axis=0).astype(BF16)


def _cmp_sample_kernel(q_ref, ck_ref, cv_ref, selmap_ref, eall_ref, ocmp_ref, mexp_ref, selblk_ref, v_s, sel_s,
                       *, past_len, n_pick):
    n_seq = q_ref.shape[0]
    n_rows = SAMPLE_ROWS
    half = n_rows // 2
    n_c = ck_ref.shape[1]
    crow = lax.broadcasted_iota(jnp.int32, (n_rows, n_c), 1)
    cend = crow * CMP_STRIDE + (CMP_STRIDE - 1)
    cq = past_len + _row_ids((n_rows, n_c))[1]
    cmask = jnp.logical_and(cend <= cq, crow >= 1)
    slope = _head_slopes((n_rows, n_c))
    crel = (cq - cend).astype(F32)
    n_past_blocks = past_len // SEL_BLOCK
    lane = lax.broadcasted_iota(jnp.int32, (NSA_KV_GROUPS * Q_PAD, N_SEL_LANES), 1)
    forced = jnp.logical_or(lane == 0, lane == n_past_blocks - 1)
    allowed = lane < n_past_blocks
    rows_seq = NSA_KV_GROUPS * Q_PAD
    for b in range(n_seq):
        p = _cmp_probs(_stack_nsa_queries(q_ref[b]), ck_ref[b], slope, crel, cmask)
        pb = p.astype(BF16)
        ocmp_ref[b, 0:half, :] = _dot(pb[0:half], cv_ref[b, :, 0:LANES])
        ocmp_ref[b, half:n_rows, :] = _dot(pb[half:n_rows], cv_ref[b, :, LANES:2 * LANES])
        pg = []
        for g in range(NSA_KV_GROUPS):
            acc = jnp.zeros((Q_PAD, n_c), F32)
            for c in range(HEADS_PER_GROUP):
                r0 = (g * HEADS_PER_GROUP + c) * Q_PAD
                acc = acc + p[r0:r0 + Q_PAD]
            pg.append(acc)
        imp = _dot_split(jnp.concatenate(pg, axis=0), selmap_ref[...])
        v_s[b * rows_seq:(b + 1) * rows_seq, :] = jnp.where(allowed, jnp.where(forced, BIG, imp), -BIG)
    lane_f = lax.broadcasted_iota(jnp.int32, v_s.shape, 1).astype(F32)
    _pick_blocks(v_s, sel_s, lane_f, n_pick)
    allowed_all = lax.broadcasted_iota(jnp.int32, v_s.shape, 1) < n_past_blocks
    sel32 = jnp.where(allowed_all, sel_s[...], 0.0)
    mexp = _dot(sel32.astype(BF16), eall_ref[...])
    for b in range(n_seq):
        mexp_ref[b] = mexp[b * rows_seq:(b + 1) * rows_seq]
        selblk_ref[b] = sel32[b * rows_seq:(b + 1) * rows_seq]


def _cmp_sample(q8, ck, cv, selmap, eall, past_len, n_pick, n_seq):
    n_b = q8.shape[0]
    n_c = ck.shape[1]
    rows_seq = NSA_KV_GROUPS * Q_PAD
    kern = functools.partial(_cmp_sample_kernel, past_len=past_len, n_pick=n_pick)
    blk = lambda b: (b, 0, 0)
    return pl.pallas_call(
        kern,
        grid=(n_b // n_seq,),
        in_specs=[pl.BlockSpec((n_seq, Q_PAD, 4 * LANES), blk),
                  pl.BlockSpec((n_seq, n_c, LANES), blk),
                  pl.BlockSpec((n_seq, n_c, 2 * LANES), blk),
                  pl.BlockSpec((n_c, N_SEL_LANES), lambda b: (0, 0)),
                  pl.BlockSpec((N_SEL_LANES, past_len), lambda b: (0, 0))],
        out_specs=[pl.BlockSpec((n_seq, SAMPLE_ROWS, LANES), blk),
                   pl.BlockSpec((n_seq, rows_seq, past_len), blk),
                   pl.BlockSpec((n_seq, rows_seq, N_SEL_LANES), blk)],
        out_shape=[jax.ShapeDtypeStruct((n_b, SAMPLE_ROWS, LANES), F32),
                   jax.ShapeDtypeStruct((n_b, rows_seq, past_len), F32),
                   jax.ShapeDtypeStruct((n_b, rows_seq, N_SEL_LANES), F32)],
        scratch_shapes=[pltpu.VMEM((n_seq * rows_seq, N_SEL_LANES), F32),
                        pltpu.VMEM((n_seq * rows_seq, N_SEL_LANES), F32)],
        compiler_params=pltpu.CompilerParams(dimension_semantics=("arbitrary",),
                                             vmem_limit_bytes=VMEM_LIMIT),
        name="cmp_sample",
    )(q8, ck, cv, selmap, eall)


def _pad_keys(x):
    return jnp.concatenate([x, jnp.zeros((LANES - Q_PAD, x.shape[1]), F32)], axis=0)


def _value_variants(v, axis):
    first = lax.broadcasted_iota(jnp.int32, v.shape, axis) < HEAD_DIM
    one = jnp.ones_like(v)
    return jnp.where(first, v, one).astype(BF16), jnp.where(first, one, v).astype(BF16)


def _stack_sb_queries(qs):
    lane8 = lax.broadcasted_iota(jnp.int32, (Q_PAD, LANES), 1)
    low8 = lane8 < HEAD_DIM
    parts = []
    for k in range(SB_KV_HEADS):
        for r in range(SB_HEADS // SB_KV_HEADS):
            ch = 2 * (k // 2) + r
            keep = low8 if k % 2 == 0 else jnp.logical_not(low8)
            piece = jnp.where(keep, qs[:, ch * LANES:(ch + 1) * LANES], 0.0)
            zero = jnp.zeros_like(piece)
            parts.append(jnp.concatenate([piece, zero] if k // 2 == 0 else [zero, piece], axis=1))
    return jnp.concatenate(parts, axis=0).astype(BF16)


def _sb_pages_sweep(qb, pages, carry, accb, u):
    zs, sps, sums = [], [], []
    for pg in pages:
        z = _dot(qb, pg[0, 0:2 * LANES, :].astype(BF16))
        sp = _softplus(z)
        zs.append(z)
        sps.append(sp)
        sums.append(jnp.sum(-sp, axis=1, keepdims=True))
    for pi in range(len(pages) - 1, -1, -1):
        a = jnp.exp((zs[pi] - sps[pi]) + _dot_split(-sps[pi], u) + carry)
        accb = accb + _dot_nt(a.astype(BF16), pages[pi][0, 2 * LANES:4 * LANES, :].astype(BF16))
        carry = carry + sums[pi]
    return carry, accb


def _attn_sample_kernel(*refs, past_len, dec_seq, n_slots, n_near, n_list):
    table_ref, plist_ref, pcount_ref = refs[:3]
    refs = refs[3:]
    (qa_ref, qs_ref, gate_ref, ocmp_ref, slcn_ref, winn_ref, sbn_ref, state_ref, u_ref) = refs[:9]
    mask_refs = refs[9:9 + n_slots]
    slc_pages = refs[9 + n_slots:9 + 2 * n_slots]
    sb_pages = refs[9 + 2 * n_slots:9 + 2 * n_slots + n_near]
    oa_ref, accb_ref, carry_ref = refs[9 + 2 * n_slots + n_near:12 + 2 * n_slots + n_near]
    qa_s, m_s, acc_s, win_s = refs[12 + 2 * n_slots + n_near:]

    b = pl.program_id(0)
    j = pl.program_id(1)
    n_steps = pl.num_programs(1)
    n_rows = SAMPLE_ROWS
    half = n_rows // 2
    lane = lax.broadcasted_iota(jnp.int32, (n_rows, LANES), 1)
    low = lane < HEAD_DIM
    t_q = _row_ids((n_rows, LANES))[1]
    qpos = past_len + t_q
    slope = _head_slopes((n_rows, LANES))

    @pl.when(j == 0)
    def _():
        qa = _stack_nsa_queries(qa_ref[0])
        qa_s[...] = qa
        qb = _stack_sb_queries(qs_ref[0])
        u = u_ref[...]

        new_valid = lane < dec_seq
        slcn = _pad_keys(slcn_ref[0])
        smask = jnp.logical_and(new_valid, lane <= t_q)
        s = _dot_nt(qa, slcn[:, 0:LANES].astype(BF16)) - slope * (t_q - lane).astype(F32)
        s = jnp.where(smask, s, -BIG)
        m0 = jnp.max(s, axis=1, keepdims=True)
        p = jnp.where(smask, jnp.exp(s - m0), 0.0).astype(BF16)
        v0, v1 = _value_variants(slcn[:, LANES:2 * LANES], 1)
        m_s[...] = jnp.broadcast_to(m0, (n_rows, LANES))
        acc_s[0:half, :] = _dot(p[0:half], v0)
        acc_s[half:n_rows, :] = _dot(p[half:n_rows], v1)

        sbn = _pad_keys(sbn_ref[0])
        z = _dot_nt(qb, sbn[:, 0:2 * LANES].astype(BF16))
        a, tile_sum = _sb_weights(z, jnp.logical_and(new_valid, lane < t_q), 0.0, u)
        accb = _dot(a.astype(BF16), sbn[:, 2 * LANES:4 * LANES].astype(BF16))
        carry, accb = _sb_pages_sweep(qb, sb_pages, jnp.broadcast_to(tile_sum, (n_rows, LANES)), accb, u)
        accb_ref[0] = accb
        carry_ref[0] = carry

        n_state = state_ref.shape[2]
        winn = _pad_keys(winn_ref[0])
        n_k = n_state + LANES
        col = lax.broadcasted_iota(jnp.int32, (n_rows, n_k), 1)
        kpos = jnp.where(col < n_state, past_len - n_state + col, past_len + col - n_state)
        rel = past_len + _row_ids((n_rows, n_k))[1] - kpos
        wmask = jnp.logical_and(jnp.logical_and(rel >= 0, rel < WINDOW), col < n_state + dec_seq)
        qk = jnp.concatenate([_dot(qa, state_ref[0, 0:LANES, :].astype(BF16)),
                              _dot_nt(qa, winn[:, 0:LANES].astype(BF16))], axis=1)
        s = qk - _head_slopes((n_rows, n_k)) * rel.astype(F32)
        s = jnp.where(wmask, s, -BIG)
        m = jnp.max(s, axis=1, keepdims=True)
        e = jnp.where(wmask, jnp.exp(s - m), 0.0)
        p = (e / jnp.maximum(jnp.sum(e, axis=1, keepdims=True), 1e-30)).astype(BF16)
        win_s[...] = (_dot_nt(p[:, 0:n_state], state_ref[0, LANES:2 * LANES, :].astype(BF16))
                      + _dot(p[:, n_state:n_k], winn[:, LANES:2 * LANES].astype(BF16)))

    n_used = pcount_ref[b]

    @pl.when(j * n_slots < n_used)
    def _():
        qa = qa_s[...]
        ss = []
        for pi in range(n_slots):
            slot = j * n_slots + pi
            msk = mask_refs[pi][0]
            bias16 = jnp.where(slot < n_used, (msk - 1.0) * BIG, -BIG)
            bias = jnp.concatenate(
                [bias16[0:Q_PAD]] * HEADS_PER_GROUP + [bias16[Q_PAD:2 * Q_PAD]] * HEADS_PER_GROUP, axis=0)
            kpos = plist_ref[b * n_list + slot] * LANES + lane
            qk = _dot(qa, slc_pages[pi][0, 0:LANES, :].astype(BF16))
            ss.append((qk - slope * (qpos - kpos).astype(F32)) + bias)
        m_old = m_s[...]
        m_new = m_old
        for s in ss:
            m_new = jnp.maximum(m_new, jnp.max(s, axis=1, keepdims=True))
        pv0 = jnp.zeros((half, LANES), F32)
        pv1 = jnp.zeros((half, LANES), F32)
        for pi in range(n_slots):
            p = jnp.exp(ss[pi] - m_new).astype(BF16)
            v0, v1 = _value_variants(slc_pages[pi][0, LANES:2 * LANES, :], 0)
            pv0 = pv0 + _dot_nt(p[0:half], v0)
            pv1 = pv1 + _dot_nt(p[half:n_rows], v1)
        alpha = jnp.exp(m_old - m_new)
        acc_s[0:half, :] = alpha[0:half] * acc_s[0:half, :] + pv0
        acc_s[half:n_rows, :] = alpha[half:n_rows] * acc_s[half:n_rows, :] + pv1
        m_s[...] = m_new

    @pl.when(j == n_steps - 1)
    def _():
        o_slc = _normalise(acc_s[...])
        o_win = win_s[...]
        o_cmp = ocmp_ref[0]
        for c in range(HEADS_PER_GROUP):
            tot = []
            for g in range(NSA_KV_GROUPS):
                h = g * HEADS_PER_GROUP + c
                r0 = h * Q_PAD
                gc = gate_ref[0, :, h:h + 1]
                gs = gate_ref[0, :, NSA_HEADS + h:NSA_HEADS + h + 1]
                gw = gate_ref[0, :, 2 * NSA_HEADS + h:2 * NSA_HEADS + h + 1]
                tot.append(gc * o_cmp[r0:r0 + Q_PAD] + gs * o_slc[r0:r0 + Q_PAD] + gw * o_win[r0:r0 + Q_PAD])
            oa_ref[0, :, c * LANES:(c + 1) * LANES] = jnp.where(low[0:Q_PAD], tot[0], tot[1])


def _sb_rest_kernel(*refs, n_chunks, chunk):
    need_ref = refs[1]
    refs = refs[2:]
    qs_ref, accb_ref, carry_ref, u_ref = refs[:4]
    pages = refs[4:4 + n_chunks * chunk]
    ob_ref = refs[4 + n_chunks * chunk]
    accb_s, carry_s = refs[5 + n_chunks * chunk:]
    accb_s[...] = accb_ref[0]
    carry_s[...] = carry_ref[0]

    @pl.when(need_ref[pl.program_id(0)] > 0)
    def _():
        qb = _stack_sb_queries(qs_ref[0])
        u = u_ref[...]
        for c in range(n_chunks - 1, -1, -1):
            @pl.when(jnp.max(carry_s[...]) > EXP_UNDERFLOW)
            def _():
                carry, accb = _sb_pages_sweep(qb, pages[c * chunk:(c + 1) * chunk], carry_s[...], accb_s[...], u)
                carry_s[...] = carry
                accb_s[...] = accb

    low = lax.broadcasted_iota(jnp.int32, (Q_PAD, LANES), 1) < HEAD_DIM
    acc_b = accb_s[...]
    for kp in range(SB_KV_HEADS // 2):
        for r in range(SB_HEADS // SB_KV_HEADS):
            ra = ((2 * kp) * 2 + r) * Q_PAD
            rb = ((2 * kp + 1) * 2 + r) * Q_PAD
            a0 = acc_b[ra:ra + Q_PAD, kp * LANES:(kp + 1) * LANES]
            a1 = acc_b[rb:rb + Q_PAD, kp * LANES:(kp + 1) * LANES]
            ch = 2 * kp + r
            ob_ref[0, :, ch * LANES:(ch + 1) * LANES] = jnp.where(low, a0, a1)


def _page_schedule(selblk, n_pages, n_list, n_slots):
    n_b = selblk.shape[0]
    per_page = LANES // SEL_BLOCK
    used = jnp.max(selblk[:, :, :n_pages * per_page].reshape(n_b, -1, n_pages, per_page), axis=(1, 3)) > 0
    page_id = jnp.arange(n_pages, dtype=jnp.int32)
    order = -jnp.sort(-jnp.where(used, page_id[None, :], -1), axis=-1)[:, :n_list]
    order = jnp.pad(order, ((0, 0), (0, n_list - order.shape[1])), constant_values=-1)
    steps = [order[:, 0:n_slots]]
    steps[0] = jnp.where(steps[0] < 0, order[:, 0:1], steps[0])
    for j in range(1, n_list // n_slots):
        cur = order[:, j * n_slots:(j + 1) * n_slots]
        steps.append(jnp.where(cur < 0, steps[j - 1], cur))
    plist = jnp.maximum(jnp.concatenate(steps, axis=1), 0)
    return plist.reshape(-1).astype(jnp.int32), jnp.minimum(jnp.sum(used, axis=-1), n_list).astype(jnp.int32)


def _attn_sample(table, plist, pcount, qa8, qs8, gate8, ocmp, mexp, slcn, winn, sbn, state, u,
                 slc_pool, sb_pool, n_pages, n_list, n_slots, n_near, past_len, dec_seq):
    n_b = qa8.shape[0]
    n_rows = SAMPLE_ROWS
    n_state = state.shape[2]
    kern = functools.partial(_attn_sample_kernel, past_len=past_len, dec_seq=dec_seq,
                             n_slots=n_slots, n_near=n_near, n_list=n_list)
    per_b = lambda b, j, *_: (b, 0, 0)

    def mask_spec(p):
        return pl.BlockSpec((1, 2 * Q_PAD, LANES), lambda b, j, t, pls, pc: (b, 0, pls[b * n_list + j * n_slots + p]))

    def slc_spec(p):
        return pl.BlockSpec((1, 2 * LANES, LANES),
                            lambda b, j, t, pls, pc: (t[b * n_pages + pls[b * n_list + j * n_slots + p]], 0, 0))

    def near_spec(p):
        return pl.BlockSpec((1, 4 * LANES, LANES), lambda b, j, t, pls, pc: (t[b * n_pages + n_pages - n_near + p], 0, 0))

    grid_spec = pltpu.PrefetchScalarGridSpec(
        num_scalar_prefetch=3,
        grid=(n_b, n_list // n_slots),
        in_specs=[pl.BlockSpec((1, Q_PAD, 4 * LANES), per_b),
                  pl.BlockSpec((1, Q_PAD, 4 * LANES), per_b),
                  pl.BlockSpec((1, Q_PAD, LANES), per_b),
                  pl.BlockSpec((1, n_rows, LANES), per_b),
                  pl.BlockSpec((1, Q_PAD, 2 * LANES), per_b),
                  pl.BlockSpec((1, Q_PAD, 2 * LANES), per_b),
                  pl.BlockSpec((1, Q_PAD, 4 * LANES), per_b),
                  pl.BlockSpec((1, 2 * LANES, n_state), per_b),
                  pl.BlockSpec((LANES, LANES), lambda b, j, *_: (0, 0))]
        + [mask_spec(p) for p in range(n_slots)]
        + [slc_spec(p) for p in range(n_slots)]
        + [near_spec(p) for p in range(n_near)],
        out_specs=[pl.BlockSpec((1, Q_PAD, 4 * LANES), per_b),
                   pl.BlockSpec((1, n_rows, 2 * LANES), per_b),
                   pl.BlockSpec((1, n_rows, LANES), per_b)],
        scratch_shapes=[pltpu.VMEM((n_rows, LANES), BF16),
                        pltpu.VMEM((n_rows, LANES), F32),
                        pltpu.VMEM((n_rows, LANES), F32),
                        pltpu.VMEM((n_rows, LANES), F32)],
    )
    return pl.pallas_call(
        kern,
        grid_spec=grid_spec,
        out_shape=[jax.ShapeDtypeStruct((n_b, Q_PAD, 4 * LANES), F32),
                   jax.ShapeDtypeStruct((n_b, n_rows, 2 * LANES), F32),
                   jax.ShapeDtypeStruct((n_b, n_rows, LANES), F32)],
        compiler_params=pltpu.CompilerParams(dimension_semantics=("arbitrary", "arbitrary"),
                                             vmem_limit_bytes=VMEM_LIMIT),
        name="attn_sample",
    )(table, plist, pcount, qa8, qs8, gate8, ocmp, slcn, winn, sbn, state, u,
      *([mexp] * n_slots), *([slc_pool] * n_slots), *([sb_pool] * n_near))


def _sb_rest(table, need, qs8, accb, carry, u, sb_pool, n_pages, n_rest, chunk):
    n_b = qs8.shape[0]
    n_rows = SAMPLE_ROWS
    kern = functools.partial(_sb_rest_kernel, n_chunks=n_rest // chunk, chunk=chunk)
    per_b = lambda b, *_: (b, 0, 0)

    def page_spec(p):
        return pl.BlockSpec((1, 4 * LANES, LANES),
                            lambda b, t, nd: (t[jnp.where(nd[b] > 0, b * n_pages + p, 0)], 0, 0))

    grid_spec = pltpu.PrefetchScalarGridSpec(
        num_scalar_prefetch=2,
        grid=(n_b,),
        in_specs=[pl.BlockSpec((1, Q_PAD, 4 * LANES), per_b),
                  pl.BlockSpec((1, n_rows, 2 * LANES), per_b),
                  pl.BlockSpec((1, n_rows, LANES), per_b),
                  pl.BlockSpec((LANES, LANES), lambda b, *_: (0, 0))]
        + [page_spec(p) for p in range(n_rest)],
        out_specs=pl.BlockSpec((1, Q_PAD, 4 * LANES), per_b),
        scratch_shapes=[pltpu.VMEM((n_rows, 2 * LANES), F32),
                        pltpu.VMEM((n_rows, LANES), F32)],
    )
    return pl.pallas_call(
        kern,
        grid_spec=grid_spec,
        out_shape=jax.ShapeDtypeStruct((n_b, Q_PAD, 4 * LANES), F32),
        compiler_params=pltpu.CompilerParams(dimension_semantics=("arbitrary",),
                                             vmem_limit_bytes=VMEM_LIMIT),
        name="sb_rest",
    )(table, need, qs8, accb, carry, u, *([sb_pool] * n_rest))


def _head_index(base, perm):
    return np.concatenate([np.arange(base + h * HEAD_DIM, base + (h + 1) * HEAD_DIM) for h in perm])


def _largest_divisor(n, cap):
    d = min(n, cap)
    while n % d:
        d -= 1
    return d


def kernel(x_prompt, x_sample, cache_cmp_kv, cache_slc_kv, cache_sb_kv, state_win_kv, page_table,
           norm_g, w_in, q_norm_g, k_norm_g, cmp_pe, cmp_w1, cmp_b1, cmp_w2, w_out):
    n_b, t, d = x_prompt.shape
    n_db, dec_seq, _ = x_sample.shape
    depth = w_in.shape[0]
    assert depth == 1, "single-layer step"
    n_pages = page_table.shape[1]
    page = cache_cmp_kv.shape[2]
    past_len = n_pages * page
    n_state = state_win_kv.shape[2]
    assert page == LANES and t % LANES == 0 and t // SEL_BLOCK <= N_SEL_LANES
    assert past_len // SEL_BLOCK <= N_SEL_LANES and dec_seq <= Q_PAD and dec_seq <= SEL_BLOCK
    assert n_state == WINDOW and past_len >= WINDOW and t >= WINDOW

    nsa_w = NSA_HEADS * HEAD_DIM
    kvw = NSA_KV_GROUPS * HEAD_DIM
    sb_w = SB_HEADS * HEAD_DIM
    sbkv_w = SB_KV_HEADS * HEAD_DIM
    o_kv = nsa_w
    o_gt = o_kv + 6 * kvw
    o_za = o_gt + N_BRANCH * NSA_HEADS
    o_qs = o_za + nsa_w
    o_kb = o_qs + sb_w
    o_zb = o_kb + 2 * sbkv_w
    w0 = w_in[0]
    cols = np.concatenate([
        _head_index(0, PERM_A), np.arange(o_kv, o_gt), _head_index(o_za, PERM_A),
        _head_index(o_qs, PERM_B), np.arange(o_kb, o_zb), _head_index(o_zb, PERM_B),
        np.arange(o_gt, o_za)])
    w_perm = jnp.pad(w0[:, cols], ((0, 0), (0, LANES - N_BRANCH * NSA_HEADS))).astype(BF16)
    rows_out = np.concatenate([_head_index(0, PERM_A), _head_index(nsa_w, PERM_B)])
    w_out_perm = w_out[0][rows_out, :].astype(BF16)

    seg = jnp.asarray(np.kron(np.eye(2), np.ones((HEAD_DIM, HEAD_DIM))), BF16)
    tile2 = lambda v: jnp.tile(v, 2).reshape(1, LANES)
    gq = tile2(q_norm_g[0])
    gkc, gks, gkw = tile2(k_norm_g[0, 0]), tile2(k_norm_g[0, 1]), tile2(k_norm_g[0, 2])
    g_in = norm_g[0].reshape(1, d)

    w1 = cmp_w1[0].reshape(2, 2, CMP_STRIDE, HEAD_DIM, CMP_HIDDEN)
    w1 = jnp.transpose(w1, (0, 2, 3, 1, 4))
    zeros = jnp.zeros_like(w1)
    wc = jnp.stack([jnp.concatenate([w1, zeros], axis=3), jnp.concatenate([zeros, w1], axis=3)], axis=2)
    wc = wc.reshape(2, CMP_STRIDE // 2, 2 * LANES, 4 * CMP_HIDDEN).astype(BF16)
    w2 = cmp_w2[0]
    z2 = jnp.zeros_like(w2)
    w2p = jnp.stack([jnp.concatenate([w2, z2], axis=2), jnp.concatenate([z2, w2], axis=2)], axis=1).astype(BF16)
    cb = _cbias(cmp_pe[0].reshape(2, 1, CMP_LEN * HEAD_DIM), cmp_w1[0],
                cmp_b1[0].reshape(2, 1, CMP_HIDDEN))

    def selection_map(n_rows):
        r = np.arange(n_rows)[:, None]
        jb = np.arange(N_SEL_LANES)[None, :]
        start = (r - 1) * CMP_STRIDE
        ok = (r >= 1) & (start < jb * SEL_BLOCK + SEL_BLOCK) & (start + CMP_LEN > jb * SEL_BLOCK)
        return jnp.asarray(ok, BF16)

    u = jnp.asarray(np.arange(LANES)[:, None] > np.arange(LANES)[None, :], BF16)

    tm = _largest_divisor(n_b * t, 256)
    xp2 = x_prompt.reshape(n_b * t, d)
    (q16, cmpkv, slckv, winkv, slck, slcv, wink, winv, gates, sza, qs16, sbkv, sbk, sbv, szb) = _proj(
        xp2, g_in, w_perm, gq, gks, gkw, seg, tm)
    n_sub = t // CMP_STRIDE
    sub_w = CMP_STRIDE * 2 * kvw
    ck, cv = _compress_prompt(cmpkv.reshape(n_b * n_sub, sub_w), wc, cb, w2p, gkc, seg,
                              n_b, _largest_divisor(n_sub, 64))
    r3 = lambda a: a.reshape(n_b, t, a.shape[-1])
    oc, sel = _cmp_prompt(r3(q16), r3(gates), ck.reshape(n_b, n_sub, LANES), cv.reshape(n_b, n_sub, 2 * LANES),
                          selection_map(n_sub), _largest_divisor(t, CMP_Q_ROWS))
    oa = _slcwin_prompt(r3(q16), r3(gates), oc, sel, r3(slck), r3(slcv), r3(wink), r3(winv))
    ob = _sb_prompt(r3(qs16), r3(sbk), r3(sbv), u)
    y_prompt = _final(xp2, oa.reshape(n_b * t, nsa_w), sza, ob.reshape(n_b * t, sb_w), szb,
                      w_out_perm, tm).reshape(n_b, t, d)
    kv5 = lambda a, n, heads: a.reshape(1, n, -1, 2, heads, HEAD_DIM)
    cmp_p = kv5(cmpkv, n_b, NSA_KV_GROUPS)
    slc_p = kv5(slckv, n_b, NSA_KV_GROUPS)
    sb_p = kv5(sbkv, n_b, SB_KV_HEADS)
    win_p = kv5(winkv, n_b, NSA_KV_GROUPS)[:, :, t - min(WINDOW, t):]

    rows_s = n_db * dec_seq
    tms = _largest_divisor(rows_s, 256)
    xs2 = x_sample.reshape(rows_s, d)
    (q16s, cmpkv_s, slckv_s, winkv_s, _, _, _, _, gates_s, sza_s, qs16s, sbkv_s, _, _, szb_s) = _proj(
        xs2, g_in, w_perm, gq, gks, gkw, seg, tms)
    table = page_table.reshape(-1).astype(jnp.int32)
    n_sub_s = n_pages * (page // CMP_STRIDE)
    chan_major = lambda a: jnp.transpose(a, (0, 2, 3, 4, 1)).reshape(a.shape[0], -1, a.shape[1])
    cks, cvs = _compress_sample(table, chan_major(cache_cmp_kv[0]), wc, cb, w2p, gkc, seg,
                                n_db, n_pages, _largest_divisor(n_pages, COMPRESS_PAGES))
    pad8 = lambda a: jnp.pad(a.reshape(n_db, dec_seq, a.shape[-1]).astype(F32),
                             ((0, 0), (0, Q_PAD - dec_seq), (0, 0)))
    qa8, qs8, gate8 = pad8(q16s), pad8(qs16s), pad8(gates_s)
    n_blocks = -(-(past_len + dec_seq) // SEL_BLOCK)
    n_pick = min(SEL_TOPK, n_blocks) - 1
    eall = jnp.asarray(np.arange(N_SEL_LANES)[:, None] == (np.arange(past_len)[None, :] // SEL_BLOCK), BF16)
    ocmp, mexp, selblk = _cmp_sample(qa8, cks.reshape(n_db, n_sub_s, LANES), cvs.reshape(n_db, n_sub_s, 2 * LANES),
                                     selection_map(n_sub_s), eall, past_len, n_pick,
                                     _largest_divisor(n_db, CMP_SAMPLE_SEQS))
    n_slots = min(ATTN_PAGES, n_pages)
    n_list = -(-min(n_pages, NSA_KV_GROUPS * n_pick) // n_slots) * n_slots
    plist, pcount = _page_schedule(selblk, n_pages, n_list, n_slots)
    n_near = min(ATTN_PAGES, n_pages)
    sb_pool = chan_major(cache_sb_kv[0])
    oa8, accb, carry = _attn_sample(table, plist, pcount, qa8, qs8, gate8, ocmp, mexp,
                                    pad8(slckv_s), pad8(winkv_s), pad8(sbkv_s),
                                    chan_major(state_win_kv[0]), u, chan_major(cache_slc_kv[0]), sb_pool,
                                    n_pages, n_list, n_slots, n_near, past_len, dec_seq)
    need = (jnp.max(carry, axis=(1, 2)) > EXP_UNDERFLOW).astype(jnp.int32)
    n_rest = n_pages - n_near
    ob8 = _sb_rest(table, need, qs8, accb, carry, u, sb_pool, n_pages, n_rest,
                   _largest_divisor(n_rest, ATTN_PAGES) if n_rest else 1)
    y_sample = _final(xs2, oa8[:, :dec_seq].reshape(rows_s, nsa_w), sza_s,
                      ob8[:, :dec_seq].reshape(rows_s, sb_w), szb_s, w_out_perm, tms).reshape(n_db, dec_seq, d)
    kv5s = lambda a, heads: a.reshape(1, n_db, dec_seq, 2, heads, HEAD_DIM)
    cmp_s = kv5s(cmpkv_s, NSA_KV_GROUPS)
    slc_s = kv5s(slckv_s, NSA_KV_GROUPS)
    sb_s = kv5s(sbkv_s, SB_KV_HEADS)
    win_new = kv5s(winkv_s, NSA_KV_GROUPS)
    win_s = jnp.concatenate([state_win_kv[:, :, dec_seq:], win_new], axis=2)

    return (y_prompt, y_sample, cmp_p, cmp_s, slc_p, slc_s, sb_p, sb_s, win_p, win_s)
```

```python
import functools

import numpy as np
import jax
import jax.numpy as jnp
from jax import lax
from jax.experimental import pallas as pl
from jax.experimental.pallas import tpu as pltpu

F32 = jnp.float32
BF16 = jnp.bfloat16

HEAD_DIM = 64
NSA_HEADS = 8
SB_HEADS = 8
NSA_KV_GROUPS = 2
SB_KV_HEADS = 4
HEADS_PER_GROUP = NSA_HEADS // NSA_KV_GROUPS
N_BRANCH = 3
CMP_LEN = 32
CMP_STRIDE = 16
CMP_HIDDEN = 128
SEL_BLOCK = 64
SEL_TOPK = 16
WINDOW = 512
RMS_EPS = 1e-6
BIG = 1e30
PICKED = -3e38
EXP_UNDERFLOW = -104.0
SCALE = HEAD_DIM ** -0.5
SEL_SHIFT = 6
assert 1 << SEL_SHIFT == SEL_BLOCK

LANES = 128
SUBLANES = 8
N_SEL_LANES = 128
VMEM_LIMIT = 52 * 1024 * 1024
CMP_Q_ROWS = 512
CMP_SAMPLE_SEQS = 8
COMPRESS_PAGES = 64
ATTN_PAGES = 8
SLC_SLOTS = 16

PERM_A = (0, 4, 1, 5, 2, 6, 3, 7)
PERM_B = (0, 2, 1, 3, 4, 6, 5, 7)


def _slope(h):
    return float(2.0 ** (-(h + 1)))


def _dot(a, b):
    return jnp.dot(a, b, preferred_element_type=F32)


def _dot_nt(a, b):
    return lax.dot_general(a, b, (((1,), (1,)), ((), ())), preferred_element_type=F32)


def _dot_split(x, m):
    hi = x.astype(BF16)
    lo = (x - hi.astype(F32)).astype(BF16)
    return _dot(hi, m) + _dot(lo, m)


def _softplus(z):
    return jnp.maximum(z, 0.0) + jnp.log(1.0 + jnp.exp(-jnp.abs(z)))


def _silu(z):
    return z * jax.nn.sigmoid(z)


def _proj_kernel(x_ref, g_ref, w_ref, gq_ref, gks_ref, gkw_ref, seg_ref,
                 q_ref, cmp_ref, slc_ref, win_ref, slck_ref, slcv_ref, wink_ref, winv_ref,
                 gate_ref, sza_ref, qs_ref, sb_ref, sbk_ref, sbv_ref, szb_ref):
    x = x_ref[...]
    ms = jnp.mean(x * x, axis=-1, keepdims=True)
    hn = ((x * lax.rsqrt(ms + RMS_EPS)) * g_ref[...]).astype(BF16)
    seg = seg_ref[...]
    lane = lax.broadcasted_iota(jnp.int32, (x.shape[0], LANES), 1)
    low = lane < HEAD_DIM

    def sec(a, n=LANES):
        return _dot(hn, w_ref[:, a:a + n])

    def headnorm(t, gain):
        ss = _dot_split(t * t, seg) * (1.0 / HEAD_DIM)
        return (t * lax.rsqrt(ss + RMS_EPS)) * gain

    def variants(v):
        one = jnp.ones_like(v)
        return jnp.where(low, v, one).astype(BF16), jnp.where(low, one, v).astype(BF16)

    for c in range(4):
        t = headnorm(sec(c * LANES), gq_ref[...])
        q_ref[:, c * LANES:(c + 1) * LANES] = (t * SCALE).astype(BF16)
    base = 4 * LANES
    cmp_ref[:, 0:LANES] = sec(base)
    cmp_ref[:, LANES:2 * LANES] = sec(base + LANES)
    for kv_out, kk_out, vv_out, gain_ref, off in (
            (slc_ref, slck_ref, slcv_ref, gks_ref, base + 2 * LANES),
            (win_ref, wink_ref, winv_ref, gkw_ref, base + 4 * LANES)):
        kn = headnorm(sec(off), gain_ref[...])
        v = sec(off + LANES)
        kv_out[:, 0:LANES] = kn
        kv_out[:, LANES:2 * LANES] = v
        kk_out[...] = kn.astype(BF16)
        v0, v1 = variants(v)
        vv_out[:, 0:LANES] = v0
        vv_out[:, LANES:2 * LANES] = v1
    base = 10 * LANES
    for c in range(4):
        sza_ref[:, c * LANES:(c + 1) * LANES] = _silu(sec(base + c * LANES))
    base = 14 * LANES
    for c in range(4):
        qs_ref[:, c * LANES:(c + 1) * LANES] = (sec(base + c * LANES) * SCALE).astype(BF16)
    base = 18 * LANES
    for c in range(2):
        kb = sec(base + c * LANES)
        vb = sec(base + (2 + c) * LANES)
        sb_ref[:, c * LANES:(c + 1) * LANES] = kb
        sb_ref[:, (2 + c) * LANES:(3 + c) * LANES] = vb
        sbk_ref[:, c * LANES:(c + 1) * LANES] = kb.astype(BF16)
        sbv_ref[:, c * LANES:(c + 1) * LANES] = vb.astype(BF16)
    base = 22 * LANES
    for c in range(4):
        szb_ref[:, c * LANES:(c + 1) * LANES] = _silu(sec(base + c * LANES))
    gate_ref[...] = jax.nn.sigmoid(sec(26 * LANES))


def _proj(x2, g, w_perm, gq, gks, gkw, seg, tm):
    rows, d = x2.shape
    wcols = w_perm.shape[1]
    row = lambda i: (i, 0)
    const = lambda i: (0, 0)
    widths = [(512, BF16), (256, F32), (256, F32), (256, F32), (128, BF16), (256, BF16),
              (128, BF16), (256, BF16), (128, F32), (512, F32), (512, BF16), (512, F32),
              (256, BF16), (256, BF16), (512, F32)]
    return pl.pallas_call(
        _proj_kernel,
        grid=(rows // tm,),
        in_specs=[pl.BlockSpec((tm, d), row), pl.BlockSpec((1, d), const),
                  pl.BlockSpec((d, wcols), const), pl.BlockSpec((1, LANES), const),
                  pl.BlockSpec((1, LANES), const), pl.BlockSpec((1, LANES), const),
                  pl.BlockSpec((LANES, LANES), const)],
        out_specs=[pl.BlockSpec((tm, w), row) for w, _ in widths],
        out_shape=[jax.ShapeDtypeStruct((rows, w), dt) for w, dt in widths],
        compiler_params=pltpu.CompilerParams(dimension_semantics=("arbitrary",),
                                             vmem_limit_bytes=VMEM_LIMIT),
        name="proj",
    )(x2, g, w_perm, gq, gks, gkw, seg)


def _cbias_kernel(pe_ref, w1_ref, b1_ref, o_ref):
    for kv in range(2):
        pe = jnp.broadcast_to(pe_ref[kv], (SUBLANES, pe_ref.shape[-1])).astype(BF16)
        o_ref[kv] = _dot(pe, w1_ref[kv].astype(BF16)) + b1_ref[kv]


def _cbias(pe_flat, w1, b1):
    return pl.pallas_call(
        _cbias_kernel,
        out_shape=jax.ShapeDtypeStruct((2, SUBLANES, CMP_HIDDEN), F32),
        name="cbias",
    )(pe_flat, w1, b1)


def _compress_kernel(*refs, n_in, has_table):
    if has_table:
        refs = refs[1:]
    x_refs = refs[:n_in]
    if has_table:
        perm_ref = refs[n_in]
        refs = refs[1:]
    wc_ref, cb_ref, w2_ref, gk_ref, seg_ref = refs[n_in:n_in + 5]
    ck_ref, cv_ref = refs[n_in + 5:n_in + 7]
    carry_ref = refs[n_in + 7]
    n_rows = ck_ref.shape[0]

    @pl.when(pl.program_id(1) == 0)
    def _():
        carry_ref[...] = jnp.zeros(carry_ref.shape, F32)

    row = lax.broadcasted_iota(jnp.int32, (n_rows, CMP_HIDDEN), 0)
    lane = lax.broadcasted_iota(jnp.int32, (n_rows, LANES), 1)
    low = lane < HEAD_DIM
    for kv in range(2):
        if has_table:
            perm = perm_ref[...]
            pages = [_dot_nt(perm, r[0, kv * LANES:(kv + 1) * LANES, :].astype(BF16)) for r in x_refs]
        def sub_rows(s):
            if has_table:
                nb = LANES // CMP_STRIDE
                return jnp.concatenate([pg[s * nb:(s + 1) * nb] for pg in pages], axis=0)
            a = s * 2 * LANES + kv * LANES
            return x_refs[0][:, a:a + LANES]

        acc = jnp.zeros((n_rows, 4 * CMP_HIDDEN), F32)
        for s2 in range(CMP_STRIDE // 2):
            xk = jnp.concatenate([sub_rows(2 * s2), sub_rows(2 * s2 + 1)], axis=1)
            acc = acc + _dot(xk.astype(BF16), wc_ref[kv, s2])
        cb = cb_ref[kv, 0:1, :]
        out = jnp.zeros((n_rows, LANES), F32)
        for g in range(NSA_KV_GROUPS):
            a0 = acc[:, (2 * g) * CMP_HIDDEN:(2 * g + 1) * CMP_HIDDEN]
            a1 = acc[:, (2 * g + 1) * CMP_HIDDEN:(2 * g + 2) * CMP_HIDDEN]
            prev = carry_ref[kv, SUBLANES - 1:SUBLANES, (2 * g) * CMP_HIDDEN:(2 * g + 1) * CMP_HIDDEN]
            shifted = jnp.where(row == 0, prev, pltpu.roll(a0, 1, axis=0))
            h = shifted + a1 + cb
            out = out + _dot(_silu(h).astype(BF16), w2_ref[kv, g])
        carry_ref[kv] = acc[n_rows - SUBLANES:, :]
        if kv == 0:
            ss = _dot_split(out * out, seg_ref[...]) * (1.0 / HEAD_DIM)
            ck_ref[...] = ((out * lax.rsqrt(ss + RMS_EPS)) * gk_ref[...]).astype(BF16)
        else:
            one = jnp.ones_like(out)
            cv_ref[:, 0:LANES] = jnp.where(low, out, one).astype(BF16)
            cv_ref[:, LANES:2 * LANES] = jnp.where(low, one, out).astype(BF16)


def _compress_out_shape(n_out_rows):
    return [jax.ShapeDtypeStruct((n_out_rows, LANES), BF16),
            jax.ShapeDtypeStruct((n_out_rows, 2 * LANES), BF16)]


def _compress_weight_specs():
    return [pl.BlockSpec((2, CMP_STRIDE // 2, 2 * LANES, 4 * CMP_HIDDEN), lambda *a: (0, 0, 0, 0)),
            pl.BlockSpec((2, SUBLANES, CMP_HIDDEN), lambda *a: (0, 0, 0)),
            pl.BlockSpec((2, NSA_KV_GROUPS, CMP_HIDDEN, LANES), lambda *a: (0, 0, 0, 0)),
            pl.BlockSpec((1, LANES), lambda *a: (0, 0)),
            pl.BlockSpec((LANES, LANES), lambda *a: (0, 0))]


def _compress_prompt(xsub, wc, cb, w2p, gk, seg, n_batch, rows_per_step):
    total = xsub.shape[0]
    per_b = total // n_batch
    nj = per_b // rows_per_step
    kern = functools.partial(_compress_kernel, n_in=1, has_table=False)
    return pl.pallas_call(
        kern,
        grid=(n_batch, nj),
        in_specs=[pl.BlockSpec((rows_per_step, xsub.shape[1]), lambda b, j: (b * nj + j, 0))]
        + _compress_weight_specs(),
        out_specs=[pl.BlockSpec((rows_per_step, LANES), lambda b, j: (b * nj + j, 0)),
                   pl.BlockSpec((rows_per_step, 2 * LANES), lambda b, j: (b * nj + j, 0))],
        out_shape=_compress_out_shape(total),
        scratch_shapes=[pltpu.VMEM((2, SUBLANES, 4 * CMP_HIDDEN), F32)],
        compiler_params=pltpu.CompilerParams(dimension_semantics=("arbitrary", "arbitrary"),
                                             vmem_limit_bytes=VMEM_LIMIT),
        name="compress_prompt",
    )(xsub, wc, cb, w2p, gk, seg)


def _compress_sample(table, pool_t, wc, cb, w2p, gk, seg, n_batch, n_pages, pages_per_step):
    page = pool_t.shape[2]
    sub_per_page = page // CMP_STRIDE
    nj = n_pages // pages_per_step
    rows_per_step = pages_per_step * sub_per_page
    total = n_batch * n_pages * sub_per_page
    kern = functools.partial(_compress_kernel, n_in=pages_per_step, has_table=True)
    rho = np.arange(page)
    perm = jnp.asarray(np.arange(page)[None, :] == (CMP_STRIDE * (rho % sub_per_page) + rho // sub_per_page)[:, None],
                       BF16)

    def page_spec(p):
        return pl.BlockSpec((1, pool_t.shape[1], page),
                            lambda b, j, t: (t[b * n_pages + j * pages_per_step + p], 0, 0))

    grid_spec = pltpu.PrefetchScalarGridSpec(
        num_scalar_prefetch=1,
        grid=(n_batch, nj),
        in_specs=[page_spec(p) for p in range(pages_per_step)]
        + [pl.BlockSpec((page, page), lambda b, j, t: (0, 0))] + _compress_weight_specs(),
        out_specs=[pl.BlockSpec((rows_per_step, LANES), lambda b, j, t: (b * nj + j, 0)),
                   pl.BlockSpec((rows_per_step, 2 * LANES), lambda b, j, t: (b * nj + j, 0))],
        scratch_shapes=[pltpu.VMEM((2, SUBLANES, 4 * CMP_HIDDEN), F32)],
    )
    return pl.pallas_call(
        kern,
        grid_spec=grid_spec,
        out_shape=_compress_out_shape(total),
        compiler_params=pltpu.CompilerParams(dimension_semantics=("arbitrary", "arbitrary"),
                                             vmem_limit_bytes=VMEM_LIMIT),
        name="compress_sample",
    )(table, *([pool_t] * pages_per_step), perm, wc, cb, w2p, gk, seg)


def _pick_blocks(v_s, sel_s, lane_f, n_pick):
    sel_s[...] = jnp.zeros(sel_s.shape, F32)

    def body(_, c):
        v = v_s[...]
        mx = jnp.max(v, axis=1, keepdims=True)
        cand = jnp.where(v == mx, lane_f, float(2 * N_SEL_LANES))
        idx = jnp.min(cand, axis=1, keepdims=True)
        hit = lane_f == idx
        sel_s[...] = jnp.where(hit, 1.0, sel_s[...])
        v_s[...] = jnp.where(hit, PICKED, v)
        return c

    lax.fori_loop(0, n_pick, body, 0)


def _cmp_probs(qh, ck, slope, crel, cmask):
    s = _dot_nt(qh, ck) - slope * crel
    s = jnp.where(cmask, s, -BIG)
    m = jnp.max(s, axis=1, keepdims=True)
    e = jnp.where(cmask, jnp.exp(s - m), 0.0)
    return e / jnp.maximum(jnp.sum(e, axis=1, keepdims=True), 1e-30)


def _normalise(acc):
    return acc / jnp.maximum(pltpu.roll(acc, HEAD_DIM, axis=1), 1e-30)


def _sb_weights(z, causal, carry, u):
    sp = _softplus(z)
    l1m = -sp if causal is None else jnp.where(causal, -sp, 0.0)
    a = jnp.exp((z - sp) + _dot_split(l1m, u) + carry)
    if causal is not None:
        a = jnp.where(causal, a, 0.0)
    return a, jnp.sum(l1m, axis=1, keepdims=True)


def _tile_sweep(n_tiles, kb_of, tile, live=None, pair=None):
    tile(kb_of(0), True)
    n_rest = n_tiles - 1
    odd = n_rest & 1

    @pl.when(odd == 1)
    def _():
        tile(kb_of(1), False)

    start = 1 + odd
    n_trips = n_rest >> 1

    def run(t):
        if pair is None:
            tile(kb_of(start + 2 * t), False)
            tile(kb_of(start + 2 * t + 1), False)
        else:
            pair(kb_of(start + 2 * t), kb_of(start + 2 * t + 1))

    if live is None:
        def body(t, c):
            run(t)
            return c

        lax.fori_loop(0, n_trips, body, 0)
    else:
        def cond(state):
            t, go = state
            return jnp.logical_and(t < n_trips, go)

        def wbody(state):
            run(state[0])
            return state[0] + 1, live()

        lax.while_loop(cond, wbody, (jnp.int32(0), live()))


def _cmp_prompt_kernel(q_ref, gate_ref, ck_ref, cv_ref, selmap_ref, oc_ref, sel_ref, v_s, sel_s):
    nq = q_ref.shape[1]
    q0 = pl.program_id(1) * nq
    n_c = ck_ref.shape[1]
    lane = lax.broadcasted_iota(jnp.int32, (nq, LANES), 1)
    low = lane < HEAD_DIM
    crow = lax.broadcasted_iota(jnp.int32, (nq, n_c), 1)
    cend = crow * CMP_STRIDE + (CMP_STRIDE - 1)
    cq = q0 + lax.broadcasted_iota(jnp.int32, (nq, n_c), 0)
    cmask = jnp.logical_and(cend <= cq, crow >= 1)
    crel = (cq - cend).astype(F32)
    ck = ck_ref[0]
    cur = (q0 + lax.broadcasted_iota(jnp.int32, (nq, LANES), 0)) >> SEL_SHIFT
    forced = jnp.logical_or(lane == 0, jnp.logical_or(lane == cur, lane == cur - 1))
    allowed = lane <= cur
    outs = [[None] * HEADS_PER_GROUP for _ in range(NSA_KV_GROUPS)]
    for g in range(NSA_KV_GROUPS):
        keep = low if g == 0 else jnp.logical_not(low)
        imp = jnp.zeros((nq, N_SEL_LANES), F32)
        for c in range(HEADS_PER_GROUP):
            h = g * HEADS_PER_GROUP + c
            qc = q_ref[0, :, c * LANES:(c + 1) * LANES]
            p = _cmp_probs(jnp.where(keep, qc, jnp.zeros_like(qc)), ck, _slope(h), crel, cmask)
            oc = _dot(p.astype(BF16), cv_ref[0, :, g * LANES:(g + 1) * LANES])
            outs[g][c] = oc * gate_ref[0, :, h:h + 1]
            imp = imp + _dot_split(p, selmap_ref[...])
        v_s[g * nq:(g + 1) * nq, :] = jnp.where(forced, BIG, jnp.where(allowed, imp, -BIG))
    for c in range(HEADS_PER_GROUP):
        oc_ref[0, :, c * LANES:(c + 1) * LANES] = jnp.where(low, outs[0][c], outs[1][c])
    lane_f = lax.broadcasted_iota(jnp.int32, v_s.shape, 1).astype(F32)
    _pick_blocks(v_s, sel_s, lane_f, SEL_TOPK)
    for g in range(NSA_KV_GROUPS):
        sel_ref[0, g] = jnp.where(allowed, sel_s[g * nq:(g + 1) * nq, :], 0.0).astype(BF16)


def _cmp_prompt(q, gates, ck, cv, selmap, nq):
    n_b, t, _ = q.shape
    n_c = ck.shape[1]
    per_q = lambda b, i: (b, i, 0)
    per_b = lambda b, i: (b, 0, 0)
    return pl.pallas_call(
        _cmp_prompt_kernel,
        grid=(n_b, t // nq),
        in_specs=[pl.BlockSpec((1, nq, 4 * LANES), per_q),
                  pl.BlockSpec((1, nq, LANES), per_q),
                  pl.BlockSpec((1, n_c, LANES), per_b),
                  pl.BlockSpec((1, n_c, 2 * LANES), per_b),
                  pl.BlockSpec((n_c, N_SEL_LANES), lambda b, i: (0, 0))],
        out_specs=[pl.BlockSpec((1, nq, 4 * LANES), per_q),
                   pl.BlockSpec((1, NSA_KV_GROUPS, nq, N_SEL_LANES), lambda b, i: (b, 0, i, 0))],
        out_shape=[jax.ShapeDtypeStruct((n_b, t, 4 * LANES), F32),
                   jax.ShapeDtypeStruct((n_b, NSA_KV_GROUPS, t, N_SEL_LANES), BF16)],
        scratch_shapes=[pltpu.VMEM((NSA_KV_GROUPS * nq, N_SEL_LANES), F32),
                        pltpu.VMEM((NSA_KV_GROUPS * nq, N_SEL_LANES), F32)],
        compiler_params=pltpu.CompilerParams(dimension_semantics=("arbitrary", "arbitrary"),
                                             vmem_limit_bytes=VMEM_LIMIT),
        name="cmp_prompt",
    )(q, gates, ck, cv, selmap)


def _slcwin_prompt_kernel(tiles_ref, count_ref, q_ref, gate_ref, oc_ref, sel_ref, sk_ref, sv_ref, wk_ref, wv_ref,
                          o_ref, qm_s, m_s, acc_s, tot_s):
    i = pl.program_id(1)
    step = pl.program_id(0) * pl.num_programs(1) + i
    tiles_per_step = sk_ref.shape[1] // LANES
    nq = LANES
    lane = lax.broadcasted_iota(jnp.int32, (nq, LANES), 1)
    row = lax.broadcasted_iota(jnp.int32, (nq, LANES), 0)
    low = lane < HEAD_DIM
    qpos = i * nq + row

    for g in range(NSA_KV_GROUPS):
        keep = low if g == 0 else jnp.logical_not(low)
        for c in range(HEADS_PER_GROUP):
            qc = q_ref[0, :, c * LANES:(c + 1) * LANES]
            r0 = (g * HEADS_PER_GROUP + c) * nq
            qm_s[r0:r0 + nq, :] = jnp.where(keep, qc, jnp.zeros_like(qc))

    def run_branch(k_ref, v_ref, banded, n_tiles, kb_of, gate_base, first_branch):
        m_s[...] = jnp.full(m_s.shape, -BIG, F32)
        acc_s[...] = jnp.zeros(acc_s.shape, F32)

        def tiles(kbs, diag):
            k0s = [pl.multiple_of(kb * LANES, LANES) for kb in kbs]
            rels = [qpos - (k0 + lane) for k0 in k0s]
            kt = jnp.concatenate([k_ref[0, pl.ds(k0, LANES), :] for k0 in k0s], axis=0)
            s = _dot_nt(qm_s[...], kt)
            relf = jnp.concatenate([rel.astype(F32) for rel in rels], axis=1)
            if banded:
                band = jnp.concatenate(
                    [jnp.where(jnp.logical_and(rel >= 0, rel < WINDOW), 0.0, -BIG) for rel in rels], axis=1)
            else:
                expands = [jnp.where(row == (LANES // SEL_BLOCK) * kb + (lane >> SEL_SHIFT), 1.0, 0.0).astype(BF16)
                           for kb in kbs]
            for g in range(NSA_KV_GROUPS):
                if banded:
                    bias = band
                else:
                    parts = [(_dot(sel_ref[0, g], e) - 1.0) * BIG for e in expands]
                    if diag:
                        parts = [jnp.where(rel >= 0, b, -BIG) for rel, b in zip(rels, parts)]
                    bias = jnp.concatenate(parts, axis=1)
                ps, alphas = [], []
                for c in range(HEADS_PER_GROUP):
                    h = g * HEADS_PER_GROUP + c
                    r0 = h * nq
                    sh = (s[r0:r0 + nq] - _slope(h) * relf) + bias
                    m_old = m_s[r0:r0 + nq, :]
                    m_new = jnp.maximum(m_old, jnp.max(sh, axis=1, keepdims=True))
                    ps.append(jnp.exp(sh - jnp.concatenate([m_new] * len(kbs), axis=1)).astype(BF16))
                    alphas.append(jnp.exp(m_old - m_new))
                    m_s[r0:r0 + nq, :] = m_new
                g0 = g * HEADS_PER_GROUP * nq
                g1 = (g + 1) * HEADS_PER_GROUP * nq
                vt = jnp.concatenate([v_ref[0, pl.ds(k0, LANES), g * LANES:(g + 1) * LANES] for k0 in k0s], axis=0)
                pv = _dot(jnp.concatenate(ps, axis=0), vt)
                acc_s[g0:g1, :] = jnp.concatenate(alphas, axis=0) * acc_s[g0:g1, :] + pv

        _tile_sweep(n_tiles, kb_of, lambda kb, diag: tiles([kb], diag),
                    pair=lambda ka, kb: tiles([ka, kb], False))
        for h in range(NSA_HEADS):
            r0 = h * nq
            o = _normalise(acc_s[r0:r0 + nq, :]) * gate_ref[0, :, gate_base + h:gate_base + h + 1]
            tot_s[r0:r0 + nq, :] = o if first_branch else tot_s[r0:r0 + nq, :] + o

    run_branch(sk_ref, sv_ref, False, count_ref[step], lambda t: tiles_ref[step * tiles_per_step + t],
               NSA_HEADS, True)
    run_branch(wk_ref, wv_ref, True, jnp.minimum(i, WINDOW // LANES) + 1, lambda t: i - t,
               2 * NSA_HEADS, False)

    for c in range(HEADS_PER_GROUP):
        a = tot_s[c * nq:(c + 1) * nq, :]
        b = tot_s[(HEADS_PER_GROUP + c) * nq:(HEADS_PER_GROUP + c + 1) * nq, :]
        o_ref[0, :, c * LANES:(c + 1) * LANES] = oc_ref[0, :, c * LANES:(c + 1) * LANES] + jnp.where(low, a, b)


def _tile_schedule(sel):
    n_b, _, t, n_blk = sel.shape
    nqb = t // LANES
    per_tile = LANES // SEL_BLOCK
    used = jnp.max(sel.reshape(n_b, NSA_KV_GROUPS, nqb, LANES, n_blk // per_tile, per_tile), axis=(1, 3, 5)) > 0
    kb = jnp.arange(n_blk // per_tile, dtype=jnp.int32)
    used = jnp.logical_and(used[:, :, :nqb], kb[None, None, :nqb] <= jnp.arange(nqb, dtype=jnp.int32)[None, :, None])
    order = -jnp.sort(-jnp.where(used, kb[None, None, :nqb], -1), axis=-1)
    return jnp.maximum(order, 0).reshape(-1).astype(jnp.int32), jnp.sum(used, axis=-1).reshape(-1).astype(jnp.int32)


def _slcwin_prompt(q, gates, oc, sel, sk, sv, wk, wv):
    n_b, t, _ = q.shape
    nqb = t // LANES
    n_rows = NSA_HEADS * LANES
    tiles, counts = _tile_schedule(sel)
    per_q = lambda b, i, *_: (b, i, 0)
    per_b = lambda b, i, *_: (b, 0, 0)
    grid_spec = pltpu.PrefetchScalarGridSpec(
        num_scalar_prefetch=2,
        grid=(n_b, nqb),
        in_specs=[pl.BlockSpec((1, LANES, 4 * LANES), per_q),
                  pl.BlockSpec((1, LANES, LANES), per_q),
                  pl.BlockSpec((1, LANES, 4 * LANES), per_q),
                  pl.BlockSpec((1, NSA_KV_GROUPS, LANES, N_SEL_LANES), lambda b, i, *_: (b, 0, i, 0)),
                  pl.BlockSpec((1, t, LANES), per_b),
                  pl.BlockSpec((1, t, 2 * LANES), per_b),
                  pl.BlockSpec((1, t, LANES), per_b),
                  pl.BlockSpec((1, t, 2 * LANES), per_b)],
        out_specs=pl.BlockSpec((1, LANES, 4 * LANES), per_q),
        scratch_shapes=[pltpu.VMEM((n_rows, LANES), BF16),
                        pltpu.VMEM((n_rows, LANES), F32),
                        pltpu.VMEM((n_rows, LANES), F32),
                        pltpu.VMEM((n_rows, LANES), F32)],
    )
    return pl.pallas_call(
        _slcwin_prompt_kernel,
        grid_spec=grid_spec,
        out_shape=jax.ShapeDtypeStruct((n_b, t, 4 * LANES), F32),
        compiler_params=pltpu.CompilerParams(dimension_semantics=("arbitrary", "arbitrary"),
                                             vmem_limit_bytes=VMEM_LIMIT),
        name="slcwin_prompt",
    )(tiles, counts, q, gates, oc, sel, sk, sv, wk, wv)


def _sb_prompt_kernel(q_ref, k_ref, v_ref, u_ref, o_ref, qm_s, carry_s, acc_s):
    i = pl.program_id(1)
    nq = LANES
    n_pairs = SB_KV_HEADS // 2
    rows = 4 * nq
    lane = lax.broadcasted_iota(jnp.int32, (rows, LANES), 1)
    qrow = lax.broadcasted_iota(jnp.int32, (rows, LANES), 0) & (nq - 1)
    low = lax.broadcasted_iota(jnp.int32, (nq, LANES), 1) < HEAD_DIM
    qpos = i * nq + qrow
    for kp in range(n_pairs):
        for r in range(2):
            ch = 2 * kp + r
            qc = q_ref[0, :, ch * LANES:(ch + 1) * LANES]
            zero = jnp.zeros_like(qc)
            qm_s[kp, (2 * r) * nq:(2 * r + 1) * nq, :] = jnp.where(low, qc, zero)
            qm_s[kp, (2 * r + 1) * nq:(2 * r + 2) * nq, :] = jnp.where(low, zero, qc)
    carry_s[...] = jnp.zeros(carry_s.shape, F32)
    acc_s[...] = jnp.zeros(acc_s.shape, F32)
    u = u_ref[...]

    def tile(kb, diag):
        k0 = pl.multiple_of(kb * LANES, LANES)
        causal = (k0 + lane) < qpos if diag else None
        for kp in range(n_pairs):
            kt = k_ref[0, pl.ds(k0, LANES), kp * LANES:(kp + 1) * LANES]
            vt = v_ref[0, pl.ds(k0, LANES), kp * LANES:(kp + 1) * LANES]
            z = _dot_nt(qm_s[kp], kt)
            a, tile_sum = _sb_weights(z, causal, carry_s[kp], u)
            acc_s[kp] = acc_s[kp] + _dot(a.astype(BF16), vt)
            carry_s[kp] = carry_s[kp] + tile_sum

    def live():
        c = carry_s[0]
        for kp in range(1, n_pairs):
            c = jnp.maximum(c, carry_s[kp])
        return jnp.max(c) > EXP_UNDERFLOW

    _tile_sweep(i + 1, lambda t: i - t, tile, live)
    for kp in range(n_pairs):
        for r in range(2):
            ch = 2 * kp + r
            o_ref[0, :, ch * LANES:(ch + 1) * LANES] = jnp.where(
                low, acc_s[kp, (2 * r) * nq:(2 * r + 1) * nq, :], acc_s[kp, (2 * r + 1) * nq:(2 * r + 2) * nq, :])


def _sb_prompt(qs, sbk, sbv, u):
    n_b, t, _ = qs.shape
    nqb = t // LANES
    n_pairs = SB_KV_HEADS // 2
    per_q = lambda b, i: (b, i, 0)
    per_b = lambda b, i: (b, 0, 0)
    return pl.pallas_call(
        _sb_prompt_kernel,
        grid=(n_b, nqb),
        in_specs=[pl.BlockSpec((1, LANES, 4 * LANES), per_q),
                  pl.BlockSpec((1, t, n_pairs * LANES), per_b),
                  pl.BlockSpec((1, t, n_pairs * LANES), per_b),
                  pl.BlockSpec((LANES, LANES), lambda b, i: (0, 0))],
        out_specs=pl.BlockSpec((1, LANES, 4 * LANES), per_q),
        out_shape=jax.ShapeDtypeStruct((n_b, t, 4 * LANES), F32),
        scratch_shapes=[pltpu.VMEM((n_pairs, 4 * LANES, LANES), BF16),
                        pltpu.VMEM((n_pairs, 4 * LANES, LANES), F32),
                        pltpu.VMEM((n_pairs, 4 * LANES, LANES), F32)],
        compiler_params=pltpu.CompilerParams(dimension_semantics=("arbitrary", "arbitrary"),
                                             vmem_limit_bytes=VMEM_LIMIT),
        name="sb_prompt",
    )(qs, sbk, sbv, u)


def _final_kernel(x_ref, oa_ref, sza_ref, ob_ref, szb_ref, w_ref, y_ref):
    half = oa_ref.shape[1]
    ma = (oa_ref[...] * sza_ref[...]).astype(BF16)
    mb = (ob_ref[...] * szb_ref[...]).astype(BF16)
    y_ref[...] = x_ref[...] + _dot(ma, w_ref[0:half, :]) + _dot(mb, w_ref[half:2 * half, :])


def _final(x2, oa, sza, ob, szb, w_out_perm, tm):
    rows, d = x2.shape
    half = oa.shape[1]
    row = lambda i: (i, 0)
    return pl.pallas_call(
        _final_kernel,
        grid=(rows // tm,),
        in_specs=[pl.BlockSpec((tm, d), row), pl.BlockSpec((tm, half), row),
                  pl.BlockSpec((tm, half), row), pl.BlockSpec((tm, half), row),
                  pl.BlockSpec((tm, half), row), pl.BlockSpec((d, d), lambda i: (0, 0))],
        out_specs=pl.BlockSpec((tm, d), row),
        out_shape=jax.ShapeDtypeStruct((rows, d), F32),
        compiler_params=pltpu.CompilerParams(dimension_semantics=("arbitrary",),
                                             vmem_limit_bytes=VMEM_LIMIT),
        name="final",
    )(x2, oa, sza, ob, szb, w_out_perm)


Q_PAD = SUBLANES
Q_SHIFT = 3
assert 1 << Q_SHIFT == Q_PAD
SAMPLE_ROWS = NSA_HEADS * Q_PAD


def _row_ids(shape):
    r = lax.broadcasted_iota(jnp.int32, shape, 0)
    return r >> Q_SHIFT, r & (Q_PAD - 1)


def _head_slopes(shape):
    hidx, _ = _row_ids(shape)
    out = jnp.zeros(shape, F32)
    for h in range(NSA_HEADS):
        out = jnp.where(hidx == h, _slope(h), out)
    return out


def _stack_nsa_queries(q):
    lane = lax.broadcasted_iota(jnp.int32, (Q_PAD, LANES), 1)
    low = lane < HEAD_DIM
    parts = []
    for g in range(NSA_KV_GROUPS):
        keep = low if g == 0 else jnp.logical_not(low)
        for c in range(HEADS_PER_GROUP):
            parts.append(jnp.where(keep, q[:, c * LANES:(c + 1) * LANES], 0.0))
    return jnp.concatenate(parts, axis=0).astype(BF16)


def _cmp_sample_kernel(q_ref, ck_ref, cv_ref, selmap_ref, eall_ref, ocmp_ref, mexp_ref, selblk_ref, v_s, sel_s,
                       *, past_len, n_pick):
    n_seq = q_ref.shape[0]
    n_rows = SAMPLE_ROWS
    half = n_rows // 2
    n_c = ck_ref.shape[1]
    crow = lax.broadcasted_iota(jnp.int32, (n_rows, n_c), 1)
    cend = crow * CMP_STRIDE + (CMP_STRIDE - 1)
    cq = past_len + _row_ids((n_rows, n_c))[1]
    cmask = jnp.logical_and(cend <= cq, crow >= 1)
    slope = _head_slopes((n_rows, n_c))
    crel = (cq - cend).astype(F32)
    n_past_blocks = past_len // SEL_BLOCK
    lane = lax.broadcasted_iota(jnp.int32, (NSA_KV_GROUPS * Q_PAD, N_SEL_LANES), 1)
    forced = jnp.logical_or(lane == 0, lane == n_past_blocks - 1)
    allowed = lane < n_past_blocks
    rows_seq = NSA_KV_GROUPS * Q_PAD
    for b in range(n_seq):
        p = _cmp_probs(_stack_nsa_queries(q_ref[b]), ck_ref[b], slope, crel, cmask)
        pb = p.astype(BF16)
        ocmp_ref[b, 0:half, :] = _dot(pb[0:half], cv_ref[b, :, 0:LANES])
        ocmp_ref[b, half:n_rows, :] = _dot(pb[half:n_rows], cv_ref[b, :, LANES:2 * LANES])
        pg = []
        for g in range(NSA_KV_GROUPS):
            acc = jnp.zeros((Q_PAD, n_c), F32)
            for c in range(HEADS_PER_GROUP):
                r0 = (g * HEADS_PER_GROUP + c) * Q_PAD
                acc = acc + p[r0:r0 + Q_PAD]
            pg.append(acc)
        imp = _dot_split(jnp.concatenate(pg, axis=0), selmap_ref[...])
        v_s[b * rows_seq:(b + 1) * rows_seq, :] = jnp.where(allowed, jnp.where(forced, BIG, imp), -BIG)
    lane_f = lax.broadcasted_iota(jnp.int32, v_s.shape, 1).astype(F32)
    _pick_blocks(v_s, sel_s, lane_f, n_pick)
    allowed_all = lax.broadcasted_iota(jnp.int32, v_s.shape, 1) < n_past_blocks
    sel32 = jnp.where(allowed_all, sel_s[...], 0.0)
    mexp = _dot(sel32.astype(BF16), eall_ref[...])
    for b in range(n_seq):
        mexp_ref[b] = mexp[b * rows_seq:(b + 1) * rows_seq]
        selblk_ref[b] = sel32[b * rows_seq:(b + 1) * rows_seq]


def _cmp_sample(q8, ck, cv, selmap, eall, past_len, n_pick, n_seq):
    n_b = q8.shape[0]
    n_c = ck.shape[1]
    rows_seq = NSA_KV_GROUPS * Q_PAD
    kern = functools.partial(_cmp_sample_kernel, past_len=past_len, n_pick=n_pick)
    blk = lambda b: (b, 0, 0)
    return pl.pallas_call(
        kern,
        grid=(n_b // n_seq,),
        in_specs=[pl.BlockSpec((n_seq, Q_PAD, 4 * LANES), blk),
                  pl.BlockSpec((n_seq, n_c, LANES), blk),
                  pl.BlockSpec((n_seq, n_c, 2 * LANES), blk),
                  pl.BlockSpec((n_c, N_SEL_LANES), lambda b: (0, 0)),
                  pl.BlockSpec((N_SEL_LANES, past_len), lambda b: (0, 0))],
        out_specs=[pl.BlockSpec((n_seq, SAMPLE_ROWS, LANES), blk),
                   pl.BlockSpec((n_seq, rows_seq, past_len), blk),
                   pl.BlockSpec((n_seq, rows_seq, N_SEL_LANES), blk)],
        out_shape=[jax.ShapeDtypeStruct((n_b, SAMPLE_ROWS, LANES), F32),
                   jax.ShapeDtypeStruct((n_b, rows_seq, past_len), F32),
                   jax.ShapeDtypeStruct((n_b, rows_seq, N_SEL_LANES), F32)],
        scratch_shapes=[pltpu.VMEM((n_seq * rows_seq, N_SEL_LANES), F32),
                        pltpu.VMEM((n_seq * rows_seq, N_SEL_LANES), F32)],
        compiler_params=pltpu.CompilerParams(dimension_semantics=("arbitrary",),
                                             vmem_limit_bytes=VMEM_LIMIT),
        name="cmp_sample",
    )(q8, ck, cv, selmap, eall)


def _pad_keys(x):
    return jnp.concatenate([x, jnp.zeros((LANES - Q_PAD, x.shape[1]), F32)], axis=0)


def _value_variants(v, axis):
    first = lax.broadcasted_iota(jnp.int32, v.shape, axis) < HEAD_DIM
    one = jnp.ones_like(v)
    return jnp.where(first, v, one).astype(BF16), jnp.where(first, one, v).astype(BF16)


def _stack_sb_queries(qs):
    lane8 = lax.broadcasted_iota(jnp.int32, (Q_PAD, LANES), 1)
    low8 = lane8 < HEAD_DIM
    parts = []
    for k in range(SB_KV_HEADS):
        for r in range(SB_HEADS // SB_KV_HEADS):
            ch = 2 * (k // 2) + r
            keep = low8 if k % 2 == 0 else jnp.logical_not(low8)
            piece = jnp.where(keep, qs[:, ch * LANES:(ch + 1) * LANES], 0.0)
            zero = jnp.zeros_like(piece)
            parts.append(jnp.concatenate([piece, zero] if k // 2 == 0 else [zero, piece], axis=1))
    return jnp.concatenate(parts, axis=0).astype(BF16)


def _sb_pages_sweep(qb, pages, carry, accb, u):
    zs, sps, sums = [], [], []
    for pg in pages:
        z = _dot(qb, pg[0, 0:2 * LANES, :].astype(BF16))
        sp = _softplus(z)
        zs.append(z)
        sps.append(sp)
        sums.append(jnp.sum(-sp, axis=1, keepdims=True))
    for pi in range(len(pages) - 1, -1, -1):
        a = jnp.exp((zs[pi] - sps[pi]) + _dot_split(-sps[pi], u) + carry)
        accb = accb + _dot_nt(a.astype(BF16), pages[pi][0, 2 * LANES:4 * LANES, :].astype(BF16))
        carry = carry + sums[pi]
    return carry, accb


def _attn_sample_kernel(*refs, past_len, dec_seq, n_slots, n_near, n_list):
    table_ref, plist_ref, pcount_ref = refs[:3]
    refs = refs[3:]
    (qa_ref, qs_ref, gate_ref, ocmp_ref, slcn_ref, winn_ref, sbn_ref, state_ref, u_ref) = refs[:9]
    mask_refs = refs[9:9 + n_slots]
    slc_pages = refs[9 + n_slots:9 + 2 * n_slots]
    sb_pages = refs[9 + 2 * n_slots:9 + 2 * n_slots + n_near]
    oa_ref, accb_ref, carry_ref = refs[9 + 2 * n_slots + n_near:12 + 2 * n_slots + n_near]
    qa_s, m_s, acc_s, win_s = refs[12 + 2 * n_slots + n_near:]

    b = pl.program_id(0)
    j = pl.program_id(1)
    n_steps = pl.num_programs(1)
    n_rows = SAMPLE_ROWS
    half = n_rows // 2
    lane = lax.broadcasted_iota(jnp.int32, (n_rows, LANES), 1)
    low = lane < HEAD_DIM
    t_q = _row_ids((n_rows, LANES))[1]
    qpos = past_len + t_q
    slope = _head_slopes((n_rows, LANES))

    @pl.when(j == 0)
    def _():
        qa = _stack_nsa_queries(qa_ref[0])
        qa_s[...] = qa
        qb = _stack_sb_queries(qs_ref[0])
        u = u_ref[...]

        new_valid = lane < dec_seq
        slcn = _pad_keys(slcn_ref[0])
        smask = jnp.logical_and(new_valid, lane <= t_q)
        s = _dot_nt(qa, slcn[:, 0:LANES].astype(BF16)) - slope * (t_q - lane).astype(F32)
        s = jnp.where(smask, s, -BIG)
        m0 = jnp.max(s, axis=1, keepdims=True)
        p = jnp.where(smask, jnp.exp(s - m0), 0.0).astype(BF16)
        v0, v1 = _value_variants(slcn[:, LANES:2 * LANES], 1)
        m_s[...] = jnp.broadcast_to(m0, (n_rows, LANES))
        acc_s[0:half, :] = _dot(p[0:half], v0)
        acc_s[half:n_rows, :] = _dot(p[half:n_rows], v1)

        sbn = _pad_keys(sbn_ref[0])
        z = _dot_nt(qb, sbn[:, 0:2 * LANES].astype(BF16))
        a, tile_sum = _sb_weights(z, jnp.logical_and(new_valid, lane < t_q), 0.0, u)
        accb = _dot(a.astype(BF16), sbn[:, 2 * LANES:4 * LANES].astype(BF16))
        carry, accb = _sb_pages_sweep(qb, sb_pages, jnp.broadcast_to(tile_sum, (n_rows, LANES)), accb, u)
        accb_ref[0] = accb
        carry_ref[0] = carry

        n_state = state_ref.shape[2]
        winn = _pad_keys(winn_ref[0])
        n_k = n_state + LANES
        col = lax.broadcasted_iota(jnp.int32, (n_rows, n_k), 1)
        kpos = jnp.where(col < n_state, past_len - n_state + col, past_len + col - n_state)
        rel = past_len + _row_ids((n_rows, n_k))[1] - kpos
        wmask = jnp.logical_and(jnp.logical_and(rel >= 0, rel < WINDOW), col < n_state + dec_seq)
        qk = jnp.concatenate([_dot(qa, state_ref[0, 0:LANES, :].astype(BF16)),
                              _dot_nt(qa, winn[:, 0:LANES].astype(BF16))], axis=1)
        s = qk - _head_slopes((n_rows, n_k)) * rel.astype(F32)
        s = jnp.where(wmask, s, -BIG)
        m = jnp.max(s, axis=1, keepdims=True)
        e = jnp.where(wmask, jnp.exp(s - m), 0.0)
        p = (e / jnp.maximum(jnp.sum(e, axis=1, keepdims=True), 1e-30)).astype(BF16)
        win_s[...] = (_dot_nt(p[:, 0:n_state], state_ref[0, LANES:2 * LANES, :].astype(BF16))
                      + _dot(p[:, n_state:n_k], winn[:, LANES:2 * LANES].astype(BF16)))

    n_used = pcount_ref[b]

    @pl.when(j * n_slots < n_used)
    def _():
        qa = qa_s[...]
        ss = []
        for pi in range(n_slots):
            slot = j * n_slots + pi
            msk = mask_refs[pi][0]
            bias16 = jnp.where(slot < n_used, (msk - 1.0) * BIG, -BIG)
            bias = jnp.concatenate(
                [bias16[0:Q_PAD]] * HEADS_PER_GROUP + [bias16[Q_PAD:2 * Q_PAD]] * HEADS_PER_GROUP, axis=0)
            kpos = plist_ref[b * n_list + slot] * LANES + lane
            qk = _dot(qa, slc_pages[pi][0, 0:LANES, :].astype(BF16))
            ss.append((qk - slope * (qpos - kpos).astype(F32)) + bias)
        m_old = m_s[...]
        m_new = m_old
        for s in ss:
            m_new = jnp.maximum(m_new, jnp.max(s, axis=1, keepdims=True))
        pv0 = jnp.zeros((half, LANES), F32)
        pv1 = jnp.zeros((half, LANES), F32)
        for pi in range(n_slots):
            p = jnp.exp(ss[pi] - m_new).astype(BF16)
            v0, v1 = _value_variants(slc_pages[pi][0, LANES:2 * LANES, :], 0)
            pv0 = pv0 + _dot_nt(p[0:half], v0)
            pv1 = pv1 + _dot_nt(p[half:n_rows], v1)
        alpha = jnp.exp(m_old - m_new)
        acc_s[0:half, :] = alpha[0:half] * acc_s[0:half, :] + pv0
        acc_s[half:n_rows, :] = alpha[half:n_rows] * acc_s[half:n_rows, :] + pv1
        m_s[...] = m_new

    @pl.when(j == n_steps - 1)
    def _():
        o_slc = _normalise(acc_s[...])
        o_win = win_s[...]
        o_cmp = ocmp_ref[0]
        for c in range(HEADS_PER_GROUP):
            tot = []
            for g in range(NSA_KV_GROUPS):
                h = g * HEADS_PER_GROUP + c
                r0 = h * Q_PAD
                gc = gate_ref[0, :, h:h + 1]
                gs = gate_ref[0, :, NSA_HEADS + h:NSA_HEADS + h + 1]
                gw = gate_ref[0, :, 2 * NSA_HEADS + h:2 * NSA_HEADS + h + 1]
                tot.append(gc * o_cmp[r0:r0 + Q_PAD] + gs * o_slc[r0:r0 + Q_PAD] + gw * o_win[r0:r0 + Q_PAD])
            oa_ref[0, :, c * LANES:(c + 1) * LANES] = jnp.where(low[0:Q_PAD], tot[0], tot[1])


def _sb_rest_kernel(*refs, n_chunks, chunk):
    need_ref = refs[1]
    refs = refs[2:]
    qs_ref, accb_ref, carry_ref, u_ref = refs[:4]
    pages = refs[4:4 + n_chunks * chunk]
    ob_ref = refs[4 + n_chunks * chunk]
    accb_s, carry_s = refs[5 + n_chunks * chunk:]
    accb_s[...] = accb_ref[0]
    carry_s[...] = carry_ref[0]

    @pl.when(need_ref[pl.program_id(0)] > 0)
    def _():
        qb = _stack_sb_queries(qs_ref[0])
        u = u_ref[...]
        for c in range(n_chunks - 1, -1, -1):
            @pl.when(jnp.max(carry_s[...]) > EXP_UNDERFLOW)
            def _():
                carry, accb = _sb_pages_sweep(qb, pages[c * chunk:(c + 1) * chunk], carry_s[...], accb_s[...], u)
                carry_s[...] = carry
                accb_s[...] = accb

    low = lax.broadcasted_iota(jnp.int32, (Q_PAD, LANES), 1) < HEAD_DIM
    acc_b = accb_s[...]
    for kp in range(SB_KV_HEADS // 2):
        for r in range(SB_HEADS // SB_KV_HEADS):
            ra = ((2 * kp) * 2 + r) * Q_PAD
            rb = ((2 * kp + 1) * 2 + r) * Q_PAD
            a0 = acc_b[ra:ra + Q_PAD, kp * LANES:(kp + 1) * LANES]
            a1 = acc_b[rb:rb + Q_PAD, kp * LANES:(kp + 1) * LANES]
            ch = 2 * kp + r
            ob_ref[0, :, ch * LANES:(ch + 1) * LANES] = jnp.where(low, a0, a1)


def _page_schedule(selblk, n_pages, n_list, n_slots):
    n_b = selblk.shape[0]
    per_page = LANES // SEL_BLOCK
    used = jnp.max(selblk[:, :, :n_pages * per_page].reshape(n_b, -1, n_pages, per_page), axis=(1, 3)) > 0
    page_id = jnp.arange(n_pages, dtype=jnp.int32)
    order = -jnp.sort(-jnp.where(used, page_id[None, :], -1), axis=-1)[:, :n_list]
    order = jnp.pad(order, ((0, 0), (0, n_list - order.shape[1])), constant_values=-1)
    steps = [order[:, 0:n_slots]]
    steps[0] = jnp.where(steps[0] < 0, order[:, 0:1], steps[0])
    for j in range(1, n_list // n_slots):
        cur = order[:, j * n_slots:(j + 1) * n_slots]
        steps.append(jnp.where(cur < 0, steps[j - 1], cur))
    plist = jnp.maximum(jnp.concatenate(steps, axis=1), 0)
    return plist.reshape(-1).astype(jnp.int32), jnp.minimum(jnp.sum(used, axis=-1), n_list).astype(jnp.int32)


def _attn_sample(table, plist, pcount, qa8, qs8, gate8, ocmp, mexp, slcn, winn, sbn, state, u,
                 slc_pool, sb_pool, n_pages, n_list, n_slots, n_near, past_len, dec_seq):
    n_b = qa8.shape[0]
    n_rows = SAMPLE_ROWS
    n_state = state.shape[2]
    kern = functools.partial(_attn_sample_kernel, past_len=past_len, dec_seq=dec_seq,
                             n_slots=n_slots, n_near=n_near, n_list=n_list)
    per_b = lambda b, j, *_: (b, 0, 0)

    def mask_spec(p):
        return pl.BlockSpec((1, 2 * Q_PAD, LANES), lambda b, j, t, pls, pc: (b, 0, pls[b * n_list + j * n_slots + p]))

    def slc_spec(p):
        return pl.BlockSpec((1, 2 * LANES, LANES),
                            lambda b, j, t, pls, pc: (t[b * n_pages + pls[b * n_list + j * n_slots + p]], 0, 0))

    def near_spec(p):
        return pl.BlockSpec((1, 4 * LANES, LANES), lambda b, j, t, pls, pc: (t[b * n_pages + n_pages - n_near + p], 0, 0))

    grid_spec = pltpu.PrefetchScalarGridSpec(
        num_scalar_prefetch=3,
        grid=(n_b, n_list // n_slots),
        in_specs=[pl.BlockSpec((1, Q_PAD, 4 * LANES), per_b),
                  pl.BlockSpec((1, Q_PAD, 4 * LANES), per_b),
                  pl.BlockSpec((1, Q_PAD, LANES), per_b),
                  pl.BlockSpec((1, n_rows, LANES), per_b),
                  pl.BlockSpec((1, Q_PAD, 2 * LANES), per_b),
                  pl.BlockSpec((1, Q_PAD, 2 * LANES), per_b),
                  pl.BlockSpec((1, Q_PAD, 4 * LANES), per_b),
                  pl.BlockSpec((1, 2 * LANES, n_state), per_b),
                  pl.BlockSpec((LANES, LANES), lambda b, j, *_: (0, 0))]
        + [mask_spec(p) for p in range(n_slots)]
        + [slc_spec(p) for p in range(n_slots)]
        + [near_spec(p) for p in range(n_near)],
        out_specs=[pl.BlockSpec((1, Q_PAD, 4 * LANES), per_b),
                   pl.BlockSpec((1, n_rows, 2 * LANES), per_b),
                   pl.BlockSpec((1, n_rows, LANES), per_b)],
        scratch_shapes=[pltpu.VMEM((n_rows, LANES), BF16),
                        pltpu.VMEM((n_rows, LANES), F32),
                        pltpu.VMEM((n_rows, LANES), F32),
                        pltpu.VMEM((n_rows, LANES), F32)],
    )
    return pl.pallas_call(
        kern,
        grid_spec=grid_spec,
        out_shape=[jax.ShapeDtypeStruct((n_b, Q_PAD, 4 * LANES), F32),
                   jax.ShapeDtypeStruct((n_b, n_rows, 2 * LANES), F32),
                   jax.ShapeDtypeStruct((n_b, n_rows, LANES), F32)],
        compiler_params=pltpu.CompilerParams(dimension_semantics=("arbitrary", "arbitrary"),
                                             vmem_limit_bytes=VMEM_LIMIT),
        name="attn_sample",
    )(table, plist, pcount, qa8, qs8, gate8, ocmp, slcn, winn, sbn, state, u,
      *([mexp] * n_slots), *([slc_pool] * n_slots), *([sb_pool] * n_near))


def _sb_rest(table, need, qs8, accb, carry, u, sb_pool, n_pages, n_rest, chunk):
    n_b = qs8.shape[0]
    n_rows = SAMPLE_ROWS
    kern = functools.partial(_sb_rest_kernel, n_chunks=n_rest // chunk, chunk=chunk)
    per_b = lambda b, *_: (b, 0, 0)

    def page_spec(p):
        return pl.BlockSpec((1, 4 * LANES, LANES),
                            lambda b, t, nd: (t[jnp.where(nd[b] > 0, b * n_pages + p, 0)], 0, 0))

    grid_spec = pltpu.PrefetchScalarGridSpec(
        num_scalar_prefetch=2,
        grid=(n_b,),
        in_specs=[pl.BlockSpec((1, Q_PAD, 4 * LANES), per_b),
                  pl.BlockSpec((1, n_rows, 2 * LANES), per_b),
                  pl.BlockSpec((1, n_rows, LANES), per_b),
                  pl.BlockSpec((LANES, LANES), lambda b, *_: (0, 0))]
        + [page_spec(p) for p in range(n_rest)],
        out_specs=pl.BlockSpec((1, Q_PAD, 4 * LANES), per_b),
        scratch_shapes=[pltpu.VMEM((n_rows, 2 * LANES), F32),
                        pltpu.VMEM((n_rows, LANES), F32)],
    )
    return pl.pallas_call(
        kern,
        grid_spec=grid_spec,
        out_shape=jax.ShapeDtypeStruct((n_b, Q_PAD, 4 * LANES), F32),
        compiler_params=pltpu.CompilerParams(dimension_semantics=("arbitrary",),
                                             vmem_limit_bytes=VMEM_LIMIT),
        name="sb_rest",
    )(table, need, qs8, accb, carry, u, *([sb_pool] * n_rest))


def _head_index(base, perm):
    return np.concatenate([np.arange(base + h * HEAD_DIM, base + (h + 1) * HEAD_DIM) for h in perm])


def _largest_divisor(n, cap):
    d = min(n, cap)
    while n % d:
        d -= 1
    return d


def kernel(x_prompt, x_sample, cache_cmp_kv, cache_slc_kv, cache_sb_kv, state_win_kv, page_table,
           norm_g, w_in, q_norm_g, k_norm_g, cmp_pe, cmp_w1, cmp_b1, cmp_w2, w_out):
    n_b, t, d = x_prompt.shape
    n_db, dec_seq, _ = x_sample.shape
    depth = w_in.shape[0]
    assert depth == 1, "single-layer step"
    n_pages = page_table.shape[1]
    page = cache_cmp_kv.shape[2]
    past_len = n_pages * page
    n_state = state_win_kv.shape[2]
    assert page == LANES and t % LANES == 0 and t // SEL_BLOCK <= N_SEL_LANES
    assert past_len // SEL_BLOCK <= N_SEL_LANES and dec_seq <= Q_PAD and dec_seq <= SEL_BLOCK
    assert n_state == WINDOW and past_len >= WINDOW and t >= WINDOW

    nsa_w = NSA_HEADS * HEAD_DIM
    kvw = NSA_KV_GROUPS * HEAD_DIM
    sb_w = SB_HEADS * HEAD_DIM
    sbkv_w = SB_KV_HEADS * HEAD_DIM
    o_kv = nsa_w
    o_gt = o_kv + 6 * kvw
    o_za = o_gt + N_BRANCH * NSA_HEADS
    o_qs = o_za + nsa_w
    o_kb = o_qs + sb_w
    o_zb = o_kb + 2 * sbkv_w
    w0 = w_in[0]
    cols = np.concatenate([
        _head_index(0, PERM_A), np.arange(o_kv, o_gt), _head_index(o_za, PERM_A),
        _head_index(o_qs, PERM_B), np.arange(o_kb, o_zb), _head_index(o_zb, PERM_B),
        np.arange(o_gt, o_za)])
    w_perm = jnp.pad(w0[:, cols], ((0, 0), (0, LANES - N_BRANCH * NSA_HEADS))).astype(BF16)
    rows_out = np.concatenate([_head_index(0, PERM_A), _head_index(nsa_w, PERM_B)])
    w_out_perm = w_out[0][rows_out, :].astype(BF16)

    seg = jnp.asarray(np.kron(np.eye(2), np.ones((HEAD_DIM, HEAD_DIM))), BF16)
    tile2 = lambda v: jnp.tile(v, 2).reshape(1, LANES)
    gq = tile2(q_norm_g[0])
    gkc, gks, gkw = tile2(k_norm_g[0, 0]), tile2(k_norm_g[0, 1]), tile2(k_norm_g[0, 2])
    g_in = norm_g[0].reshape(1, d)

    w1 = cmp_w1[0].reshape(2, 2, CMP_STRIDE, HEAD_DIM, CMP_HIDDEN)
    w1 = jnp.transpose(w1, (0, 2, 3, 1, 4))
    zeros = jnp.zeros_like(w1)
    wc = jnp.stack([jnp.concatenate([w1, zeros], axis=3), jnp.concatenate([zeros, w1], axis=3)], axis=2)
    wc = wc.reshape(2, CMP_STRIDE // 2, 2 * LANES, 4 * CMP_HIDDEN).astype(BF16)
    w2 = cmp_w2[0]
    z2 = jnp.zeros_like(w2)
    w2p = jnp.stack([jnp.concatenate([w2, z2], axis=2), jnp.concatenate([z2, w2], axis=2)], axis=1).astype(BF16)
    cb = _cbias(cmp_pe[0].reshape(2, 1, CMP_LEN * HEAD_DIM), cmp_w1[0],
                cmp_b1[0].reshape(2, 1, CMP_HIDDEN))

    def selection_map(n_rows):
        r = np.arange(n_rows)[:, None]
        jb = np.arange(N_SEL_LANES)[None, :]
        start = (r - 1) * CMP_STRIDE
        ok = (r >= 1) & (start < jb * SEL_BLOCK + SEL_BLOCK) & (start + CMP_LEN > jb * SEL_BLOCK)
        return jnp.asarray(ok, BF16)

    u = jnp.asarray(np.arange(LANES)[:, None] > np.arange(LANES)[None, :], BF16)

    tm = _largest_divisor(n_b * t, 256)
    xp2 = x_prompt.reshape(n_b * t, d)
    (q16, cmpkv, slckv, winkv, slck, slcv, wink, winv, gates, sza, qs16, sbkv, sbk, sbv, szb) = _proj(
        xp2, g_in, w_perm, gq, gks, gkw, seg, tm)
    n_sub = t // CMP_STRIDE
    sub_w = CMP_STRIDE * 2 * kvw
    ck, cv = _compress_prompt(cmpkv.reshape(n_b * n_sub, sub_w), wc, cb, w2p, gkc, seg,
                              n_b, _largest_divisor(n_sub, 64))
    r3 = lambda a: a.reshape(n_b, t, a.shape[-1])
    oc, sel = _cmp_prompt(r3(q16), r3(gates), ck.reshape(n_b, n_sub, LANES), cv.reshape(n_b, n_sub, 2 * LANES),
                          selection_map(n_sub), _largest_divisor(t, CMP_Q_ROWS))
    oa = _slcwin_prompt(r3(q16), r3(gates), oc, sel, r3(slck), r3(slcv), r3(wink), r3(winv))
    ob = _sb_prompt(r3(qs16), r3(sbk), r3(sbv), u)
    y_prompt = _final(xp2, oa.reshape(n_b * t, nsa_w), sza, ob.reshape(n_b * t, sb_w), szb,
                      w_out_perm, tm).reshape(n_b, t, d)
    kv5 = lambda a, n, heads: a.reshape(1, n, -1, 2, heads, HEAD_DIM)
    cmp_p = kv5(cmpkv, n_b, NSA_KV_GROUPS)
    slc_p = kv5(slckv, n_b, NSA_KV_GROUPS)
    sb_p = kv5(sbkv, n_b, SB_KV_HEADS)
    win_p = kv5(winkv, n_b, NSA_KV_GROUPS)[:, :, t - min(WINDOW, t):]

    rows_s = n_db * dec_seq
    tms = _largest_divisor(rows_s, 256)
    xs2 = x_sample.reshape(rows_s, d)
    (q16s, cmpkv_s, slckv_s, winkv_s, _, _, _, _, gates_s, sza_s, qs16s, sbkv_s, _, _, szb_s) = _proj(
        xs2, g_in, w_perm, gq, gks, gkw, seg, tms)
    table = page_table.reshape(-1).astype(jnp.int32)
    n_sub_s = n_pages * (page // CMP_STRIDE)
    chan_major = lambda a: jnp.transpose(a, (0, 2, 3, 4, 1)).reshape(a.shape[0], -1, a.shape[1])
    cks, cvs = _compress_sample(table, chan_major(cache_cmp_kv[0]), wc, cb, w2p, gkc, seg,
                                n_db, n_pages, _largest_divisor(n_pages, COMPRESS_PAGES))
    pad8 = lambda a: jnp.pad(a.reshape(n_db, dec_seq, a.shape[-1]).astype(F32),
                             ((0, 0), (0, Q_PAD - dec_seq), (0, 0)))
    qa8, qs8, gate8 = pad8(q16s), pad8(qs16s), pad8(gates_s)
    n_blocks = -(-(past_len + dec_seq) // SEL_BLOCK)
    n_pick = min(SEL_TOPK, n_blocks) - 1
    eall = jnp.asarray(np.arange(N_SEL_LANES)[:, None] == (np.arange(past_len)[None, :] // SEL_BLOCK), BF16)
    ocmp, mexp, selblk = _cmp_sample(qa8, cks.reshape(n_db, n_sub_s, LANES), cvs.reshape(n_db, n_sub_s, 2 * LANES),
                                     selection_map(n_sub_s), eall, past_len, n_pick,
                                     _largest_divisor(n_db, CMP_SAMPLE_SEQS))
    n_slots = min(SLC_SLOTS, n_pages)
    n_list = -(-min(n_pages, NSA_KV_GROUPS * n_pick) // n_slots) * n_slots
    plist, pcount = _page_schedule(selblk, n_pages, n_list, n_slots)
    n_near = min(ATTN_PAGES, n_pages)
    sb_pool = chan_major(cache_sb_kv[0])
    oa8, accb, carry = _attn_sample(table, plist, pcount, qa8, qs8, gate8, ocmp, mexp,
                                    pad8(slckv_s), pad8(winkv_s), pad8(sbkv_s),
                                    chan_major(state_win_kv[0]), u, chan_major(cache_slc_kv[0]), sb_pool,
                                    n_pages, n_list, n_slots, n_near, past_len, dec_seq)
    need = (jnp.max(carry, axis=(1, 2)) > EXP_UNDERFLOW).astype(jnp.int32)
    n_rest = n_pages - n_near
    ob8 = _sb_rest(table, need, qs8, accb, carry, u, sb_pool, n_pages, n_rest,
                   _largest_divisor(n_rest, ATTN_PAGES) if n_rest else 1)
    y_sample = _final(xs2, oa8[:, :dec_seq].reshape(rows_s, nsa_w), sza_s,
                      ob8[:, :dec_seq].reshape(rows_s, sb_w), szb_s, w_out_perm, tms).reshape(n_db, dec_seq, d)
    kv5s = lambda a, heads: a.reshape(1, n_db, dec_seq, 2, heads, HEAD_DIM)
    cmp_s = kv5s(cmpkv_s, NSA_KV_GROUPS)
    slc_s = kv5s(slckv_s, NSA_KV_GROUPS)
    sb_s = kv5s(sbkv_s, SB_KV_HEADS)
    win_new = kv5s(winkv_s, NSA_KV_GROUPS)
    win_s = jnp.concatenate([state_win_kv[:, :, dec_seq:], win_new], axis=2)

    return (y_prompt, y_sample, cmp_p, cmp_s, slc_p, slc_s, sb_p, sb_s, win_p, win_s)
```

```python
import functools

import numpy as np
import jax
import jax.numpy as jnp
from jax import lax
from jax.experimental import pallas as pl
from jax.experimental.pallas import tpu as pltpu

F32 = jnp.float32
BF16 = jnp.bfloat16

HEAD_DIM = 64
NSA_HEADS = 8
SB_HEADS = 8
NSA_KV_GROUPS = 2
SB_KV_HEADS = 4
HEADS_PER_GROUP = NSA_HEADS // NSA_KV_GROUPS
N_BRANCH = 3
CMP_LEN = 32
CMP_STRIDE = 16
CMP_HIDDEN = 128
SEL_BLOCK = 64
SEL_TOPK = 16
WINDOW = 512
RMS_EPS = 1e-6
BIG = 1e30
PICKED = -3e38
EXP_UNDERFLOW = -104.0
SCALE = HEAD_DIM ** -0.5
SEL_SHIFT = 6
assert 1 << SEL_SHIFT == SEL_BLOCK

LANES = 128
SUBLANES = 8
N_SEL_LANES = 128
VMEM_LIMIT = 52 * 1024 * 1024
CMP_Q_ROWS = 512
CMP_SAMPLE_SEQS = 8
COMPRESS_PAGES = 64
ATTN_PAGES = 8
SLC_SLOTS = 16

PERM_A = (0, 4, 1, 5, 2, 6, 3, 7)
PERM_B = (0, 2, 1, 3, 4, 6, 5, 7)


def _slope(h):
    return float(2.0 ** (-(h + 1)))


def _dot(a, b):
    return jnp.dot(a, b, preferred_element_type=F32)


def _dot_nt(a, b):
    return lax.dot_general(a, b, (((1,), (1,)), ((), ())), preferred_element_type=F32)


def _dot_split(x, m):
    hi = x.astype(BF16)
    lo = (x - hi.astype(F32)).astype(BF16)
    return _dot(hi, m) + _dot(lo, m)


def _softplus(z):
    return jnp.maximum(z, 0.0) + jnp.log(1.0 + jnp.exp(-jnp.abs(z)))


def _silu(z):
    return z * jax.nn.sigmoid(z)


def _proj_kernel(x_ref, g_ref, w_ref, gq_ref, gks_ref, gkw_ref, seg_ref,
                 q_ref, cmp_ref, slc_ref, win_ref, slck_ref, slcv_ref, wink_ref, winv_ref,
                 gate_ref, sza_ref, qs_ref, sb_ref, sbk_ref, sbv_ref, szb_ref):
    x = x_ref[...]
    ms = jnp.mean(x * x, axis=-1, keepdims=True)
    hn = ((x * lax.rsqrt(ms + RMS_EPS)) * g_ref[...]).astype(BF16)
    seg = seg_ref[...]
    lane = lax.broadcasted_iota(jnp.int32, (x.shape[0], LANES), 1)
    low = lane < HEAD_DIM

    def sec(a, n=LANES):
        return _dot(hn, w_ref[:, a:a + n])

    def headnorm(t, gain):
        ss = _dot_split(t * t, seg) * (1.0 / HEAD_DIM)
        return (t * lax.rsqrt(ss + RMS_EPS)) * gain

    def variants(v):
        one = jnp.ones_like(v)
        return jnp.where(low, v, one).astype(BF16), jnp.where(low, one, v).astype(BF16)

    for c in range(4):
        t = headnorm(sec(c * LANES), gq_ref[...])
        q_ref[:, c * LANES:(c + 1) * LANES] = (t * SCALE).astype(BF16)
    base = 4 * LANES
    cmp_ref[:, 0:LANES] = sec(base)
    cmp_ref[:, LANES:2 * LANES] = sec(base + LANES)
    for kv_out, kk_out, vv_out, gain_ref, off in (
            (slc_ref, slck_ref, slcv_ref, gks_ref, base + 2 * LANES),
            (win_ref, wink_ref, winv_ref, gkw_ref, base + 4 * LANES)):
        kn = headnorm(sec(off), gain_ref[...])
        v = sec(off + LANES)
        kv_out[:, 0:LANES] = kn
        kv_out[:, LANES:2 * LANES] = v
        kk_out[...] = kn.astype(BF16)
        v0, v1 = variants(v)
        vv_out[:, 0:LANES] = v0
        vv_out[:, LANES:2 * LANES] = v1
    base = 10 * LANES
    for c in range(4):
        sza_ref[:, c * LANES:(c + 1) * LANES] = _silu(sec(base + c * LANES))
    base = 14 * LANES
    for c in range(4):
        qs_ref[:, c * LANES:(c + 1) * LANES] = (sec(base + c * LANES) * SCALE).astype(BF16)
    base = 18 * LANES
    for c in range(2):
        kb = sec(base + c * LANES)
        vb = sec(base + (2 + c) * LANES)
        sb_ref[:, c * LANES:(c + 1) * LANES] = kb
        sb_ref[:, (2 + c) * LANES:(3 + c) * LANES] = vb
        sbk_ref[:, c * LANES:(c + 1) * LANES] = kb.astype(BF16)
        sbv_ref[:, c * LANES:(c + 1) * LANES] = vb.astype(BF16)
    base = 22 * LANES
    for c in range(4):
        szb_ref[:, c * LANES:(c + 1) * LANES] = _silu(sec(base + c * LANES))
    gate_ref[...] = jax.nn.sigmoid(sec(26 * LANES))


def _proj(x2, g, w_perm, gq, gks, gkw, seg, tm):
    rows, d = x2.shape
    wcols = w_perm.shape[1]
    row = lambda i: (i, 0)
    const = lambda i: (0, 0)
    widths = [(512, BF16), (256, F32), (256, F32), (256, F32), (128, BF16), (256, BF16),
              (128, BF16), (256, BF16), (128, F32), (512, F32), (512, BF16), (512, F32),
              (256, BF16), (256, BF16), (512, F32)]
    return pl.pallas_call(
        _proj_kernel,
        grid=(rows // tm,),
        in_specs=[pl.BlockSpec((tm, d), row), pl.BlockSpec((1, d), const),
                  pl.BlockSpec((d, wcols), const), pl.BlockSpec((1, LANES), const),
                  pl.BlockSpec((1, LANES), const), pl.BlockSpec((1, LANES), const),
                  pl.BlockSpec((LANES, LANES), const)],
        out_specs=[pl.BlockSpec((tm, w), row) for w, _ in widths],
        out_shape=[jax.ShapeDtypeStruct((rows, w), dt) for w, dt in widths],
        compiler_params=pltpu.CompilerParams(dimension_semantics=("arbitrary",),
                                             vmem_limit_bytes=VMEM_LIMIT),
        name="proj",
    )(x2, g, w_perm, gq, gks, gkw, seg)


def _cbias_kernel(pe_ref, w1_ref, b1_ref, o_ref):
    for kv in range(2):
        pe = jnp.broadcast_to(pe_ref[kv], (SUBLANES, pe_ref.shape[-1])).astype(BF16)
        o_ref[kv] = _dot(pe, w1_ref[kv].astype(BF16)) + b1_ref[kv]


def _cbias(pe_flat, w1, b1):
    return pl.pallas_call(
        _cbias_kernel,
        out_shape=jax.ShapeDtypeStruct((2, SUBLANES, CMP_HIDDEN), F32),
        name="cbias",
    )(pe_flat, w1, b1)


def _compress_kernel(*refs, n_in, has_table):
    if has_table:
        refs = refs[1:]
    x_refs = refs[:n_in]
    if has_table:
        perm_ref = refs[n_in]
        refs = refs[1:]
    wc_ref, cb_ref, w2_ref, gk_ref, seg_ref = refs[n_in:n_in + 5]
    ck_ref, cv_ref = refs[n_in + 5:n_in + 7]
    carry_ref = refs[n_in + 7]
    n_rows = ck_ref.shape[0]

    @pl.when(pl.program_id(1) == 0)
    def _():
        carry_ref[...] = jnp.zeros(carry_ref.shape, F32)

    row = lax.broadcasted_iota(jnp.int32, (n_rows, CMP_HIDDEN), 0)
    lane = lax.broadcasted_iota(jnp.int32, (n_rows, LANES), 1)
    low = lane < HEAD_DIM
    for kv in range(2):
        if has_table:
            perm = perm_ref[...]
            pages = [_dot_nt(perm, r[0, kv * LANES:(kv + 1) * LANES, :].astype(BF16)) for r in x_refs]
        def sub_rows(s):
            if has_table:
                nb = LANES // CMP_STRIDE
                return jnp.concatenate([pg[s * nb:(s + 1) * nb] for pg in pages], axis=0)
            a = s * 2 * LANES + kv * LANES
            return x_refs[0][:, a:a + LANES]

        acc = jnp.zeros((n_rows, 4 * CMP_HIDDEN), F32)
        for s2 in range(CMP_STRIDE // 2):
            xk = jnp.concatenate([sub_rows(2 * s2), sub_rows(2 * s2 + 1)], axis=1)
            acc = acc + _dot(xk.astype(BF16), wc_ref[kv, s2])
        cb = cb_ref[kv, 0:1, :]
        out = jnp.zeros((n_rows, LANES), F32)
        for g in range(NSA_KV_GROUPS):
            a0 = acc[:, (2 * g) * CMP_HIDDEN:(2 * g + 1) * CMP_HIDDEN]
            a1 = acc[:, (2 * g + 1) * CMP_HIDDEN:(2 * g + 2) * CMP_HIDDEN]
            prev = carry_ref[kv, SUBLANES - 1:SUBLANES, (2 * g) * CMP_HIDDEN:(2 * g + 1) * CMP_HIDDEN]
            shifted = jnp.where(row == 0, prev, pltpu.roll(a0, 1, axis=0))
            h = shifted + a1 + cb
            out = out + _dot(_silu(h).astype(BF16), w2_ref[kv, g])
        carry_ref[kv] = acc[n_rows - SUBLANES:, :]
        if kv == 0:
            ss = _dot_split(out * out, seg_ref[...]) * (1.0 / HEAD_DIM)
            ck_ref[...] = ((out * lax.rsqrt(ss + RMS_EPS)) * gk_ref[...]).astype(BF16)
        else:
            one = jnp.ones_like(out)
            cv_ref[:, 0:LANES] = jnp.where(low, out, one).astype(BF16)
            cv_ref[:, LANES:2 * LANES] = jnp.where(low, one, out).astype(BF16)


def _compress_out_shape(n_out_rows):
    return [jax.ShapeDtypeStruct((n_out_rows, LANES), BF16),
            jax.ShapeDtypeStruct((n_out_rows, 2 * LANES), BF16)]


def _compress_weight_specs():
    return [pl.BlockSpec((2, CMP_STRIDE // 2, 2 * LANES, 4 * CMP_HIDDEN), lambda *a: (0, 0, 0, 0)),
            pl.BlockSpec((2, SUBLANES, CMP_HIDDEN), lambda *a: (0, 0, 0)),
            pl.BlockSpec((2, NSA_KV_GROUPS, CMP_HIDDEN, LANES), lambda *a: (0, 0, 0, 0)),
            pl.BlockSpec((1, LANES), lambda *a: (0, 0)),
            pl.BlockSpec((LANES, LANES), lambda *a: (0, 0))]


def _compress_prompt(xsub, wc, cb, w2p, gk, seg, n_batch, rows_per_step):
    total = xsub.shape[0]
    per_b = total // n_batch
    nj = per_b // rows_per_step
    kern = functools.partial(_compress_kernel, n_in=1, has_table=False)
    return pl.pallas_call(
        kern,
        grid=(n_batch, nj),
        in_specs=[pl.BlockSpec((rows_per_step, xsub.shape[1]), lambda b, j: (b * nj + j, 0))]
        + _compress_weight_specs(),
        out_specs=[pl.BlockSpec((rows_per_step, LANES), lambda b, j: (b * nj + j, 0)),
                   pl.BlockSpec((rows_per_step, 2 * LANES), lambda b, j: (b * nj + j, 0))],
        out_shape=_compress_out_shape(total),
        scratch_shapes=[pltpu.VMEM((2, SUBLANES, 4 * CMP_HIDDEN), F32)],
        compiler_params=pltpu.CompilerParams(dimension_semantics=("arbitrary", "arbitrary"),
                                             vmem_limit_bytes=VMEM_LIMIT),
        name="compress_prompt",
    )(xsub, wc, cb, w2p, gk, seg)


def _compress_sample(table, pool_t, wc, cb, w2p, gk, seg, n_batch, n_pages, pages_per_step):
    page = pool_t.shape[2]
    sub_per_page = page // CMP_STRIDE
    nj = n_pages // pages_per_step
    rows_per_step = pages_per_step * sub_per_page
    total = n_batch * n_pages * sub_per_page
    kern = functools.partial(_compress_kernel, n_in=pages_per_step, has_table=True)
    rho = np.arange(page)
    perm = jnp.asarray(np.arange(page)[None, :] == (CMP_STRIDE * (rho % sub_per_page) + rho // sub_per_page)[:, None],
                       BF16)

    def page_spec(p):
        return pl.BlockSpec((1, pool_t.shape[1], page),
                            lambda b, j, t: (t[b * n_pages + j * pages_per_step + p], 0, 0))

    grid_spec = pltpu.PrefetchScalarGridSpec(
        num_scalar_prefetch=1,
        grid=(n_batch, nj),
        in_specs=[page_spec(p) for p in range(pages_per_step)]
        + [pl.BlockSpec((page, page), lambda b, j, t: (0, 0))] + _compress_weight_specs(),
        out_specs=[pl.BlockSpec((rows_per_step, LANES), lambda b, j, t: (b * nj + j, 0)),
                   pl.BlockSpec((rows_per_step, 2 * LANES), lambda b, j, t: (b * nj + j, 0))],
        scratch_shapes=[pltpu.VMEM((2, SUBLANES, 4 * CMP_HIDDEN), F32)],
    )
    return pl.pallas_call(
        kern,
        grid_spec=grid_spec,
        out_shape=_compress_out_shape(total),
        compiler_params=pltpu.CompilerParams(dimension_semantics=("arbitrary", "arbitrary"),
                                             vmem_limit_bytes=VMEM_LIMIT),
        name="compress_sample",
    )(table, *([pool_t] * pages_per_step), perm, wc, cb, w2p, gk, seg)


def _pick_blocks(v_s, sel_s, lane_f, n_pick):
    sel_s[...] = jnp.zeros(sel_s.shape, F32)

    def body(_, c):
        v = v_s[...]
        mx = jnp.max(v, axis=1, keepdims=True)
        cand = jnp.where(v == mx, lane_f, float(2 * N_SEL_LANES))
        idx = jnp.min(cand, axis=1, keepdims=True)
        hit = lane_f == idx
        sel_s[...] = jnp.where(hit, 1.0, sel_s[...])
        v_s[...] = jnp.where(hit, PICKED, v)
        return c

    lax.fori_loop(0, n_pick, body, 0)


def _cmp_probs(qh, ck, slope, crel, cmask):
    s = _dot_nt(qh, ck) - slope * crel
    s = jnp.where(cmask, s, -BIG)
    m = jnp.max(s, axis=1, keepdims=True)
    e = jnp.where(cmask, jnp.exp(s - m), 0.0)
    return e / jnp.maximum(jnp.sum(e, axis=1, keepdims=True), 1e-30)


def _normalise(acc):
    return acc / jnp.maximum(pltpu.roll(acc, HEAD_DIM, axis=1), 1e-30)


def _sb_weights(z, causal, carry, u):
    sp = _softplus(z)
    l1m = -sp if causal is None else jnp.where(causal, -sp, 0.0)
    a = jnp.exp((z - sp) + _dot_split(l1m, u) + carry)
    if causal is not None:
        a = jnp.where(causal, a, 0.0)
    return a, jnp.sum(l1m, axis=1, keepdims=True)


def _tile_sweep(n_tiles, kb_of, tile, live=None, pair=None):
    tile(kb_of(0), True)
    n_rest = n_tiles - 1
    odd = n_rest & 1

    @pl.when(odd == 1)
    def _():
        tile(kb_of(1), False)

    start = 1 + odd
    n_trips = n_rest >> 1

    def run(t):
        if pair is None:
            tile(kb_of(start + 2 * t), False)
            tile(kb_of(start + 2 * t + 1), False)
        else:
            pair(kb_of(start + 2 * t), kb_of(start + 2 * t + 1))

    if live is None:
        def body(t, c):
            run(t)
            return c

        lax.fori_loop(0, n_trips, body, 0)
    else:
        def cond(state):
            t, go = state
            return jnp.logical_and(t < n_trips, go)

        def wbody(state):
            run(state[0])
            return state[0] + 1, live()

        lax.while_loop(cond, wbody, (jnp.int32(0), live()))


def _cmp_prompt_kernel(q_ref, gate_ref, ck_ref, cv_ref, selmap_ref, oc_ref, sel_ref, v_s, sel_s):
    nq = q_ref.shape[1]
    q0 = pl.program_id(1) * nq
    n_c = ck_ref.shape[1]
    lane = lax.broadcasted_iota(jnp.int32, (nq, LANES), 1)
    low = lane < HEAD_DIM
    cur = (q0 + lax.broadcasted_iota(jnp.int32, (nq, LANES), 0)) >> SEL_SHIFT
    forced = jnp.logical_or(lane == 0, jnp.logical_or(lane == cur, lane == cur - 1))
    allowed = lane <= cur

    def attend(width):
        crow = lax.broadcasted_iota(jnp.int32, (nq, width), 1)
        cend = crow * CMP_STRIDE + (CMP_STRIDE - 1)
        cq = q0 + lax.broadcasted_iota(jnp.int32, (nq, width), 0)
        cmask = jnp.logical_and(cend <= cq, crow >= 1)
        crel = (cq - cend).astype(F32)
        ck = ck_ref[0, 0:width, :]
        outs = [[None] * HEADS_PER_GROUP for _ in range(NSA_KV_GROUPS)]
        for g in range(NSA_KV_GROUPS):
            keep = low if g == 0 else jnp.logical_not(low)
            imp = jnp.zeros((nq, N_SEL_LANES), F32)
            for c in range(HEADS_PER_GROUP):
                h = g * HEADS_PER_GROUP + c
                qc = q_ref[0, :, c * LANES:(c + 1) * LANES]
                p = _cmp_probs(jnp.where(keep, qc, jnp.zeros_like(qc)), ck, _slope(h), crel, cmask)
                oc = _dot(p.astype(BF16), cv_ref[0, 0:width, g * LANES:(g + 1) * LANES])
                outs[g][c] = oc * gate_ref[0, :, h:h + 1]
                imp = imp + _dot_split(p, selmap_ref[0:width, :])
            v_s[g * nq:(g + 1) * nq, :] = jnp.where(forced, BIG, jnp.where(allowed, imp, -BIG))
        for c in range(HEADS_PER_GROUP):
            oc_ref[0, :, c * LANES:(c + 1) * LANES] = jnp.where(low, outs[0][c], outs[1][c])

    w_step = min(LANES, n_c)
    n_needed = (q0 + nq) // CMP_STRIDE
    n_cols = (n_needed + (w_step - 1)) // w_step
    for k in range(1, n_c // w_step + 1):
        pl.when(n_cols == k)(functools.partial(attend, k * w_step))

    lane_f = lax.broadcasted_iota(jnp.int32, v_s.shape, 1).astype(F32)
    _pick_blocks(v_s, sel_s, lane_f, SEL_TOPK)
    for g in range(NSA_KV_GROUPS):
        sel_ref[0, g] = jnp.where(allowed, sel_s[g * nq:(g + 1) * nq, :], 0.0).astype(BF16)


def _cmp_prompt(q, gates, ck, cv, selmap, nq):
    n_b, t, _ = q.shape
    n_c = ck.shape[1]
    per_q = lambda b, i: (b, i, 0)
    per_b = lambda b, i: (b, 0, 0)
    return pl.pallas_call(
        _cmp_prompt_kernel,
        grid=(n_b, t // nq),
        in_specs=[pl.BlockSpec((1, nq, 4 * LANES), per_q),
                  pl.BlockSpec((1, nq, LANES), per_q),
                  pl.BlockSpec((1, n_c, LANES), per_b),
                  pl.BlockSpec((1, n_c, 2 * LANES), per_b),
                  pl.BlockSpec((n_c, N_SEL_LANES), lambda b, i: (0, 0))],
        out_specs=[pl.BlockSpec((1, nq, 4 * LANES), per_q),
                   pl.BlockSpec((1, NSA_KV_GROUPS, nq, N_SEL_LANES), lambda b, i: (b, 0, i, 0))],
        out_shape=[jax.ShapeDtypeStruct((n_b, t, 4 * LANES), F32),
                   jax.ShapeDtypeStruct((n_b, NSA_KV_GROUPS, t, N_SEL_LANES), BF16)],
        scratch_shapes=[pltpu.VMEM((NSA_KV_GROUPS * nq, N_SEL_LANES), F32),
                        pltpu.VMEM((NSA_KV_GROUPS * nq, N_SEL_LANES), F32)],
        compiler_params=pltpu.CompilerParams(dimension_semantics=("arbitrary", "arbitrary"),
                                             vmem_limit_bytes=VMEM_LIMIT),
        name="cmp_prompt",
    )(q, gates, ck, cv, selmap)


def _slcwin_prompt_kernel(tiles_ref, count_ref, q_ref, gate_ref, oc_ref, sel_ref, sk_ref, sv_ref, wk_ref, wv_ref,
                          o_ref, qm_s, m_s, acc_s, tot_s):
    i = pl.program_id(1)
    step = pl.program_id(0) * pl.num_programs(1) + i
    tiles_per_step = sk_ref.shape[1] // LANES
    nq = LANES
    lane = lax.broadcasted_iota(jnp.int32, (nq, LANES), 1)
    row = lax.broadcasted_iota(jnp.int32, (nq, LANES), 0)
    low = lane < HEAD_DIM
    qpos = i * nq + row

    for g in range(NSA_KV_GROUPS):
        keep = low if g == 0 else jnp.logical_not(low)
        for c in range(HEADS_PER_GROUP):
            qc = q_ref[0, :, c * LANES:(c + 1) * LANES]
            r0 = (g * HEADS_PER_GROUP + c) * nq
            qm_s[r0:r0 + nq, :] = jnp.where(keep, qc, jnp.zeros_like(qc))

    def run_branch(k_ref, v_ref, banded, n_tiles, kb_of, gate_base, first_branch):
        m_s[...] = jnp.full(m_s.shape, -BIG, F32)
        acc_s[...] = jnp.zeros(acc_s.shape, F32)

        def tiles(kbs, diag):
            k0s = [pl.multiple_of(kb * LANES, LANES) for kb in kbs]
            rels = [qpos - (k0 + lane) for k0 in k0s]
            kt = jnp.concatenate([k_ref[0, pl.ds(k0, LANES), :] for k0 in k0s], axis=0)
            s = _dot_nt(qm_s[...], kt)
            relf = jnp.concatenate([rel.astype(F32) for rel in rels], axis=1)
            if banded:
                band = jnp.concatenate(
                    [jnp.where(jnp.logical_and(rel >= 0, rel < WINDOW), 0.0, -BIG) for rel in rels], axis=1)
            else:
                expands = [jnp.where(row == (LANES // SEL_BLOCK) * kb + (lane >> SEL_SHIFT), 1.0, 0.0).astype(BF16)
                           for kb in kbs]
            for g in range(NSA_KV_GROUPS):
                if banded:
                    bias = band
                else:
                    parts = [(_dot(sel_ref[0, g], e) - 1.0) * BIG for e in expands]
                    if diag:
                        parts = [jnp.where(rel >= 0, b, -BIG) for rel, b in zip(rels, parts)]
                    bias = jnp.concatenate(parts, axis=1)
                ps, alphas = [], []
                for c in range(HEADS_PER_GROUP):
                    h = g * HEADS_PER_GROUP + c
                    r0 = h * nq
                    sh = (s[r0:r0 + nq] - _slope(h) * relf) + bias
                    m_old = m_s[r0:r0 + nq, :]
                    m_new = jnp.maximum(m_old, jnp.max(sh, axis=1, keepdims=True))
                    ps.append(jnp.exp(sh - jnp.concatenate([m_new] * len(kbs), axis=1)).astype(BF16))
                    alphas.append(jnp.exp(m_old - m_new))
                    m_s[r0:r0 + nq, :] = m_new
                g0 = g * HEADS_PER_GROUP * nq
                g1 = (g + 1) * HEADS_PER_GROUP * nq
                vt = jnp.concatenate([v_ref[0, pl.ds(k0, LANES), g * LANES:(g + 1) * LANES] for k0 in k0s], axis=0)
                pv = _dot(jnp.concatenate(ps, axis=0), vt)
                acc_s[g0:g1, :] = jnp.concatenate(alphas, axis=0) * acc_s[g0:g1, :] + pv

        _tile_sweep(n_tiles, kb_of, lambda kb, diag: tiles([kb], diag),
                    pair=lambda ka, kb: tiles([ka, kb], False))
        for h in range(NSA_HEADS):
            r0 = h * nq
            o = _normalise(acc_s[r0:r0 + nq, :]) * gate_ref[0, :, gate_base + h:gate_base + h + 1]
            tot_s[r0:r0 + nq, :] = o if first_branch else tot_s[r0:r0 + nq, :] + o

    run_branch(sk_ref, sv_ref, False, count_ref[step], lambda t: tiles_ref[step * tiles_per_step + t],
               NSA_HEADS, True)
    run_branch(wk_ref, wv_ref, True, jnp.minimum(i, WINDOW // LANES) + 1, lambda t: i - t,
               2 * NSA_HEADS, False)

    for c in range(HEADS_PER_GROUP):
        a = tot_s[c * nq:(c + 1) * nq, :]
        b = tot_s[(HEADS_PER_GROUP + c) * nq:(HEADS_PER_GROUP + c + 1) * nq, :]
        o_ref[0, :, c * LANES:(c + 1) * LANES] = oc_ref[0, :, c * LANES:(c + 1) * LANES] + jnp.where(low, a, b)


def _tile_schedule(sel):
    n_b, _, t, n_blk = sel.shape
    nqb = t // LANES
    per_tile = LANES // SEL_BLOCK
    used = jnp.max(sel.reshape(n_b, NSA_KV_GROUPS, nqb, LANES, n_blk // per_tile, per_tile), axis=(1, 3, 5)) > 0
    kb = jnp.arange(n_blk // per_tile, dtype=jnp.int32)
    used = jnp.logical_and(used[:, :, :nqb], kb[None, None, :nqb] <= jnp.arange(nqb, dtype=jnp.int32)[None, :, None])
    order = -jnp.sort(-jnp.where(used, kb[None, None, :nqb], -1), axis=-1)
    return jnp.maximum(order, 0).reshape(-1).astype(jnp.int32), jnp.sum(used, axis=-1).reshape(-1).astype(jnp.int32)


def _slcwin_prompt(q, gates, oc, sel, sk, sv, wk, wv):
    n_b, t, _ = q.shape
    nqb = t // LANES
    n_rows = NSA_HEADS * LANES
    tiles, counts = _tile_schedule(sel)
    per_q = lambda b, i, *_: (b, i, 0)
    per_b = lambda b, i, *_: (b, 0, 0)
    grid_spec = pltpu.PrefetchScalarGridSpec(
        num_scalar_prefetch=2,
        grid=(n_b, nqb),
        in_specs=[pl.BlockSpec((1, LANES, 4 * LANES), per_q),
                  pl.BlockSpec((1, LANES, LANES), per_q),
                  pl.BlockSpec((1, LANES, 4 * LANES), per_q),
                  pl.BlockSpec((1, NSA_KV_GROUPS, LANES, N_SEL_LANES), lambda b, i, *_: (b, 0, i, 0)),
                  pl.BlockSpec((1, t, LANES), per_b),
                  pl.BlockSpec((1, t, 2 * LANES), per_b),
                  pl.BlockSpec((1, t, LANES), per_b),
                  pl.BlockSpec((1, t, 2 * LANES), per_b)],
        out_specs=pl.BlockSpec((1, LANES, 4 * LANES), per_q),
        scratch_shapes=[pltpu.VMEM((n_rows, LANES), BF16),
                        pltpu.VMEM((n_rows, LANES), F32),
                        pltpu.VMEM((n_rows, LANES), F32),
                        pltpu.VMEM((n_rows, LANES), F32)],
    )
    return pl.pallas_call(
        _slcwin_prompt_kernel,
        grid_spec=grid_spec,
        out_shape=jax.ShapeDtypeStruct((n_b, t, 4 * LANES), F32),
        compiler_params=pltpu.CompilerParams(dimension_semantics=("arbitrary", "arbitrary"),
                                             vmem_limit_bytes=VMEM_LIMIT),
        name="slcwin_prompt",
    )(tiles, counts, q, gates, oc, sel, sk, sv, wk, wv)


def _sb_prompt_kernel(q_ref, k_ref, v_ref, u_ref, o_ref, qm_s, carry_s, acc_s):
    i = pl.program_id(1)
    nq = LANES
    n_pairs = SB_KV_HEADS // 2
    rows = 4 * nq
    lane = lax.broadcasted_iota(jnp.int32, (rows, LANES), 1)
    qrow = lax.broadcasted_iota(jnp.int32, (rows, LANES), 0) & (nq - 1)
    low = lax.broadcasted_iota(jnp.int32, (nq, LANES), 1) < HEAD_DIM
    qpos = i * nq + qrow
    for kp in range(n_pairs):
        for r in range(2):
            ch = 2 * kp + r
            qc = q_ref[0, :, ch * LANES:(ch + 1) * LANES]
            zero = jnp.zeros_like(qc)
            qm_s[kp, (2 * r) * nq:(2 * r + 1) * nq, :] = jnp.where(low, qc, zero)
            qm_s[kp, (2 * r + 1) * nq:(2 * r + 2) * nq, :] = jnp.where(low, zero, qc)
    carry_s[...] = jnp.zeros(carry_s.shape, F32)
    acc_s[...] = jnp.zeros(acc_s.shape, F32)
    u = u_ref[...]

    def tile(kb, diag):
        k0 = pl.multiple_of(kb * LANES, LANES)
        causal = (k0 + lane) < qpos if diag else None
        for kp in range(n_pairs):
            kt = k_ref[0, pl.ds(k0, LANES), kp * LANES:(kp + 1) * LANES]
            vt = v_ref[0, pl.ds(k0, LANES), kp * LANES:(kp + 1) * LANES]
            z = _dot_nt(qm_s[kp], kt)
            a, tile_sum = _sb_weights(z, causal, carry_s[kp], u)
            acc_s[kp] = acc_s[kp] + _dot(a.astype(BF16), vt)
            carry_s[kp] = carry_s[kp] + tile_sum

    def live():
        c = carry_s[0]
        for kp in range(1, n_pairs):
            c = jnp.maximum(c, carry_s[kp])
        return jnp.max(c) > EXP_UNDERFLOW

    _tile_sweep(i + 1, lambda t: i - t, tile, live)
    for kp in range(n_pairs):
        for r in range(2):
            ch = 2 * kp + r
            o_ref[0, :, ch * LANES:(ch + 1) * LANES] = jnp.where(
                low, acc_s[kp, (2 * r) * nq:(2 * r + 1) * nq, :], acc_s[kp, (2 * r + 1) * nq:(2 * r + 2) * nq, :])


def _sb_prompt(qs, sbk, sbv, u):
    n_b, t, _ = qs.shape
    nqb = t // LANES
    n_pairs = SB_KV_HEADS // 2
    per_q = lambda b, i: (b, i, 0)
    per_b = lambda b, i: (b, 0, 0)
    return pl.pallas_call(
        _sb_prompt_kernel,
        grid=(n_b, nqb),
        in_specs=[pl.BlockSpec((1, LANES, 4 * LANES), per_q),
                  pl.BlockSpec((1, t, n_pairs * LANES), per_b),
                  pl.BlockSpec((1, t, n_pairs * LANES), per_b),
                  pl.BlockSpec((LANES, LANES), lambda b, i: (0, 0))],
        out_specs=pl.BlockSpec((1, LANES, 4 * LANES), per_q),
        out_shape=jax.ShapeDtypeStruct((n_b, t, 4 * LANES), F32),
        scratch_shapes=[pltpu.VMEM((n_pairs, 4 * LANES, LANES), BF16),
                        pltpu.VMEM((n_pairs, 4 * LANES, LANES), F32),
                        pltpu.VMEM((n_pairs, 4 * LANES, LANES), F32)],
        compiler_params=pltpu.CompilerParams(dimension_semantics=("arbitrary", "arbitrary"),
                                             vmem_limit_bytes=VMEM_LIMIT),
        name="sb_prompt",
    )(qs, sbk, sbv, u)


def _final_kernel(x_ref, oa_ref, sza_ref, ob_ref, szb_ref, w_ref, y_ref):
    half = oa_ref.shape[1]
    ma = (oa_ref[...] * sza_ref[...]).astype(BF16)
    mb = (ob_ref[...] * szb_ref[...]).astype(BF16)
    y_ref[...] = x_ref[...] + _dot(ma, w_ref[0:half, :]) + _dot(mb, w_ref[half:2 * half, :])


def _final(x2, oa, sza, ob, szb, w_out_perm, tm):
    rows, d = x2.shape
    half = oa.shape[1]
    row = lambda i: (i, 0)
    return pl.pallas_call(
        _final_kernel,
        grid=(rows // tm,),
        in_specs=[pl.BlockSpec((tm, d), row), pl.BlockSpec((tm, half), row),
                  pl.BlockSpec((tm, half), row), pl.BlockSpec((tm, half), row),
                  pl.BlockSpec((tm, half), row), pl.BlockSpec((d, d), lambda i: (0, 0))],
        out_specs=pl.BlockSpec((tm, d), row),
        out_shape=jax.ShapeDtypeStruct((rows, d), F32),
        compiler_params=pltpu.CompilerParams(dimension_semantics=("arbitrary",),
                                             vmem_limit_bytes=VMEM_LIMIT),
        name="final",
    )(x2, oa, sza, ob, szb, w_out_perm)


Q_PAD = SUBLANES
Q_SHIFT = 3
assert 1 << Q_SHIFT == Q_PAD
SAMPLE_ROWS = NSA_HEADS * Q_PAD


def _row_ids(shape):
    r = lax.broadcasted_iota(jnp.int32, shape, 0)
    return r >> Q_SHIFT, r & (Q_PAD - 1)


def _head_slopes(shape):
    hidx, _ = _row_ids(shape)
    out = jnp.zeros(shape, F32)
    for h in range(NSA_HEADS):
        out = jnp.where(hidx == h, _slope(h), out)
    return out


def _stack_nsa_queries(q):
    lane = lax.broadcasted_iota(jnp.int32, (Q_PAD, LANES), 1)
    low = lane < HEAD_DIM
    parts = []
    for g in range(NSA_KV_GROUPS):
        keep = low if g == 0 else jnp.logical_not(low)
        for c in range(HEADS_PER_GROUP):
            parts.append(jnp.where(keep, q[:, c * LANES:(c + 1) * LANES], 0.0))
    return jnp.concatenate(parts, axis=0).astype(BF16)


def _cmp_sample_kernel(q_ref, ck_ref, cv_ref, selmap_ref, ocmp_ref, selblk_ref, v_s, sel_s,
                       *, past_len, n_pick):
    n_seq = q_ref.shape[0]
    n_rows = SAMPLE_ROWS
    half = n_rows // 2
    n_c = ck_ref.shape[1]
    crow = lax.broadcasted_iota(jnp.int32, (n_rows, n_c), 1)
    cend = crow * CMP_STRIDE + (CMP_STRIDE - 1)
    cq = past_len + _row_ids((n_rows, n_c))[1]
    cmask = jnp.logical_and(cend <= cq, crow >= 1)
    slope = _head_slopes((n_rows, n_c))
    crel = (cq - cend).astype(F32)
    n_past_blocks = past_len // SEL_BLOCK
    lane = lax.broadcasted_iota(jnp.int32, (NSA_KV_GROUPS * Q_PAD, N_SEL_LANES), 1)
    forced = jnp.logical_or(lane == 0, lane == n_past_blocks - 1)
    allowed = lane < n_past_blocks
    rows_seq = NSA_KV_GROUPS * Q_PAD
    for b in range(n_seq):
        p = _cmp_probs(_stack_nsa_queries(q_ref[b]), ck_ref[b], slope, crel, cmask)
        pb = p.astype(BF16)
        ocmp_ref[b, 0:half, :] = _dot(pb[0:half], cv_ref[b, :, 0:LANES])
        ocmp_ref[b, half:n_rows, :] = _dot(pb[half:n_rows], cv_ref[b, :, LANES:2 * LANES])
        pg = []
        for g in range(NSA_KV_GROUPS):
            acc = jnp.zeros((Q_PAD, n_c), F32)
            for c in range(HEADS_PER_GROUP):
                r0 = (g * HEADS_PER_GROUP + c) * Q_PAD
                acc = acc + p[r0:r0 + Q_PAD]
            pg.append(acc)
        imp = _dot_split(jnp.concatenate(pg, axis=0), selmap_ref[...])
        v_s[b * rows_seq:(b + 1) * rows_seq, :] = jnp.where(allowed, jnp.where(forced, BIG, imp), -BIG)
    lane_f = lax.broadcasted_iota(jnp.int32, v_s.shape, 1).astype(F32)
    _pick_blocks(v_s, sel_s, lane_f, n_pick)
    allowed_all = lax.broadcasted_iota(jnp.int32, v_s.shape, 1) < n_past_blocks
    sel32 = jnp.where(allowed_all, sel_s[...], 0.0)
    for b in range(n_seq):
        selblk_ref[b] = sel32[b * rows_seq:(b + 1) * rows_seq]


def _cmp_sample(q8, ck, cv, selmap, past_len, n_pick, n_seq):
    n_b = q8.shape[0]
    n_c = ck.shape[1]
    rows_seq = NSA_KV_GROUPS * Q_PAD
    kern = functools.partial(_cmp_sample_kernel, past_len=past_len, n_pick=n_pick)
    blk = lambda b: (b, 0, 0)
    return pl.pallas_call(
        kern,
        grid=(n_b // n_seq,),
        in_specs=[pl.BlockSpec((n_seq, Q_PAD, 4 * LANES), blk),
                  pl.BlockSpec((n_seq, n_c, LANES), blk),
                  pl.BlockSpec((n_seq, n_c, 2 * LANES), blk),
                  pl.BlockSpec((n_c, N_SEL_LANES), lambda b: (0, 0))],
        out_specs=[pl.BlockSpec((n_seq, SAMPLE_ROWS, LANES), blk),
                   pl.BlockSpec((n_seq, rows_seq, N_SEL_LANES), blk)],
        out_shape=[jax.ShapeDtypeStruct((n_b, SAMPLE_ROWS, LANES), F32),
                   jax.ShapeDtypeStruct((n_b, rows_seq, N_SEL_LANES), F32)],
        scratch_shapes=[pltpu.VMEM((n_seq * rows_seq, N_SEL_LANES), F32),
                        pltpu.VMEM((n_seq * rows_seq, N_SEL_LANES), F32)],
        compiler_params=pltpu.CompilerParams(dimension_semantics=("arbitrary",),
                                             vmem_limit_bytes=VMEM_LIMIT),
        name="cmp_sample",
    )(q8, ck, cv, selmap)


def _pad_keys(x):
    return jnp.concatenate([x, jnp.zeros((LANES - Q_PAD, x.shape[1]), F32)], axis=0)


def _value_variants(v, axis):
    first = lax.broadcasted_iota(jnp.int32, v.shape, axis) < HEAD_DIM
    one = jnp.ones_like(v)
    return jnp.where(first, v, one).astype(BF16), jnp.where(first, one, v).astype(BF16)


def _stack_sb_queries(qs):
    lane8 = lax.broadcasted_iota(jnp.int32, (Q_PAD, LANES), 1)
    low8 = lane8 < HEAD_DIM
    parts = []
    for k in range(SB_KV_HEADS):
        for r in range(SB_HEADS // SB_KV_HEADS):
            ch = 2 * (k // 2) + r
            keep = low8 if k % 2 == 0 else jnp.logical_not(low8)
            piece = jnp.where(keep, qs[:, ch * LANES:(ch + 1) * LANES], 0.0)
            zero = jnp.zeros_like(piece)
            parts.append(jnp.concatenate([piece, zero] if k // 2 == 0 else [zero, piece], axis=1))
    return jnp.concatenate(parts, axis=0).astype(BF16)


def _sb_pages_sweep(qb, pages, carry, accb, u):
    zs, sps, sums = [], [], []
    for pg in pages:
        z = _dot(qb, pg[0, 0:2 * LANES, :].astype(BF16))
        sp = _softplus(z)
        zs.append(z)
        sps.append(sp)
        sums.append(jnp.sum(-sp, axis=1, keepdims=True))
    for pi in range(len(pages) - 1, -1, -1):
        a = jnp.exp((zs[pi] - sps[pi]) + _dot_split(-sps[pi], u) + carry)
        accb = accb + _dot_nt(a.astype(BF16), pages[pi][0, 2 * LANES:4 * LANES, :].astype(BF16))
        carry = carry + sums[pi]
    return carry, accb


def _attn_sample_kernel(*refs, past_len, dec_seq, n_slots, n_near, n_list):
    table_ref, plist_ref, pcount_ref = refs[:3]
    refs = refs[3:]
    (qa_ref, qs_ref, gate_ref, ocmp_ref, selblk_ref, slcn_ref, winn_ref, sbn_ref, state_ref, u_ref) = refs[:10]
    slc_pages = refs[10:10 + n_slots]
    sb_pages = refs[10 + n_slots:10 + n_slots + n_near]
    oa_ref, accb_ref, carry_ref = refs[10 + n_slots + n_near:13 + n_slots + n_near]
    qa_s, m_s, acc_s, win_s = refs[13 + n_slots + n_near:]

    b = pl.program_id(0)
    j = pl.program_id(1)
    n_steps = pl.num_programs(1)
    n_rows = SAMPLE_ROWS
    half = n_rows // 2
    lane = lax.broadcasted_iota(jnp.int32, (n_rows, LANES), 1)
    low = lane < HEAD_DIM
    t_q = _row_ids((n_rows, LANES))[1]
    qpos = past_len + t_q
    slope = _head_slopes((n_rows, LANES))

    @pl.when(j == 0)
    def _():
        qa = _stack_nsa_queries(qa_ref[0])
        qa_s[...] = qa
        qb = _stack_sb_queries(qs_ref[0])
        u = u_ref[...]

        new_valid = lane < dec_seq
        slcn = _pad_keys(slcn_ref[0])
        smask = jnp.logical_and(new_valid, lane <= t_q)
        s = _dot_nt(qa, slcn[:, 0:LANES].astype(BF16)) - slope * (t_q - lane).astype(F32)
        s = jnp.where(smask, s, -BIG)
        m0 = jnp.max(s, axis=1, keepdims=True)
        p = jnp.where(smask, jnp.exp(s - m0), 0.0).astype(BF16)
        v0, v1 = _value_variants(slcn[:, LANES:2 * LANES], 1)
        m_s[...] = jnp.broadcast_to(m0, (n_rows, LANES))
        acc_s[0:half, :] = _dot(p[0:half], v0)
        acc_s[half:n_rows, :] = _dot(p[half:n_rows], v1)

        sbn = _pad_keys(sbn_ref[0])
        z = _dot_nt(qb, sbn[:, 0:2 * LANES].astype(BF16))
        a, tile_sum = _sb_weights(z, jnp.logical_and(new_valid, lane < t_q), 0.0, u)
        accb = _dot(a.astype(BF16), sbn[:, 2 * LANES:4 * LANES].astype(BF16))
        carry, accb = _sb_pages_sweep(qb, sb_pages, jnp.broadcast_to(tile_sum, (n_rows, LANES)), accb, u)
        accb_ref[0] = accb
        carry_ref[0] = carry

        n_state = state_ref.shape[2]
        winn = _pad_keys(winn_ref[0])
        n_k = n_state + LANES
        col = lax.broadcasted_iota(jnp.int32, (n_rows, n_k), 1)
        kpos = jnp.where(col < n_state, past_len - n_state + col, past_len + col - n_state)
        rel = past_len + _row_ids((n_rows, n_k))[1] - kpos
        wmask = jnp.logical_and(jnp.logical_and(rel >= 0, rel < WINDOW), col < n_state + dec_seq)
        qk = jnp.concatenate([_dot(qa, state_ref[0, 0:LANES, :].astype(BF16)),
                              _dot_nt(qa, winn[:, 0:LANES].astype(BF16))], axis=1)
        s = qk - _head_slopes((n_rows, n_k)) * rel.astype(F32)
        s = jnp.where(wmask, s, -BIG)
        m = jnp.max(s, axis=1, keepdims=True)
        e = jnp.where(wmask, jnp.exp(s - m), 0.0)
        p = (e / jnp.maximum(jnp.sum(e, axis=1, keepdims=True), 1e-30)).astype(BF16)
        win_s[...] = (_dot_nt(p[:, 0:n_state], state_ref[0, LANES:2 * LANES, :].astype(BF16))
                      + _dot(p[:, n_state:n_k], winn[:, LANES:2 * LANES].astype(BF16)))

    n_used = pcount_ref[b]

    @pl.when(j * n_slots < n_used)
    def _():
        qa = qa_s[...]
        sel = selblk_ref[0].astype(BF16)
        blk_row = lax.broadcasted_iota(jnp.int32, (N_SEL_LANES, LANES), 0)
        blk_off = lax.broadcasted_iota(jnp.int32, (N_SEL_LANES, LANES), 1) >> SEL_SHIFT
        ss = []
        for pi in range(n_slots):
            slot = j * n_slots + pi
            page = plist_ref[b * n_list + slot]
            expand = jnp.where(blk_row == (LANES // SEL_BLOCK) * page + blk_off, 1.0, 0.0).astype(BF16)
            bias16 = jnp.where(slot < n_used, (_dot(sel, expand) - 1.0) * BIG, -BIG)
            bias = jnp.concatenate(
                [bias16[0:Q_PAD]] * HEADS_PER_GROUP + [bias16[Q_PAD:2 * Q_PAD]] * HEADS_PER_GROUP, axis=0)
            kpos = page * LANES + lane
            qk = _dot(qa, slc_pages[pi][0, 0:LANES, :].astype(BF16))
            ss.append((qk - slope * (qpos - kpos).astype(F32)) + bias)
        m_old = m_s[...]
        m_new = m_old
        for s in ss:
            m_new = jnp.maximum(m_new, jnp.max(s, axis=1, keepdims=True))
        pv0 = jnp.zeros((half, LANES), F32)
        pv1 = jnp.zeros((half, LANES), F32)
        for pi in range(n_slots):
            p = jnp.exp(ss[pi] - m_new).astype(BF16)
            v0, v1 = _value_variants(slc_pages[pi][0, LANES:2 * LANES, :], 0)
            pv0 = pv0 + _dot_nt(p[0:half], v0)
            pv1 = pv1 + _dot_nt(p[half:n_rows], v1)
        alpha = jnp.exp(m_old - m_new)
        acc_s[0:half, :] = alpha[0:half] * acc_s[0:half, :] + pv0
        acc_s[half:n_rows, :] = alpha[half:n_rows] * acc_s[half:n_rows, :] + pv1
        m_s[...] = m_new

    @pl.when(j == n_steps - 1)
    def _():
        o_slc = _normalise(acc_s[...])
        o_win = win_s[...]
        o_cmp = ocmp_ref[0]
        for c in range(HEADS_PER_GROUP):
            tot = []
            for g in range(NSA_KV_GROUPS):
                h = g * HEADS_PER_GROUP + c
                r0 = h * Q_PAD
                gc = gate_ref[0, :, h:h + 1]
                gs = gate_ref[0, :, NSA_HEADS + h:NSA_HEADS + h + 1]
                gw = gate_ref[0, :, 2 * NSA_HEADS + h:2 * NSA_HEADS + h + 1]
                tot.append(gc * o_cmp[r0:r0 + Q_PAD] + gs * o_slc[r0:r0 + Q_PAD] + gw * o_win[r0:r0 + Q_PAD])
            oa_ref[0, :, c * LANES:(c + 1) * LANES] = jnp.where(low[0:Q_PAD], tot[0], tot[1])


def _sb_rest_kernel(*refs, n_chunks, chunk):
    need_ref = refs[1]
    refs = refs[2:]
    qs_ref, accb_ref, carry_ref, u_ref = refs[:4]
    pages = refs[4:4 + n_chunks * chunk]
    ob_ref = refs[4 + n_chunks * chunk]
    accb_s, carry_s = refs[5 + n_chunks * chunk:]
    accb_s[...] = accb_ref[0]
    carry_s[...] = carry_ref[0]

    @pl.when(need_ref[pl.program_id(0)] > 0)
    def _():
        qb = _stack_sb_queries(qs_ref[0])
        u = u_ref[...]
        for c in range(n_chunks - 1, -1, -1):
            @pl.when(jnp.max(carry_s[...]) > EXP_UNDERFLOW)
            def _():
                carry, accb = _sb_pages_sweep(qb, pages[c * chunk:(c + 1) * chunk], carry_s[...], accb_s[...], u)
                carry_s[...] = carry
                accb_s[...] = accb

    low = lax.broadcasted_iota(jnp.int32, (Q_PAD, LANES), 1) < HEAD_DIM
    acc_b = accb_s[...]
    for kp in range(SB_KV_HEADS // 2):
        for r in range(SB_HEADS // SB_KV_HEADS):
            ra = ((2 * kp) * 2 + r) * Q_PAD
            rb = ((2 * kp + 1) * 2 + r) * Q_PAD
            a0 = acc_b[ra:ra + Q_PAD, kp * LANES:(kp + 1) * LANES]
            a1 = acc_b[rb:rb + Q_PAD, kp * LANES:(kp + 1) * LANES]
            ch = 2 * kp + r
            ob_ref[0, :, ch * LANES:(ch + 1) * LANES] = jnp.where(low, a0, a1)


def _page_schedule(selblk, n_pages, n_list, n_slots):
    n_b = selblk.shape[0]
    per_page = LANES // SEL_BLOCK
    used = jnp.max(selblk[:, :, :n_pages * per_page].reshape(n_b, -1, n_pages, per_page), axis=(1, 3)) > 0
    page_id = jnp.arange(n_pages, dtype=jnp.int32)
    order = -jnp.sort(-jnp.where(used, page_id[None, :], -1), axis=-1)[:, :n_list]
    order = jnp.pad(order, ((0, 0), (0, n_list - order.shape[1])), constant_values=-1)
    steps = [order[:, 0:n_slots]]
    steps[0] = jnp.where(steps[0] < 0, order[:, 0:1], steps[0])
    for j in range(1, n_list // n_slots):
        cur = order[:, j * n_slots:(j + 1) * n_slots]
        steps.append(jnp.where(cur < 0, steps[j - 1], cur))
    plist = jnp.maximum(jnp.concatenate(steps, axis=1), 0)
    return plist.reshape(-1).astype(jnp.int32), jnp.minimum(jnp.sum(used, axis=-1), n_list).astype(jnp.int32)


def _attn_sample(table, plist, pcount, qa8, qs8, gate8, ocmp, selblk, slcn, winn, sbn, state, u,
                 slc_pool, sb_pool, n_pages, n_list, n_slots, n_near, past_len, dec_seq):
    n_b = qa8.shape[0]
    n_rows = SAMPLE_ROWS
    n_state = state.shape[2]
    kern = functools.partial(_attn_sample_kernel, past_len=past_len, dec_seq=dec_seq,
                             n_slots=n_slots, n_near=n_near, n_list=n_list)
    per_b = lambda b, j, *_: (b, 0, 0)

    def slc_spec(p):
        return pl.BlockSpec((1, 2 * LANES, LANES),
                            lambda b, j, t, pls, pc: (t[b * n_pages + pls[b * n_list + j * n_slots + p]], 0, 0))

    def near_spec(p):
        return pl.BlockSpec((1, 4 * LANES, LANES), lambda b, j, t, pls, pc: (t[b * n_pages + n_pages - n_near + p], 0, 0))

    grid_spec = pltpu.PrefetchScalarGridSpec(
        num_scalar_prefetch=3,
        grid=(n_b, n_list // n_slots),
        in_specs=[pl.BlockSpec((1, Q_PAD, 4 * LANES), per_b),
                  pl.BlockSpec((1, Q_PAD, 4 * LANES), per_b),
                  pl.BlockSpec((1, Q_PAD, LANES), per_b),
                  pl.BlockSpec((1, n_rows, LANES), per_b),
                  pl.BlockSpec((1, 2 * Q_PAD, N_SEL_LANES), per_b),
                  pl.BlockSpec((1, Q_PAD, 2 * LANES), per_b),
                  pl.BlockSpec((1, Q_PAD, 2 * LANES), per_b),
                  pl.BlockSpec((1, Q_PAD, 4 * LANES), per_b),
                  pl.BlockSpec((1, 2 * LANES, n_state), per_b),
                  pl.BlockSpec((LANES, LANES), lambda b, j, *_: (0, 0))]
        + [slc_spec(p) for p in range(n_slots)]
        + [near_spec(p) for p in range(n_near)],
        out_specs=[pl.BlockSpec((1, Q_PAD, 4 * LANES), per_b),
                   pl.BlockSpec((1, n_rows, 2 * LANES), per_b),
                   pl.BlockSpec((1, n_rows, LANES), per_b)],
        scratch_shapes=[pltpu.VMEM((n_rows, LANES), BF16),
                        pltpu.VMEM((n_rows, LANES), F32),
                        pltpu.VMEM((n_rows, LANES), F32),
                        pltpu.VMEM((n_rows, LANES), F32)],
    )
    return pl.pallas_call(
        kern,
        grid_spec=grid_spec,
        out_shape=[jax.ShapeDtypeStruct((n_b, Q_PAD, 4 * LANES), F32),
                   jax.ShapeDtypeStruct((n_b, n_rows, 2 * LANES), F32),
                   jax.ShapeDtypeStruct((n_b, n_rows, LANES), F32)],
        compiler_params=pltpu.CompilerParams(dimension_semantics=("arbitrary", "arbitrary"),
                                             vmem_limit_bytes=VMEM_LIMIT),
        name="attn_sample",
    )(table, plist, pcount, qa8, qs8, gate8, ocmp, selblk, slcn, winn, sbn, state, u,
      *([slc_pool] * n_slots), *([sb_pool] * n_near))


def _sb_rest(table, need, qs8, accb, carry, u, sb_pool, n_pages, n_rest, chunk):
    n_b = qs8.shape[0]
    n_rows = SAMPLE_ROWS
    kern = functools.partial(_sb_rest_kernel, n_chunks=n_rest // chunk, chunk=chunk)
    per_b = lambda b, *_: (b, 0, 0)

    def page_spec(p):
        return pl.BlockSpec((1, 4 * LANES, LANES),
                            lambda b, t, nd: (t[jnp.where(nd[b] > 0, b * n_pages + p, 0)], 0, 0))

    grid_spec = pltpu.PrefetchScalarGridSpec(
        num_scalar_prefetch=2,
        grid=(n_b,),
        in_specs=[pl.BlockSpec((1, Q_PAD, 4 * LANES), per_b),
                  pl.BlockSpec((1, n_rows, 2 * LANES), per_b),
                  pl.BlockSpec((1, n_rows, LANES), per_b),
                  pl.BlockSpec((LANES, LANES), lambda b, *_: (0, 0))]
        + [page_spec(p) for p in range(n_rest)],
        out_specs=pl.BlockSpec((1, Q_PAD, 4 * LANES), per_b),
        scratch_shapes=[pltpu.VMEM((n_rows, 2 * LANES), F32),
                        pltpu.VMEM((n_rows, LANES), F32)],
    )
    return pl.pallas_call(
        kern,
        grid_spec=grid_spec,
        out_shape=jax.ShapeDtypeStruct((n_b, Q_PAD, 4 * LANES), F32),
        compiler_params=pltpu.CompilerParams(dimension_semantics=("arbitrary",),
                                             vmem_limit_bytes=VMEM_LIMIT),
        name="sb_rest",
    )(table, need, qs8, accb, carry, u, *([sb_pool] * n_rest))


def _head_index(base, perm):
    return np.concatenate([np.arange(base + h * HEAD_DIM, base + (h + 1) * HEAD_DIM) for h in perm])


def _largest_divisor(n, cap):
    d = min(n, cap)
    while n % d:
        d -= 1
    return d


def kernel(x_prompt, x_sample, cache_cmp_kv, cache_slc_kv, cache_sb_kv, state_win_kv, page_table,
           norm_g, w_in, q_norm_g, k_norm_g, cmp_pe, cmp_w1, cmp_b1, cmp_w2, w_out):
    n_b, t, d = x_prompt.shape
    n_db, dec_seq, _ = x_sample.shape
    depth = w_in.shape[0]
    assert depth == 1, "single-layer step"
    n_pages = page_table.shape[1]
    page = cache_cmp_kv.shape[2]
    past_len = n_pages * page
    n_state = state_win_kv.shape[2]
    assert page == LANES and t % LANES == 0 and t // SEL_BLOCK <= N_SEL_LANES
    assert past_len // SEL_BLOCK <= N_SEL_LANES and dec_seq <= Q_PAD and dec_seq <= SEL_BLOCK
    assert n_state == WINDOW and past_len >= WINDOW and t >= WINDOW

    nsa_w = NSA_HEADS * HEAD_DIM
    kvw = NSA_KV_GROUPS * HEAD_DIM
    sb_w = SB_HEADS * HEAD_DIM
    sbkv_w = SB_KV_HEADS * HEAD_DIM
    o_kv = nsa_w
    o_gt = o_kv + 6 * kvw
    o_za = o_gt + N_BRANCH * NSA_HEADS
    o_qs = o_za + nsa_w
    o_kb = o_qs + sb_w
    o_zb = o_kb + 2 * sbkv_w
    w0 = w_in[0]
    cols = np.concatenate([
        _head_index(0, PERM_A), np.arange(o_kv, o_gt), _head_index(o_za, PERM_A),
        _head_index(o_qs, PERM_B), np.arange(o_kb, o_zb), _head_index(o_zb, PERM_B),
        np.arange(o_gt, o_za)])
    w_perm = jnp.pad(w0[:, cols], ((0, 0), (0, LANES - N_BRANCH * NSA_HEADS))).astype(BF16)
    rows_out = np.concatenate([_head_index(0, PERM_A), _head_index(nsa_w, PERM_B)])
    w_out_perm = w_out[0][rows_out, :].astype(BF16)

    seg = jnp.asarray(np.kron(np.eye(2), np.ones((HEAD_DIM, HEAD_DIM))), BF16)
    tile2 = lambda v: jnp.tile(v, 2).reshape(1, LANES)
    gq = tile2(q_norm_g[0])
    gkc, gks, gkw = tile2(k_norm_g[0, 0]), tile2(k_norm_g[0, 1]), tile2(k_norm_g[0, 2])
    g_in = norm_g[0].reshape(1, d)

    w1 = cmp_w1[0].reshape(2, 2, CMP_STRIDE, HEAD_DIM, CMP_HIDDEN)
    w1 = jnp.transpose(w1, (0, 2, 3, 1, 4))
    zeros = jnp.zeros_like(w1)
    wc = jnp.stack([jnp.concatenate([w1, zeros], axis=3), jnp.concatenate([zeros, w1], axis=3)], axis=2)
    wc = wc.reshape(2, CMP_STRIDE // 2, 2 * LANES, 4 * CMP_HIDDEN).astype(BF16)
    w2 = cmp_w2[0]
    z2 = jnp.zeros_like(w2)
    w2p = jnp.stack([jnp.concatenate([w2, z2], axis=2), jnp.concatenate([z2, w2], axis=2)], axis=1).astype(BF16)
    cb = _cbias(cmp_pe[0].reshape(2, 1, CMP_LEN * HEAD_DIM), cmp_w1[0],
                cmp_b1[0].reshape(2, 1, CMP_HIDDEN))

    def selection_map(n_rows):
        r = np.arange(n_rows)[:, None]
        jb = np.arange(N_SEL_LANES)[None, :]
        start = (r - 1) * CMP_STRIDE
        ok = (r >= 1) & (start < jb * SEL_BLOCK + SEL_BLOCK) & (start + CMP_LEN > jb * SEL_BLOCK)
        return jnp.asarray(ok, BF16)

    u = jnp.asarray(np.arange(LANES)[:, None] > np.arange(LANES)[None, :], BF16)

    tm = _largest_divisor(n_b * t, 256)
    xp2 = x_prompt.reshape(n_b * t, d)
    (q16, cmpkv, slckv, winkv, slck, slcv, wink, winv, gates, sza, qs16, sbkv, sbk, sbv, szb) = _proj(
        xp2, g_in, w_perm, gq, gks, gkw, seg, tm)
    n_sub = t // CMP_STRIDE
    sub_w = CMP_STRIDE * 2 * kvw
    ck, cv = _compress_prompt(cmpkv.reshape(n_b * n_sub, sub_w), wc, cb, w2p, gkc, seg,
                              n_b, _largest_divisor(n_sub, 64))
    r3 = lambda a: a.reshape(n_b, t, a.shape[-1])
    oc, sel = _cmp_prompt(r3(q16), r3(gates), ck.reshape(n_b, n_sub, LANES), cv.reshape(n_b, n_sub, 2 * LANES),
                          selection_map(n_sub), _largest_divisor(t, CMP_Q_ROWS))
    oa = _slcwin_prompt(r3(q16), r3(gates), oc, sel, r3(slck), r3(slcv), r3(wink), r3(winv))
    ob = _sb_prompt(r3(qs16), r3(sbk), r3(sbv), u)
    y_prompt = _final(xp2, oa.reshape(n_b * t, nsa_w), sza, ob.reshape(n_b * t, sb_w), szb,
                      w_out_perm, tm).reshape(n_b, t, d)
    kv5 = lambda a, n, heads: a.reshape(1, n, -1, 2, heads, HEAD_DIM)
    cmp_p = kv5(cmpkv, n_b, NSA_KV_GROUPS)
    slc_p = kv5(slckv, n_b, NSA_KV_GROUPS)
    sb_p = kv5(sbkv, n_b, SB_KV_HEADS)
    win_p = kv5(winkv, n_b, NSA_KV_GROUPS)[:, :, t - min(WINDOW, t):]

    rows_s = n_db * dec_seq
    tms = _largest_divisor(rows_s, 256)
    xs2 = x_sample.reshape(rows_s, d)
    (q16s, cmpkv_s, slckv_s, winkv_s, _, _, _, _, gates_s, sza_s, qs16s, sbkv_s, _, _, szb_s) = _proj(
        xs2, g_in, w_perm, gq, gks, gkw, seg, tms)
    table = page_table.reshape(-1).astype(jnp.int32)
    n_sub_s = n_pages * (page // CMP_STRIDE)
    chan_major = lambda a: jnp.transpose(a, (0, 2, 3, 4, 1)).reshape(a.shape[0], -1, a.shape[1])
    cks, cvs = _compress_sample(table, chan_major(cache_cmp_kv[0]), wc, cb, w2p, gkc, seg,
                                n_db, n_pages, _largest_divisor(n_pages, COMPRESS_PAGES))
    pad8 = lambda a: jnp.pad(a.reshape(n_db, dec_seq, a.shape[-1]).astype(F32),
                             ((0, 0), (0, Q_PAD - dec_seq), (0, 0)))
    qa8, qs8, gate8 = pad8(q16s), pad8(qs16s), pad8(gates_s)
    n_blocks = -(-(past_len + dec_seq) // SEL_BLOCK)
    n_pick = min(SEL_TOPK, n_blocks) - 1
    ocmp, selblk = _cmp_sample(qa8, cks.reshape(n_db, n_sub_s, LANES), cvs.reshape(n_db, n_sub_s, 2 * LANES),
                               selection_map(n_sub_s), past_len, n_pick, _largest_divisor(n_db, CMP_SAMPLE_SEQS))
    n_slots = min(SLC_SLOTS, n_pages)
    n_list = -(-min(n_pages, NSA_KV_GROUPS * n_pick) // n_slots) * n_slots
    plist, pcount = _page_schedule(selblk, n_pages, n_list, n_slots)
    n_near = min(ATTN_PAGES, n_pages)
    sb_pool = chan_major(cache_sb_kv[0])
    oa8, accb, carry = _attn_sample(table, plist, pcount, qa8, qs8, gate8, ocmp, selblk,
                                    pad8(slckv_s), pad8(winkv_s), pad8(sbkv_s),
                                    chan_major(state_win_kv[0]), u, chan_major(cache_slc_kv[0]), sb_pool,
                                    n_pages, n_list, n_slots, n_near, past_len, dec_seq)
    need = (jnp.max(carry, axis=(1, 2)) > EXP_UNDERFLOW).astype(jnp.int32)
    n_rest = n_pages - n_near
    ob8 = _sb_rest(table, need, qs8, accb, carry, u, sb_pool, n_pages, n_rest,
                   _largest_divisor(n_rest, ATTN_PAGES) if n_rest else 1)
    y_sample = _final(xs2, oa8[:, :dec_seq].reshape(rows_s, nsa_w), sza_s,
                      ob8[:, :dec_seq].reshape(rows_s, sb_w), szb_s, w_out_perm, tms).reshape(n_db, dec_seq, d)
    kv5s = lambda a, heads: a.reshape(1, n_db, dec_seq, 2, heads, HEAD_DIM)
    cmp_s = kv5s(cmpkv_s, NSA_KV_GROUPS)
    slc_s = kv5s(slckv_s, NSA_KV_GROUPS)
    sb_s = kv5s(sbkv_s, SB_KV_HEADS)
    win_new = kv5s(winkv_s, NSA_KV_GROUPS)
    win_s = jnp.concatenate([state_win_kv[:, :, dec_seq:], win_new], axis=2)

    return (y_prompt, y_sample, cmp_p, cmp_s, slc_p, slc_s, sb_p, sb_s, win_p, win_s)
```

```python
import functools

import numpy as np
import jax
import jax.numpy as jnp
from jax import lax
from jax.experimental import pallas as pl
from jax.experimental.pallas import tpu as pltpu

F32 = jnp.float32
BF16 = jnp.bfloat16

HEAD_DIM = 64
NSA_HEADS = 8
SB_HEADS = 8
NSA_KV_GROUPS = 2
SB_KV_HEADS = 4
HEADS_PER_GROUP = NSA_HEADS // NSA_KV_GROUPS
N_BRANCH = 3
CMP_LEN = 32
CMP_STRIDE = 16
CMP_HIDDEN = 128
SEL_BLOCK = 64
SEL_TOPK = 16
WINDOW = 512
RMS_EPS = 1e-6
BIG = 1e30
PICKED = -3e38
EXP_UNDERFLOW = -104.0
SCALE = HEAD_DIM ** -0.5
LOG2E = 1.4426950408889634
SEL_SHIFT = 6
assert 1 << SEL_SHIFT == SEL_BLOCK

LANES = 128
SUBLANES = 8
N_SEL_LANES = 128
VMEM_LIMIT = 52 * 1024 * 1024
CMP_Q_ROWS = 512
CMP_SAMPLE_SEQS = 8
COMPRESS_PAGES = 64
ATTN_PAGES = 8
SLC_SLOTS = 16

PERM_A = (0, 4, 1, 5, 2, 6, 3, 7)
PERM_B = (0, 2, 1, 3, 4, 6, 5, 7)


def _slope(h):
    return float(2.0 ** (-(h + 1)))


def _dot(a, b):
    return jnp.dot(a, b, preferred_element_type=F32)


def _dot_nt(a, b):
    return lax.dot_general(a, b, (((1,), (1,)), ((), ())), preferred_element_type=F32)


def _dot_split(x, m):
    hi = x.astype(BF16)
    lo = (x - hi.astype(F32)).astype(BF16)
    return _dot(hi, m) + _dot(lo, m)


def _softplus(z):
    return jnp.maximum(z, 0.0) + jnp.log(1.0 + jnp.exp(-jnp.abs(z)))


def _silu(z):
    return z * jax.nn.sigmoid(z)


def _proj_kernel(x_ref, g_ref, w_ref, gq_ref, gks_ref, gkw_ref, seg_ref,
                 q_ref, cmp_ref, slc_ref, win_ref, slck_ref, slcv_ref, wink_ref, winv_ref,
                 gate_ref, sza_ref, qs_ref, sb_ref, sbk_ref, sbv_ref, szb_ref):
    x = x_ref[...]
    ms = jnp.mean(x * x, axis=-1, keepdims=True)
    hn = ((x * lax.rsqrt(ms + RMS_EPS)) * g_ref[...]).astype(BF16)
    seg = seg_ref[...]
    lane = lax.broadcasted_iota(jnp.int32, (x.shape[0], LANES), 1)
    low = lane < HEAD_DIM

    def sec(a, n=LANES):
        return _dot(hn, w_ref[:, a:a + n])

    def headnorm(t, gain):
        ss = _dot_split(t * t, seg) * (1.0 / HEAD_DIM)
        return (t * lax.rsqrt(ss + RMS_EPS)) * gain

    def variants(v):
        one = jnp.ones_like(v)
        return jnp.where(low, v, one).astype(BF16), jnp.where(low, one, v).astype(BF16)

    for c in range(4):
        t = headnorm(sec(c * LANES), gq_ref[...])
        q_ref[:, c * LANES:(c + 1) * LANES] = (t * SCALE).astype(BF16)
    base = 4 * LANES
    cmp_ref[:, 0:LANES] = sec(base)
    cmp_ref[:, LANES:2 * LANES] = sec(base + LANES)
    for kv_out, kk_out, vv_out, gain_ref, off in (
            (slc_ref, slck_ref, slcv_ref, gks_ref, base + 2 * LANES),
            (win_ref, wink_ref, winv_ref, gkw_ref, base + 4 * LANES)):
        kn = headnorm(sec(off), gain_ref[...])
        v = sec(off + LANES)
        kv_out[:, 0:LANES] = kn
        kv_out[:, LANES:2 * LANES] = v
        kk_out[...] = kn.astype(BF16)
        v0, v1 = variants(v)
        vv_out[:, 0:LANES] = v0
        vv_out[:, LANES:2 * LANES] = v1
    base = 10 * LANES
    for c in range(4):
        sza_ref[:, c * LANES:(c + 1) * LANES] = _silu(sec(base + c * LANES))
    base = 14 * LANES
    for c in range(4):
        qs_ref[:, c * LANES:(c + 1) * LANES] = (sec(base + c * LANES) * SCALE).astype(BF16)
    base = 18 * LANES
    for c in range(2):
        kb = sec(base + c * LANES)
        vb = sec(base + (2 + c) * LANES)
        sb_ref[:, c * LANES:(c + 1) * LANES] = kb
        sb_ref[:, (2 + c) * LANES:(3 + c) * LANES] = vb
        sbk_ref[:, c * LANES:(c + 1) * LANES] = kb.astype(BF16)
        sbv_ref[:, c * LANES:(c + 1) * LANES] = vb.astype(BF16)
    base = 22 * LANES
    for c in range(4):
        szb_ref[:, c * LANES:(c + 1) * LANES] = _silu(sec(base + c * LANES))
    gate_ref[...] = jax.nn.sigmoid(sec(26 * LANES))


def _proj(x2, g, w_perm, gq, gks, gkw, seg, tm):
    rows, d = x2.shape
    wcols = w_perm.shape[1]
    row = lambda i: (i, 0)
    const = lambda i: (0, 0)
    widths = [(512, BF16), (256, F32), (256, F32), (256, F32), (128, BF16), (256, BF16),
              (128, BF16), (256, BF16), (128, F32), (512, F32), (512, BF16), (512, F32),
              (256, BF16), (256, BF16), (512, F32)]
    return pl.pallas_call(
        _proj_kernel,
        grid=(rows // tm,),
        in_specs=[pl.BlockSpec((tm, d), row), pl.BlockSpec((1, d), const),
                  pl.BlockSpec((d, wcols), const), pl.BlockSpec((1, LANES), const),
                  pl.BlockSpec((1, LANES), const), pl.BlockSpec((1, LANES), const),
                  pl.BlockSpec((LANES, LANES), const)],
        out_specs=[pl.BlockSpec((tm, w), row) for w, _ in widths],
        out_shape=[jax.ShapeDtypeStruct((rows, w), dt) for w, dt in widths],
        compiler_params=pltpu.CompilerParams(dimension_semantics=("arbitrary",),
                                             vmem_limit_bytes=VMEM_LIMIT),
        name="proj",
    )(x2, g, w_perm, gq, gks, gkw, seg)


def _cbias_kernel(pe_ref, w1_ref, b1_ref, o_ref):
    for kv in range(2):
        pe = jnp.broadcast_to(pe_ref[kv], (SUBLANES, pe_ref.shape[-1])).astype(BF16)
        o_ref[kv] = _dot(pe, w1_ref[kv].astype(BF16)) + b1_ref[kv]


def _cbias(pe_flat, w1, b1):
    return pl.pallas_call(
        _cbias_kernel,
        out_shape=jax.ShapeDtypeStruct((2, SUBLANES, CMP_HIDDEN), F32),
        name="cbias",
    )(pe_flat, w1, b1)


def _compress_kernel(*refs, n_in, has_table):
    if has_table:
        refs = refs[1:]
    x_refs = refs[:n_in]
    if has_table:
        perm_ref = refs[n_in]
        refs = refs[1:]
    wc_ref, cb_ref, w2_ref, gk_ref, seg_ref = refs[n_in:n_in + 5]
    ck_ref, cv_ref = refs[n_in + 5:n_in + 7]
    carry_ref = refs[n_in + 7]
    n_rows = ck_ref.shape[0]

    @pl.when(pl.program_id(1) == 0)
    def _():
        carry_ref[...] = jnp.zeros(carry_ref.shape, F32)

    row = lax.broadcasted_iota(jnp.int32, (n_rows, CMP_HIDDEN), 0)
    lane = lax.broadcasted_iota(jnp.int32, (n_rows, LANES), 1)
    low = lane < HEAD_DIM
    for kv in range(2):
        if has_table:
            perm = perm_ref[...]
            pages = [_dot_nt(perm, r[0, kv * LANES:(kv + 1) * LANES, :].astype(BF16)) for r in x_refs]
        def sub_rows(s):
            if has_table:
                nb = LANES // CMP_STRIDE
                return jnp.concatenate([pg[s * nb:(s + 1) * nb] for pg in pages], axis=0)
            a = s * 2 * LANES + kv * LANES
            return x_refs[0][:, a:a + LANES]

        acc = jnp.zeros((n_rows, 4 * CMP_HIDDEN), F32)
        for s2 in range(CMP_STRIDE // 2):
            xk = jnp.concatenate([sub_rows(2 * s2), sub_rows(2 * s2 + 1)], axis=1)
            acc = acc + _dot(xk.astype(BF16), wc_ref[kv, s2])
        cb = cb_ref[kv, 0:1, :]
        out = jnp.zeros((n_rows, LANES), F32)
        for g in range(NSA_KV_GROUPS):
            a0 = acc[:, (2 * g) * CMP_HIDDEN:(2 * g + 1) * CMP_HIDDEN]
            a1 = acc[:, (2 * g + 1) * CMP_HIDDEN:(2 * g + 2) * CMP_HIDDEN]
            prev = carry_ref[kv, SUBLANES - 1:SUBLANES, (2 * g) * CMP_HIDDEN:(2 * g + 1) * CMP_HIDDEN]
            shifted = jnp.where(row == 0, prev, pltpu.roll(a0, 1, axis=0))
            h = shifted + a1 + cb
            out = out + _dot(_silu(h).astype(BF16), w2_ref[kv, g])
        carry_ref[kv] = acc[n_rows - SUBLANES:, :]
        if kv == 0:
            ss = _dot_split(out * out, seg_ref[...]) * (1.0 / HEAD_DIM)
            ck_ref[...] = ((out * lax.rsqrt(ss + RMS_EPS)) * gk_ref[...]).astype(BF16)
        else:
            one = jnp.ones_like(out)
            cv_ref[:, 0:LANES] = jnp.where(low, out, one).astype(BF16)
            cv_ref[:, LANES:2 * LANES] = jnp.where(low, one, out).astype(BF16)


def _compress_out_shape(n_out_rows):
    return [jax.ShapeDtypeStruct((n_out_rows, LANES), BF16),
            jax.ShapeDtypeStruct((n_out_rows, 2 * LANES), BF16)]


def _compress_weight_specs():
    return [pl.BlockSpec((2, CMP_STRIDE // 2, 2 * LANES, 4 * CMP_HIDDEN), lambda *a: (0, 0, 0, 0)),
            pl.BlockSpec((2, SUBLANES, CMP_HIDDEN), lambda *a: (0, 0, 0)),
            pl.BlockSpec((2, NSA_KV_GROUPS, CMP_HIDDEN, LANES), lambda *a: (0, 0, 0, 0)),
            pl.BlockSpec((1, LANES), lambda *a: (0, 0)),
            pl.BlockSpec((LANES, LANES), lambda *a: (0, 0))]


def _compress_prompt(xsub, wc, cb, w2p, gk, seg, n_batch, rows_per_step):
    total = xsub.shape[0]
    per_b = total // n_batch
    nj = per_b // rows_per_step
    kern = functools.partial(_compress_kernel, n_in=1, has_table=False)
    return pl.pallas_call(
        kern,
        grid=(n_batch, nj),
        in_specs=[pl.BlockSpec((rows_per_step, xsub.shape[1]), lambda b, j: (b * nj + j, 0))]
        + _compress_weight_specs(),
        out_specs=[pl.BlockSpec((rows_per_step, LANES), lambda b, j: (b * nj + j, 0)),
                   pl.BlockSpec((rows_per_step, 2 * LANES), lambda b, j: (b * nj + j, 0))],
        out_shape=_compress_out_shape(total),
        scratch_shapes=[pltpu.VMEM((2, SUBLANES, 4 * CMP_HIDDEN), F32)],
        compiler_params=pltpu.CompilerParams(dimension_semantics=("arbitrary", "arbitrary"),
                                             vmem_limit_bytes=VMEM_LIMIT),
        name="compress_prompt",
    )(xsub, wc, cb, w2p, gk, seg)


def _compress_sample(table, pool_t, wc, cb, w2p, gk, seg, n_batch, n_pages, pages_per_step):
    page = pool_t.shape[2]
    sub_per_page = page // CMP_STRIDE
    nj = n_pages // pages_per_step
    rows_per_step = pages_per_step * sub_per_page
    total = n_batch * n_pages * sub_per_page
    kern = functools.partial(_compress_kernel, n_in=pages_per_step, has_table=True)
    rho = np.arange(page)
    perm = jnp.asarray(np.arange(page)[None, :] == (CMP_STRIDE * (rho % sub_per_page) + rho // sub_per_page)[:, None],
                       BF16)

    def page_spec(p):
        return pl.BlockSpec((1, pool_t.shape[1], page),
                            lambda b, j, t: (t[b * n_pages + j * pages_per_step + p], 0, 0))

    grid_spec = pltpu.PrefetchScalarGridSpec(
        num_scalar_prefetch=1,
        grid=(n_batch, nj),
        in_specs=[page_spec(p) for p in range(pages_per_step)]
        + [pl.BlockSpec((page, page), lambda b, j, t: (0, 0))] + _compress_weight_specs(),
        out_specs=[pl.BlockSpec((rows_per_step, LANES), lambda b, j, t: (b * nj + j, 0)),
                   pl.BlockSpec((rows_per_step, 2 * LANES), lambda b, j, t: (b * nj + j, 0))],
        scratch_shapes=[pltpu.VMEM((2, SUBLANES, 4 * CMP_HIDDEN), F32)],
    )
    return pl.pallas_call(
        kern,
        grid_spec=grid_spec,
        out_shape=_compress_out_shape(total),
        compiler_params=pltpu.CompilerParams(dimension_semantics=("arbitrary", "arbitrary"),
                                             vmem_limit_bytes=VMEM_LIMIT),
        name="compress_sample",
    )(table, *([pool_t] * pages_per_step), perm, wc, cb, w2p, gk, seg)


def _pick_blocks(v_s, sel_s, lane_f, n_pick):
    sel_s[...] = jnp.zeros(sel_s.shape, F32)

    def body(_, c):
        v = v_s[...]
        mx = jnp.max(v, axis=1, keepdims=True)
        cand = jnp.where(v == mx, lane_f, float(2 * N_SEL_LANES))
        idx = jnp.min(cand, axis=1, keepdims=True)
        hit = lane_f == idx
        sel_s[...] = jnp.where(hit, 1.0, sel_s[...])
        v_s[...] = jnp.where(hit, PICKED, v)
        return c

    lax.fori_loop(0, n_pick, body, 0)


def _cmp_probs(qh, ck, slope, crel, cmask):
    s = _dot_nt(qh, ck) - slope * crel
    s = jnp.where(cmask, s, -BIG)
    m = jnp.max(s, axis=1, keepdims=True)
    e = jnp.where(cmask, jnp.exp(s - m), 0.0)
    return e / jnp.maximum(jnp.sum(e, axis=1, keepdims=True), 1e-30)


def _normalise(acc):
    return acc / jnp.maximum(pltpu.roll(acc, HEAD_DIM, axis=1), 1e-30)


def _sb_weights(z, causal, carry, u):
    sp = _softplus(z)
    l1m = -sp if causal is None else jnp.where(causal, -sp, 0.0)
    a = jnp.exp((z - sp) + _dot_split(l1m, u) + carry)
    if causal is not None:
        a = jnp.where(causal, a, 0.0)
    return a, jnp.sum(l1m, axis=1, keepdims=True)


def _tile_sweep(n_tiles, kb_of, tile, live=None, pair=None):
    tile(kb_of(0), True)
    n_rest = n_tiles - 1
    odd = n_rest & 1

    @pl.when(odd == 1)
    def _():
        tile(kb_of(1), False)

    start = 1 + odd
    n_trips = n_rest >> 1

    def run(t):
        if pair is None:
            tile(kb_of(start + 2 * t), False)
            tile(kb_of(start + 2 * t + 1), False)
        else:
            pair(kb_of(start + 2 * t), kb_of(start + 2 * t + 1))

    if live is None:
        def body(t, c):
            run(t)
            return c

        lax.fori_loop(0, n_trips, body, 0)
    else:
        def cond(state):
            t, go = state
            return jnp.logical_and(t < n_trips, go)

        def wbody(state):
            run(state[0])
            return state[0] + 1, live()

        lax.while_loop(cond, wbody, (jnp.int32(0), live()))


def _cmp_prompt_kernel(q_ref, gate_ref, ck_ref, cv_ref, selmap_ref, oc_ref, sel_ref, v_s, sel_s):
    nq = q_ref.shape[1]
    q0 = pl.program_id(1) * nq
    n_c = ck_ref.shape[1]
    lane = lax.broadcasted_iota(jnp.int32, (nq, LANES), 1)
    low = lane < HEAD_DIM
    cur = (q0 + lax.broadcasted_iota(jnp.int32, (nq, LANES), 0)) >> SEL_SHIFT
    forced = jnp.logical_or(lane == 0, jnp.logical_or(lane == cur, lane == cur - 1))
    allowed = lane <= cur

    def attend(width):
        crow = lax.broadcasted_iota(jnp.int32, (nq, width), 1)
        cend = crow * CMP_STRIDE + (CMP_STRIDE - 1)
        cq = q0 + lax.broadcasted_iota(jnp.int32, (nq, width), 0)
        cmask = jnp.logical_and(cend <= cq, crow >= 1)
        crel = (cq - cend).astype(F32)
        ck = ck_ref[0, 0:width, :]
        outs = [[None] * HEADS_PER_GROUP for _ in range(NSA_KV_GROUPS)]
        for g in range(NSA_KV_GROUPS):
            keep = low if g == 0 else jnp.logical_not(low)
            imp = jnp.zeros((nq, N_SEL_LANES), F32)
            for c in range(HEADS_PER_GROUP):
                h = g * HEADS_PER_GROUP + c
                qc = q_ref[0, :, c * LANES:(c + 1) * LANES]
                p = _cmp_probs(jnp.where(keep, qc, jnp.zeros_like(qc)), ck, _slope(h), crel, cmask)
                oc = _dot(p.astype(BF16), cv_ref[0, 0:width, g * LANES:(g + 1) * LANES])
                outs[g][c] = oc * gate_ref[0, :, h:h + 1]
                imp = imp + _dot_split(p, selmap_ref[0:width, :])
            v_s[g * nq:(g + 1) * nq, :] = jnp.where(forced, BIG, jnp.where(allowed, imp, -BIG))
        for c in range(HEADS_PER_GROUP):
            oc_ref[0, :, c * LANES:(c + 1) * LANES] = jnp.where(low, outs[0][c], outs[1][c])

    w_step = min(LANES, n_c)
    n_needed = (q0 + nq) // CMP_STRIDE
    n_cols = (n_needed + (w_step - 1)) // w_step
    for k in range(1, n_c // w_step + 1):
        pl.when(n_cols == k)(functools.partial(attend, k * w_step))

    lane_f = lax.broadcasted_iota(jnp.int32, v_s.shape, 1).astype(F32)
    _pick_blocks(v_s, sel_s, lane_f, SEL_TOPK)
    for g in range(NSA_KV_GROUPS):
        sel_ref[0, g] = jnp.where(allowed, sel_s[g * nq:(g + 1) * nq, :], 0.0).astype(BF16)


def _cmp_prompt(q, gates, ck, cv, selmap, nq):
    n_b, t, _ = q.shape
    n_c = ck.shape[1]
    per_q = lambda b, i: (b, i, 0)
    per_b = lambda b, i: (b, 0, 0)
    return pl.pallas_call(
        _cmp_prompt_kernel,
        grid=(n_b, t // nq),
        in_specs=[pl.BlockSpec((1, nq, 4 * LANES), per_q),
                  pl.BlockSpec((1, nq, LANES), per_q),
                  pl.BlockSpec((1, n_c, LANES), per_b),
                  pl.BlockSpec((1, n_c, 2 * LANES), per_b),
                  pl.BlockSpec((n_c, N_SEL_LANES), lambda b, i: (0, 0))],
        out_specs=[pl.BlockSpec((1, nq, 4 * LANES), per_q),
                   pl.BlockSpec((1, NSA_KV_GROUPS, nq, N_SEL_LANES), lambda b, i: (b, 0, i, 0))],
        out_shape=[jax.ShapeDtypeStruct((n_b, t, 4 * LANES), F32),
                   jax.ShapeDtypeStruct((n_b, NSA_KV_GROUPS, t, N_SEL_LANES), BF16)],
        scratch_shapes=[pltpu.VMEM((NSA_KV_GROUPS * nq, N_SEL_LANES), F32),
                        pltpu.VMEM((NSA_KV_GROUPS * nq, N_SEL_LANES), F32)],
        compiler_params=pltpu.CompilerParams(dimension_semantics=("arbitrary", "arbitrary"),
                                             vmem_limit_bytes=VMEM_LIMIT),
        name="cmp_prompt",
    )(q, gates, ck, cv, selmap)


def _slcwin_prompt_kernel(tiles_ref, count_ref, q_ref, gate_ref, oc_ref, sel_ref, sk_ref, sv_ref, wk_ref, wv_ref,
                          o_ref, qm_s, m_s, acc_s, tot_s):
    i = pl.program_id(1)
    step = pl.program_id(0) * pl.num_programs(1) + i
    tiles_per_step = sk_ref.shape[1] // LANES
    nq = LANES
    lane = lax.broadcasted_iota(jnp.int32, (nq, LANES), 1)
    row = lax.broadcasted_iota(jnp.int32, (nq, LANES), 0)
    low = lane < HEAD_DIM
    qpos = i * nq + row

    for g in range(NSA_KV_GROUPS):
        keep = low if g == 0 else jnp.logical_not(low)
        for c in range(HEADS_PER_GROUP):
            qc = q_ref[0, :, c * LANES:(c + 1) * LANES]
            r0 = (g * HEADS_PER_GROUP + c) * nq
            qm_s[r0:r0 + nq, :] = (jnp.where(keep, qc, jnp.zeros_like(qc)).astype(F32) * LOG2E).astype(BF16)

    def run_branch(k_ref, v_ref, banded, n_tiles, kb_of, gate_base, first_branch):
        m_s[...] = jnp.full(m_s.shape, -BIG, F32)
        acc_s[...] = jnp.zeros(acc_s.shape, F32)

        def tiles(kbs, diag):
            k0s = [pl.multiple_of(kb * LANES, LANES) for kb in kbs]
            rels = [qpos - (k0 + lane) for k0 in k0s]
            kt = jnp.concatenate([k_ref[0, pl.ds(k0, LANES), :] for k0 in k0s], axis=0)
            s = _dot_nt(qm_s[...], kt)
            relf = jnp.concatenate([rel.astype(F32) for rel in rels], axis=1)
            if banded:
                band = jnp.concatenate(
                    [jnp.where(jnp.logical_and(rel >= 0, rel < WINDOW), 0.0, -BIG) for rel in rels], axis=1)
            else:
                expands = [jnp.where(row == (LANES // SEL_BLOCK) * kb + (lane >> SEL_SHIFT), 1.0, 0.0).astype(BF16)
                           for kb in kbs]
            for g in range(NSA_KV_GROUPS):
                if banded:
                    bias = band
                else:
                    parts = [(_dot(sel_ref[0, g], e) - 1.0) * BIG for e in expands]
                    if diag:
                        parts = [jnp.where(rel >= 0, b, -BIG) for rel, b in zip(rels, parts)]
                    bias = jnp.concatenate(parts, axis=1)
                ps, alphas = [], []
                for c in range(HEADS_PER_GROUP):
                    h = g * HEADS_PER_GROUP + c
                    r0 = h * nq
                    sh = (s[r0:r0 + nq] - (_slope(h) * LOG2E) * relf) + bias
                    m_old = m_s[r0:r0 + nq, :]
                    m_new = jnp.maximum(m_old, jnp.max(sh, axis=1, keepdims=True))
                    ps.append(jnp.exp2(sh - jnp.concatenate([m_new] * len(kbs), axis=1)).astype(BF16))
                    alphas.append(jnp.exp2(m_old - m_new))
                    m_s[r0:r0 + nq, :] = m_new
                g0 = g * HEADS_PER_GROUP * nq
                g1 = (g + 1) * HEADS_PER_GROUP * nq
                vt = jnp.concatenate([v_ref[0, pl.ds(k0, LANES), g * LANES:(g + 1) * LANES] for k0 in k0s], axis=0)
                pv = _dot(jnp.concatenate(ps, axis=0), vt)
                acc_s[g0:g1, :] = jnp.concatenate(alphas, axis=0) * acc_s[g0:g1, :] + pv

        _tile_sweep(n_tiles, kb_of, lambda kb, diag: tiles([kb], diag),
                    pair=lambda ka, kb: tiles([ka, kb], False))
        for h in range(NSA_HEADS):
            r0 = h * nq
            o = _normalise(acc_s[r0:r0 + nq, :]) * gate_ref[0, :, gate_base + h:gate_base + h + 1]
            tot_s[r0:r0 + nq, :] = o if first_branch else tot_s[r0:r0 + nq, :] + o

    run_branch(sk_ref, sv_ref, False, count_ref[step], lambda t: tiles_ref[step * tiles_per_step + t],
               NSA_HEADS, True)
    run_branch(wk_ref, wv_ref, True, jnp.minimum(i, WINDOW // LANES) + 1, lambda t: i - t,
               2 * NSA_HEADS, False)

    for c in range(HEADS_PER_GROUP):
        a = tot_s[c * nq:(c + 1) * nq, :]
        b = tot_s[(HEADS_PER_GROUP + c) * nq:(HEADS_PER_GROUP + c + 1) * nq, :]
        o_ref[0, :, c * LANES:(c + 1) * LANES] = oc_ref[0, :, c * LANES:(c + 1) * LANES] + jnp.where(low, a, b)


def _tile_schedule(sel):
    n_b, _, t, n_blk = sel.shape
    nqb = t // LANES
    per_tile = LANES // SEL_BLOCK
    used = jnp.max(sel.reshape(n_b, NSA_KV_GROUPS, nqb, LANES, n_blk // per_tile, per_tile), axis=(1, 3, 5)) > 0
    kb = jnp.arange(n_blk // per_tile, dtype=jnp.int32)
    used = jnp.logical_and(used[:, :, :nqb], kb[None, None, :nqb] <= jnp.arange(nqb, dtype=jnp.int32)[None, :, None])
    order = -jnp.sort(-jnp.where(used, kb[None, None, :nqb], -1), axis=-1)
    return jnp.maximum(order, 0).reshape(-1).astype(jnp.int32), jnp.sum(used, axis=-1).reshape(-1).astype(jnp.int32)


def _slcwin_prompt(q, gates, oc, sel, sk, sv, wk, wv):
    n_b, t, _ = q.shape
    nqb = t // LANES
    n_rows = NSA_HEADS * LANES
    tiles, counts = _tile_schedule(sel)
    per_q = lambda b, i, *_: (b, i, 0)
    per_b = lambda b, i, *_: (b, 0, 0)
    grid_spec = pltpu.PrefetchScalarGridSpec(
        num_scalar_prefetch=2,
        grid=(n_b, nqb),
        in_specs=[pl.BlockSpec((1, LANES, 4 * LANES), per_q),
                  pl.BlockSpec((1, LANES, LANES), per_q),
                  pl.BlockSpec((1, LANES, 4 * LANES), per_q),
                  pl.BlockSpec((1, NSA_KV_GROUPS, LANES, N_SEL_LANES), lambda b, i, *_: (b, 0, i, 0)),
                  pl.BlockSpec((1, t, LANES), per_b),
                  pl.BlockSpec((1, t, 2 * LANES), per_b),
                  pl.BlockSpec((1, t, LANES), per_b),
                  pl.BlockSpec((1, t, 2 * LANES), per_b)],
        out_specs=pl.BlockSpec((1, LANES, 4 * LANES), per_q),
        scratch_shapes=[pltpu.VMEM((n_rows, LANES), BF16),
                        pltpu.VMEM((n_rows, LANES), F32),
                        pltpu.VMEM((n_rows, LANES), F32),
                        pltpu.VMEM((n_rows, LANES), F32)],
    )
    return pl.pallas_call(
        _slcwin_prompt_kernel,
        grid_spec=grid_spec,
        out_shape=jax.ShapeDtypeStruct((n_b, t, 4 * LANES), F32),
        compiler_params=pltpu.CompilerParams(dimension_semantics=("arbitrary", "arbitrary"),
                                             vmem_limit_bytes=VMEM_LIMIT),
        name="slcwin_prompt",
    )(tiles, counts, q, gates, oc, sel, sk, sv, wk, wv)


def _sb_prompt_kernel(q_ref, k_ref, v_ref, u_ref, o_ref, qm_s, carry_s, acc_s):
    i = pl.program_id(1)
    nq = LANES
    n_pairs = SB_KV_HEADS // 2
    rows = 4 * nq
    lane = lax.broadcasted_iota(jnp.int32, (rows, LANES), 1)
    qrow = lax.broadcasted_iota(jnp.int32, (rows, LANES), 0) & (nq - 1)
    low = lax.broadcasted_iota(jnp.int32, (nq, LANES), 1) < HEAD_DIM
    qpos = i * nq + qrow
    for kp in range(n_pairs):
        for r in range(2):
            ch = 2 * kp + r
            qc = q_ref[0, :, ch * LANES:(ch + 1) * LANES]
            zero = jnp.zeros_like(qc)
            qm_s[kp, (2 * r) * nq:(2 * r + 1) * nq, :] = jnp.where(low, qc, zero)
            qm_s[kp, (2 * r + 1) * nq:(2 * r + 2) * nq, :] = jnp.where(low, zero, qc)
    carry_s[...] = jnp.zeros(carry_s.shape, F32)
    acc_s[...] = jnp.zeros(acc_s.shape, F32)
    u = u_ref[...]

    def tile(kb, diag):
        k0 = pl.multiple_of(kb * LANES, LANES)
        causal = (k0 + lane) < qpos if diag else None
        for kp in range(n_pairs):
            kt = k_ref[0, pl.ds(k0, LANES), kp * LANES:(kp + 1) * LANES]
            vt = v_ref[0, pl.ds(k0, LANES), kp * LANES:(kp + 1) * LANES]
            z = _dot_nt(qm_s[kp], kt)
            a, tile_sum = _sb_weights(z, causal, carry_s[kp], u)
            acc_s[kp] = acc_s[kp] + _dot(a.astype(BF16), vt)
            carry_s[kp] = carry_s[kp] + tile_sum

    def live():
        c = carry_s[0]
        for kp in range(1, n_pairs):
            c = jnp.maximum(c, carry_s[kp])
        return jnp.max(c) > EXP_UNDERFLOW

    _tile_sweep(i + 1, lambda t: i - t, tile, live)
    for kp in range(n_pairs):
        for r in range(2):
            ch = 2 * kp + r
            o_ref[0, :, ch * LANES:(ch + 1) * LANES] = jnp.where(
                low, acc_s[kp, (2 * r) * nq:(2 * r + 1) * nq, :], acc_s[kp, (2 * r + 1) * nq:(2 * r + 2) * nq, :])


def _sb_prompt(qs, sbk, sbv, u):
    n_b, t, _ = qs.shape
    nqb = t // LANES
    n_pairs = SB_KV_HEADS // 2
    per_q = lambda b, i: (b, i, 0)
    per_b = lambda b, i: (b, 0, 0)
    return pl.pallas_call(
        _sb_prompt_kernel,
        grid=(n_b, nqb),
        in_specs=[pl.BlockSpec((1, LANES, 4 * LANES), per_q),
                  pl.BlockSpec((1, t, n_pairs * LANES), per_b),
                  pl.BlockSpec((1, t, n_pairs * LANES), per_b),
                  pl.BlockSpec((LANES, LANES), lambda b, i: (0, 0))],
        out_specs=pl.BlockSpec((1, LANES, 4 * LANES), per_q),
        out_shape=jax.ShapeDtypeStruct((n_b, t, 4 * LANES), F32),
        scratch_shapes=[pltpu.VMEM((n_pairs, 4 * LANES, LANES), BF16),
                        pltpu.VMEM((n_pairs, 4 * LANES, LANES), F32),
                        pltpu.VMEM((n_pairs, 4 * LANES, LANES), F32)],
        compiler_params=pltpu.CompilerParams(dimension_semantics=("arbitrary", "arbitrary"),
                                             vmem_limit_bytes=VMEM_LIMIT),
        name="sb_prompt",
    )(qs, sbk, sbv, u)


def _final_kernel(x_ref, oa_ref, sza_ref, ob_ref, szb_ref, w_ref, y_ref):
    half = oa_ref.shape[1]
    ma = (oa_ref[...] * sza_ref[...]).astype(BF16)
    mb = (ob_ref[...] * szb_ref[...]).astype(BF16)
    y_ref[...] = x_ref[...] + _dot(ma, w_ref[0:half, :]) + _dot(mb, w_ref[half:2 * half, :])


def _final(x2, oa, sza, ob, szb, w_out_perm, tm):
    rows, d = x2.shape
    half = oa.shape[1]
    row = lambda i: (i, 0)
    return pl.pallas_call(
        _final_kernel,
        grid=(rows // tm,),
        in_specs=[pl.BlockSpec((tm, d), row), pl.BlockSpec((tm, half), row),
                  pl.BlockSpec((tm, half), row), pl.BlockSpec((tm, half), row),
                  pl.BlockSpec((tm, half), row), pl.BlockSpec((d, d), lambda i: (0, 0))],
        out_specs=pl.BlockSpec((tm, d), row),
        out_shape=jax.ShapeDtypeStruct((rows, d), F32),
        compiler_params=pltpu.CompilerParams(dimension_semantics=("arbitrary",),
                                             vmem_limit_bytes=VMEM_LIMIT),
        name="final",
    )(x2, oa, sza, ob, szb, w_out_perm)


Q_PAD = SUBLANES
Q_SHIFT = 3
assert 1 << Q_SHIFT == Q_PAD
SAMPLE_ROWS = NSA_HEADS * Q_PAD


def _row_ids(shape):
    r = lax.broadcasted_iota(jnp.int32, shape, 0)
    return r >> Q_SHIFT, r & (Q_PAD - 1)


def _head_slopes(shape):
    hidx, _ = _row_ids(shape)
    out = jnp.zeros(shape, F32)
    for h in range(NSA_HEADS):
        out = jnp.where(hidx == h, _slope(h), out)
    return out


def _stack_nsa_queries(q):
    lane = lax.broadcasted_iota(jnp.int32, (Q_PAD, LANES), 1)
    low = lane < HEAD_DIM
    parts = []
    for g in range(NSA_KV_GROUPS):
        keep = low if g == 0 else jnp.logical_not(low)
        for c in range(HEADS_PER_GROUP):
            parts.append(jnp.where(keep, q[:, c * LANES:(c + 1) * LANES], 0.0))
    return jnp.concatenate(parts, axis=0).astype(BF16)


def _cmp_sample_kernel(q_ref, ck_ref, cv_ref, selmap_ref, ocmp_ref, selblk_ref, v_s, sel_s,
                       *, past_len, n_pick):
    n_seq = q_ref.shape[0]
    n_rows = SAMPLE_ROWS
    half = n_rows // 2
    n_c = ck_ref.shape[1]
    crow = lax.broadcasted_iota(jnp.int32, (n_rows, n_c), 1)
    cend = crow * CMP_STRIDE + (CMP_STRIDE - 1)
    cq = past_len + _row_ids((n_rows, n_c))[1]
    cmask = jnp.logical_and(cend <= cq, crow >= 1)
    slope = _head_slopes((n_rows, n_c))
    crel = (cq - cend).astype(F32)
    n_past_blocks = past_len // SEL_BLOCK
    lane = lax.broadcasted_iota(jnp.int32, (NSA_KV_GROUPS * Q_PAD, N_SEL_LANES), 1)
    forced = jnp.logical_or(lane == 0, lane == n_past_blocks - 1)
    allowed = lane < n_past_blocks
    rows_seq = NSA_KV_GROUPS * Q_PAD
    for b in range(n_seq):
        p = _cmp_probs(_stack_nsa_queries(q_ref[b]), ck_ref[b], slope, crel, cmask)
        pb = p.astype(BF16)
        ocmp_ref[b, 0:half, :] = _dot(pb[0:half], cv_ref[b, :, 0:LANES])
        ocmp_ref[b, half:n_rows, :] = _dot(pb[half:n_rows], cv_ref[b, :, LANES:2 * LANES])
        pg = []
        for g in range(NSA_KV_GROUPS):
            acc = jnp.zeros((Q_PAD, n_c), F32)
            for c in range(HEADS_PER_GROUP):
                r0 = (g * HEADS_PER_GROUP + c) * Q_PAD
                acc = acc + p[r0:r0 + Q_PAD]
            pg.append(acc)
        imp = _dot_split(jnp.concatenate(pg, axis=0), selmap_ref[...])
        v_s[b * rows_seq:(b + 1) * rows_seq, :] = jnp.where(allowed, jnp.where(forced, BIG, imp), -BIG)
    lane_f = lax.broadcasted_iota(jnp.int32, v_s.shape, 1).astype(F32)
    _pick_blocks(v_s, sel_s, lane_f, n_pick)
    allowed_all = lax.broadcasted_iota(jnp.int32, v_s.shape, 1) < n_past_blocks
    sel32 = jnp.where(allowed_all, sel_s[...], 0.0)
    for b in range(n_seq):
        selblk_ref[b] = sel32[b * rows_seq:(b + 1) * rows_seq]


def _cmp_sample(q8, ck, cv, selmap, past_len, n_pick, n_seq):
    n_b = q8.shape[0]
    n_c = ck.shape[1]
    rows_seq = NSA_KV_GROUPS * Q_PAD
    kern = functools.partial(_cmp_sample_kernel, past_len=past_len, n_pick=n_pick)
    blk = lambda b: (b, 0, 0)
    return pl.pallas_call(
        kern,
        grid=(n_b // n_seq,),
        in_specs=[pl.BlockSpec((n_seq, Q_PAD, 4 * LANES), blk),
                  pl.BlockSpec((n_seq, n_c, LANES), blk),
                  pl.BlockSpec((n_seq, n_c, 2 * LANES), blk),
                  pl.BlockSpec((n_c, N_SEL_LANES), lambda b: (0, 0))],
        out_specs=[pl.BlockSpec((n_seq, SAMPLE_ROWS, LANES), blk),
                   pl.BlockSpec((n_seq, rows_seq, N_SEL_LANES), blk)],
        out_shape=[jax.ShapeDtypeStruct((n_b, SAMPLE_ROWS, LANES), F32),
                   jax.ShapeDtypeStruct((n_b, rows_seq, N_SEL_LANES), F32)],
        scratch_shapes=[pltpu.VMEM((n_seq * rows_seq, N_SEL_LANES), F32),
                        pltpu.VMEM((n_seq * rows_seq, N_SEL_LANES), F32)],
        compiler_params=pltpu.CompilerParams(dimension_semantics=("arbitrary",),
                                             vmem_limit_bytes=VMEM_LIMIT),
        name="cmp_sample",
    )(q8, ck, cv, selmap)


def _pad_keys(x):
    return jnp.concatenate([x, jnp.zeros((LANES - Q_PAD, x.shape[1]), F32)], axis=0)


def _value_variants(v, axis):
    first = lax.broadcasted_iota(jnp.int32, v.shape, axis) < HEAD_DIM
    one = jnp.ones_like(v)
    return jnp.where(first, v, one).astype(BF16), jnp.where(first, one, v).astype(BF16)


def _stack_sb_queries(qs):
    lane8 = lax.broadcasted_iota(jnp.int32, (Q_PAD, LANES), 1)
    low8 = lane8 < HEAD_DIM
    parts = []
    for k in range(SB_KV_HEADS):
        for r in range(SB_HEADS // SB_KV_HEADS):
            ch = 2 * (k // 2) + r
            keep = low8 if k % 2 == 0 else jnp.logical_not(low8)
            piece = jnp.where(keep, qs[:, ch * LANES:(ch + 1) * LANES], 0.0)
            zero = jnp.zeros_like(piece)
            parts.append(jnp.concatenate([piece, zero] if k // 2 == 0 else [zero, piece], axis=1))
    return jnp.concatenate(parts, axis=0).astype(BF16)


def _sb_pages_sweep(qb, pages, carry, accb, u):
    zs, sps, sums = [], [], []
    for pg in pages:
        z = _dot(qb, pg[0, 0:2 * LANES, :].astype(BF16))
        sp = _softplus(z)
        zs.append(z)
        sps.append(sp)
        sums.append(jnp.sum(-sp, axis=1, keepdims=True))
    for pi in range(len(pages) - 1, -1, -1):
        a = jnp.exp((zs[pi] - sps[pi]) + _dot_split(-sps[pi], u) + carry)
        accb = accb + _dot_nt(a.astype(BF16), pages[pi][0, 2 * LANES:4 * LANES, :].astype(BF16))
        carry = carry + sums[pi]
    return carry, accb


def _attn_sample_kernel(*refs, past_len, dec_seq, n_slots, n_near, n_list):
    table_ref, plist_ref, pcount_ref = refs[:3]
    refs = refs[3:]
    (qa_ref, qs_ref, gate_ref, ocmp_ref, selblk_ref, slcn_ref, winn_ref, sbn_ref, state_ref, u_ref) = refs[:10]
    slc_pages = refs[10:10 + n_slots]
    sb_pages = refs[10 + n_slots:10 + n_slots + n_near]
    oa_ref, accb_ref, carry_ref = refs[10 + n_slots + n_near:13 + n_slots + n_near]
    qa_s, m_s, acc_s, win_s = refs[13 + n_slots + n_near:]

    b = pl.program_id(0)
    j = pl.program_id(1)
    n_steps = pl.num_programs(1)
    n_rows = SAMPLE_ROWS
    half = n_rows // 2
    lane = lax.broadcasted_iota(jnp.int32, (n_rows, LANES), 1)
    low = lane < HEAD_DIM
    t_q = _row_ids((n_rows, LANES))[1]
    qpos = past_len + t_q
    slope = _head_slopes((n_rows, LANES))

    @pl.when(j == 0)
    def _():
        qa = _stack_nsa_queries(qa_ref[0])
        qa_s[...] = qa
        qb = _stack_sb_queries(qs_ref[0])
        u = u_ref[...]

        new_valid = lane < dec_seq
        slcn = _pad_keys(slcn_ref[0])
        smask = jnp.logical_and(new_valid, lane <= t_q)
        s = _dot_nt(qa, slcn[:, 0:LANES].astype(BF16)) - slope * (t_q - lane).astype(F32)
        s = jnp.where(smask, s, -BIG)
        m0 = jnp.max(s, axis=1, keepdims=True)
        p = jnp.where(smask, jnp.exp(s - m0), 0.0).astype(BF16)
        v0, v1 = _value_variants(slcn[:, LANES:2 * LANES], 1)
        m_s[...] = jnp.broadcast_to(m0, (n_rows, LANES))
        acc_s[0:half, :] = _dot(p[0:half], v0)
        acc_s[half:n_rows, :] = _dot(p[half:n_rows], v1)

        sbn = _pad_keys(sbn_ref[0])
        z = _dot_nt(qb, sbn[:, 0:2 * LANES].astype(BF16))
        a, tile_sum = _sb_weights(z, jnp.logical_and(new_valid, lane < t_q), 0.0, u)
        accb = _dot(a.astype(BF16), sbn[:, 2 * LANES:4 * LANES].astype(BF16))
        carry, accb = _sb_pages_sweep(qb, sb_pages, jnp.broadcast_to(tile_sum, (n_rows, LANES)), accb, u)
        accb_ref[0] = accb
        carry_ref[0] = carry

        n_state = state_ref.shape[2]
        winn = _pad_keys(winn_ref[0])
        n_k = n_state + LANES
        col = lax.broadcasted_iota(jnp.int32, (n_rows, n_k), 1)
        kpos = jnp.where(col < n_state, past_len - n_state + col, past_len + col - n_state)
        rel = past_len + _row_ids((n_rows, n_k))[1] - kpos
        wmask = jnp.logical_and(jnp.logical_and(rel >= 0, rel < WINDOW), col < n_state + dec_seq)
        qk = jnp.concatenate([_dot(qa, state_ref[0, 0:LANES, :].astype(BF16)),
                              _dot_nt(qa, winn[:, 0:LANES].astype(BF16))], axis=1)
        s = qk - _head_slopes((n_rows, n_k)) * rel.astype(F32)
        s = jnp.where(wmask, s, -BIG)
        m = jnp.max(s, axis=1, keepdims=True)
        e = jnp.where(wmask, jnp.exp(s - m), 0.0)
        p = (e / jnp.maximum(jnp.sum(e, axis=1, keepdims=True), 1e-30)).astype(BF16)
        win_s[...] = (_dot_nt(p[:, 0:n_state], state_ref[0, LANES:2 * LANES, :].astype(BF16))
                      + _dot(p[:, n_state:n_k], winn[:, LANES:2 * LANES].astype(BF16)))

    n_used = pcount_ref[b]

    @pl.when(j * n_slots < n_used)
    def _():
        qa = qa_s[...]
        sel = selblk_ref[0].astype(BF16)
        blk_row = lax.broadcasted_iota(jnp.int32, (N_SEL_LANES, LANES), 0)
        blk_off = lax.broadcasted_iota(jnp.int32, (N_SEL_LANES, LANES), 1) >> SEL_SHIFT
        ss = []
        for pi in range(n_slots):
            slot = j * n_slots + pi
            page = plist_ref[b * n_list + slot]
            expand = jnp.where(blk_row == (LANES // SEL_BLOCK) * page + blk_off, 1.0, 0.0).astype(BF16)
            bias16 = jnp.where(slot < n_used, (_dot(sel, expand) - 1.0) * BIG, -BIG)
            bias = jnp.concatenate(
                [bias16[0:Q_PAD]] * HEADS_PER_GROUP + [bias16[Q_PAD:2 * Q_PAD]] * HEADS_PER_GROUP, axis=0)
            kpos = page * LANES + lane
            qk = _dot(qa, slc_pages[pi][0, 0:LANES, :].astype(BF16))
            ss.append((qk - slope * (qpos - kpos).astype(F32)) + bias)
        m_old = m_s[...]
        m_new = m_old
        for s in ss:
            m_new = jnp.maximum(m_new, jnp.max(s, axis=1, keepdims=True))
        pv0 = jnp.zeros((half, LANES), F32)
        pv1 = jnp.zeros((half, LANES), F32)
        for pi in range(n_slots):
            p = jnp.exp(ss[pi] - m_new).astype(BF16)
            v0, v1 = _value_variants(slc_pages[pi][0, LANES:2 * LANES, :], 0)
            pv0 = pv0 + _dot_nt(p[0:half], v0)
            pv1 = pv1 + _dot_nt(p[half:n_rows], v1)
        alpha = jnp.exp(m_old - m_new)
        acc_s[0:half, :] = alpha[0:half] * acc_s[0:half, :] + pv0
        acc_s[half:n_rows, :] = alpha[half:n_rows] * acc_s[half:n_rows, :] + pv1
        m_s[...] = m_new

    @pl.when(j == n_steps - 1)
    def _():
        o_slc = _normalise(acc_s[...])
        o_win = win_s[...]
        o_cmp = ocmp_ref[0]
        for c in range(HEADS_PER_GROUP):
            tot = []
            for g in range(NSA_KV_GROUPS):
                h = g * HEADS_PER_GROUP + c
                r0 = h * Q_PAD
                gc = gate_ref[0, :, h:h + 1]
                gs = gate_ref[0, :, NSA_HEADS + h:NSA_HEADS + h + 1]
                gw = gate_ref[0, :, 2 * NSA_HEADS + h:2 * NSA_HEADS + h + 1]
                tot.append(gc * o_cmp[r0:r0 + Q_PAD] + gs * o_slc[r0:r0 + Q_PAD] + gw * o_win[r0:r0 + Q_PAD])
            oa_ref[0, :, c * LANES:(c + 1) * LANES] = jnp.where(low[0:Q_PAD], tot[0], tot[1])


def _sb_rest_kernel(*refs, n_chunks, chunk):
    need_ref = refs[1]
    refs = refs[2:]
    qs_ref, accb_ref, carry_ref, u_ref = refs[:4]
    pages = refs[4:4 + n_chunks * chunk]
    ob_ref = refs[4 + n_chunks * chunk]
    accb_s, carry_s = refs[5 + n_chunks * chunk:]
    accb_s[...] = accb_ref[0]
    carry_s[...] = carry_ref[0]

    @pl.when(need_ref[pl.program_id(0)] > 0)
    def _():
        qb = _stack_sb_queries(qs_ref[0])
        u = u_ref[...]
        for c in range(n_chunks - 1, -1, -1):
            @pl.when(jnp.max(carry_s[...]) > EXP_UNDERFLOW)
            def _():
                carry, accb = _sb_pages_sweep(qb, pages[c * chunk:(c + 1) * chunk], carry_s[...], accb_s[...], u)
                carry_s[...] = carry
                accb_s[...] = accb

    low = lax.broadcasted_iota(jnp.int32, (Q_PAD, LANES), 1) < HEAD_DIM
    acc_b = accb_s[...]
    for kp in range(SB_KV_HEADS // 2):
        for r in range(SB_HEADS // SB_KV_HEADS):
            ra = ((2 * kp) * 2 + r) * Q_PAD
            rb = ((2 * kp + 1) * 2 + r) * Q_PAD
            a0 = acc_b[ra:ra + Q_PAD, kp * LANES:(kp + 1) * LANES]
            a1 = acc_b[rb:rb + Q_PAD, kp * LANES:(kp + 1) * LANES]
            ch = 2 * kp + r
            ob_ref[0, :, ch * LANES:(ch + 1) * LANES] = jnp.where(low, a0, a1)


def _page_schedule(selblk, n_pages, n_list, n_slots):
    n_b = selblk.shape[0]
    per_page = LANES // SEL_BLOCK
    used = jnp.max(selblk[:, :, :n_pages * per_page].reshape(n_b, -1, n_pages, per_page), axis=(1, 3)) > 0
    page_id = jnp.arange(n_pages, dtype=jnp.int32)
    order = -jnp.sort(-jnp.where(used, page_id[None, :], -1), axis=-1)[:, :n_list]
    order = jnp.pad(order, ((0, 0), (0, n_list - order.shape[1])), constant_values=-1)
    steps = [order[:, 0:n_slots]]
    steps[0] = jnp.where(steps[0] < 0, order[:, 0:1], steps[0])
    for j in range(1, n_list // n_slots):
        cur = order[:, j * n_slots:(j + 1) * n_slots]
        steps.append(jnp.where(cur < 0, steps[j - 1], cur))
    plist = jnp.maximum(jnp.concatenate(steps, axis=1), 0)
    return plist.reshape(-1).astype(jnp.int32), jnp.minimum(jnp.sum(used, axis=-1), n_list).astype(jnp.int32)


def _attn_sample(table, plist, pcount, qa8, qs8, gate8, ocmp, selblk, slcn, winn, sbn, state, u,
                 slc_pool, sb_pool, n_pages, n_list, n_slots, n_near, past_len, dec_seq):
    n_b = qa8.shape[0]
    n_rows = SAMPLE_ROWS
    n_state = state.shape[2]
    kern = functools.partial(_attn_sample_kernel, past_len=past_len, dec_seq=dec_seq,
                             n_slots=n_slots, n_near=n_near, n_list=n_list)
    per_b = lambda b, j, *_: (b, 0, 0)

    def slc_spec(p):
        return pl.BlockSpec((1, 2 * LANES, LANES),
                            lambda b, j, t, pls, pc: (t[b * n_pages + pls[b * n_list + j * n_slots + p]], 0, 0))

    def near_spec(p):
        return pl.BlockSpec((1, 4 * LANES, LANES), lambda b, j, t, pls, pc: (t[b * n_pages + n_pages - n_near + p], 0, 0))

    grid_spec = pltpu.PrefetchScalarGridSpec(
        num_scalar_prefetch=3,
        grid=(n_b, n_list // n_slots),
        in_specs=[pl.BlockSpec((1, Q_PAD, 4 * LANES), per_b),
                  pl.BlockSpec((1, Q_PAD, 4 * LANES), per_b),
                  pl.BlockSpec((1, Q_PAD, LANES), per_b),
                  pl.BlockSpec((1, n_rows, LANES), per_b),
                  pl.BlockSpec((1, 2 * Q_PAD, N_SEL_LANES), per_b),
                  pl.BlockSpec((1, Q_PAD, 2 * LANES), per_b),
                  pl.BlockSpec((1, Q_PAD, 2 * LANES), per_b),
                  pl.BlockSpec((1, Q_PAD, 4 * LANES), per_b),
                  pl.BlockSpec((1, 2 * LANES, n_state), per_b),
                  pl.BlockSpec((LANES, LANES), lambda b, j, *_: (0, 0))]
        + [slc_spec(p) for p in range(n_slots)]
        + [near_spec(p) for p in range(n_near)],
        out_specs=[pl.BlockSpec((1, Q_PAD, 4 * LANES), per_b),
                   pl.BlockSpec((1, n_rows, 2 * LANES), per_b),
                   pl.BlockSpec((1, n_rows, LANES), per_b)],
        scratch_shapes=[pltpu.VMEM((n_rows, LANES), BF16),
                        pltpu.VMEM((n_rows, LANES), F32),
                        pltpu.VMEM((n_rows, LANES), F32),
                        pltpu.VMEM((n_rows, LANES), F32)],
    )
    return pl.pallas_call(
        kern,
        grid_spec=grid_spec,
        out_shape=[jax.ShapeDtypeStruct((n_b, Q_PAD, 4 * LANES), F32),
                   jax.ShapeDtypeStruct((n_b, n_rows, 2 * LANES), F32),
                   jax.ShapeDtypeStruct((n_b, n_rows, LANES), F32)],
        compiler_params=pltpu.CompilerParams(dimension_semantics=("arbitrary", "arbitrary"),
                                             vmem_limit_bytes=VMEM_LIMIT),
        name="attn_sample",
    )(table, plist, pcount, qa8, qs8, gate8, ocmp, selblk, slcn, winn, sbn, state, u,
      *([slc_pool] * n_slots), *([sb_pool] * n_near))


def _sb_rest(table, need, qs8, accb, carry, u, sb_pool, n_pages, n_rest, chunk):
    n_b = qs8.shape[0]
    n_rows = SAMPLE_ROWS
    kern = functools.partial(_sb_rest_kernel, n_chunks=n_rest // chunk, chunk=chunk)
    per_b = lambda b, *_: (b, 0, 0)

    def page_spec(p):
        return pl.BlockSpec((1, 4 * LANES, LANES),
                            lambda b, t, nd: (t[jnp.where(nd[b] > 0, b * n_pages + p, 0)], 0, 0))

    grid_spec = pltpu.PrefetchScalarGridSpec(
        num_scalar_prefetch=2,
        grid=(n_b,),
        in_specs=[pl.BlockSpec((1, Q_PAD, 4 * LANES), per_b),
                  pl.BlockSpec((1, n_rows, 2 * LANES), per_b),
                  pl.BlockSpec((1, n_rows, LANES), per_b),
                  pl.BlockSpec((LANES, LANES), lambda b, *_: (0, 0))]
        + [page_spec(p) for p in range(n_rest)],
        out_specs=pl.BlockSpec((1, Q_PAD, 4 * LANES), per_b),
        scratch_shapes=[pltpu.VMEM((n_rows, 2 * LANES), F32),
                        pltpu.VMEM((n_rows, LANES), F32)],
    )
    return pl.pallas_call(
        kern,
        grid_spec=grid_spec,
        out_shape=jax.ShapeDtypeStruct((n_b, Q_PAD, 4 * LANES), F32),
        compiler_params=pltpu.CompilerParams(dimension_semantics=("arbitrary",),
                                             vmem_limit_bytes=VMEM_LIMIT),
        name="sb_rest",
    )(table, need, qs8, accb, carry, u, *([sb_pool] * n_rest))


def _head_index(base, perm):
    return np.concatenate([np.arange(base + h * HEAD_DIM, base + (h + 1) * HEAD_DIM) for h in perm])


def _largest_divisor(n, cap):
    d = min(n, cap)
    while n % d:
        d -= 1
    return d


def kernel(x_prompt, x_sample, cache_cmp_kv, cache_slc_kv, cache_sb_kv, state_win_kv, page_table,
           norm_g, w_in, q_norm_g, k_norm_g, cmp_pe, cmp_w1, cmp_b1, cmp_w2, w_out):
    n_b, t, d = x_prompt.shape
    n_db, dec_seq, _ = x_sample.shape
    depth = w_in.shape[0]
    assert depth == 1, "single-layer step"
    n_pages = page_table.shape[1]
    page = cache_cmp_kv.shape[2]
    past_len = n_pages * page
    n_state = state_win_kv.shape[2]
    assert page == LANES and t % LANES == 0 and t // SEL_BLOCK <= N_SEL_LANES
    assert past_len // SEL_BLOCK <= N_SEL_LANES and dec_seq <= Q_PAD and dec_seq <= SEL_BLOCK
    assert n_state == WINDOW and past_len >= WINDOW and t >= WINDOW

    nsa_w = NSA_HEADS * HEAD_DIM
    kvw = NSA_KV_GROUPS * HEAD_DIM
    sb_w = SB_HEADS * HEAD_DIM
    sbkv_w = SB_KV_HEADS * HEAD_DIM
    o_kv = nsa_w
    o_gt = o_kv + 6 * kvw
    o_za = o_gt + N_BRANCH * NSA_HEADS
    o_qs = o_za + nsa_w
    o_kb = o_qs + sb_w
    o_zb = o_kb + 2 * sbkv_w
    w0 = w_in[0]
    cols = np.concatenate([
        _head_index(0, PERM_A), np.arange(o_kv, o_gt), _head_index(o_za, PERM_A),
        _head_index(o_qs, PERM_B), np.arange(o_kb, o_zb), _head_index(o_zb, PERM_B),
        np.arange(o_gt, o_za)])
    w_perm = jnp.pad(w0[:, cols], ((0, 0), (0, LANES - N_BRANCH * NSA_HEADS))).astype(BF16)
    rows_out = np.concatenate([_head_index(0, PERM_A), _head_index(nsa_w, PERM_B)])
    w_out_perm = w_out[0][rows_out, :].astype(BF16)

    seg = jnp.asarray(np.kron(np.eye(2), np.ones((HEAD_DIM, HEAD_DIM))), BF16)
    tile2 = lambda v: jnp.tile(v, 2).reshape(1, LANES)
    gq = tile2(q_norm_g[0])
    gkc, gks, gkw = tile2(k_norm_g[0, 0]), tile2(k_norm_g[0, 1]), tile2(k_norm_g[0, 2])
    g_in = norm_g[0].reshape(1, d)

    w1 = cmp_w1[0].reshape(2, 2, CMP_STRIDE, HEAD_DIM, CMP_HIDDEN)
    w1 = jnp.transpose(w1, (0, 2, 3, 1, 4))
    zeros = jnp.zeros_like(w1)
    wc = jnp.stack([jnp.concatenate([w1, zeros], axis=3), jnp.concatenate([zeros, w1], axis=3)], axis=2)
    wc = wc.reshape(2, CMP_STRIDE // 2, 2 * LANES, 4 * CMP_HIDDEN).astype(BF16)
    w2 = cmp_w2[0]
    z2 = jnp.zeros_like(w2)
    w2p = jnp.stack([jnp.concatenate([w2, z2], axis=2), jnp.concatenate([z2, w2], axis=2)], axis=1).astype(BF16)
    cb = _cbias(cmp_pe[0].reshape(2, 1, CMP_LEN * HEAD_DIM), cmp_w1[0],
                cmp_b1[0].reshape(2, 1, CMP_HIDDEN))

    def selection_map(n_rows):
        r = np.arange(n_rows)[:, None]
        jb = np.arange(N_SEL_LANES)[None, :]
        start = (r - 1) * CMP_STRIDE
        ok = (r >= 1) & (start < jb * SEL_BLOCK + SEL_BLOCK) & (start + CMP_LEN > jb * SEL_BLOCK)
        return jnp.asarray(ok, BF16)

    u = jnp.asarray(np.arange(LANES)[:, None] > np.arange(LANES)[None, :], BF16)

    tm = _largest_divisor(n_b * t, 256)
    xp2 = x_prompt.reshape(n_b * t, d)
    (q16, cmpkv, slckv, winkv, slck, slcv, wink, winv, gates, sza, qs16, sbkv, sbk, sbv, szb) = _proj(
        xp2, g_in, w_perm, gq, gks, gkw, seg, tm)
    n_sub = t // CMP_STRIDE
    sub_w = CMP_STRIDE * 2 * kvw
    ck, cv = _compress_prompt(cmpkv.reshape(n_b * n_sub, sub_w), wc, cb, w2p, gkc, seg,
                              n_b, _largest_divisor(n_sub, 64))
    r3 = lambda a: a.reshape(n_b, t, a.shape[-1])
    oc, sel = _cmp_prompt(r3(q16), r3(gates), ck.reshape(n_b, n_sub, LANES), cv.reshape(n_b, n_sub, 2 * LANES),
                          selection_map(n_sub), _largest_divisor(t, CMP_Q_ROWS))
    oa = _slcwin_prompt(r3(q16), r3(gates), oc, sel, r3(slck), r3(slcv), r3(wink), r3(winv))
    ob = _sb_prompt(r3(qs16), r3(sbk), r3(sbv), u)
    y_prompt = _final(xp2, oa.reshape(n_b * t, nsa_w), sza, ob.reshape(n_b * t, sb_w), szb,
                      w_out_perm, tm).reshape(n_b, t, d)
    kv5 = lambda a, n, heads: a.reshape(1, n, -1, 2, heads, HEAD_DIM)
    cmp_p = kv5(cmpkv, n_b, NSA_KV_GROUPS)
    slc_p = kv5(slckv, n_b, NSA_KV_GROUPS)
    sb_p = kv5(sbkv, n_b, SB_KV_HEADS)
    win_p = kv5(winkv, n_b, NSA_KV_GROUPS)[:, :, t - min(WINDOW, t):]

    rows_s = n_db * dec_seq
    tms = _largest_divisor(rows_s, 256)
    xs2 = x_sample.reshape(rows_s, d)
    (q16s, cmpkv_s, slckv_s, winkv_s, _, _, _, _, gates_s, sza_s, qs16s, sbkv_s, _, _, szb_s) = _proj(
        xs2, g_in, w_perm, gq, gks, gkw, seg, tms)
    table = page_table.reshape(-1).astype(jnp.int32)
    n_sub_s = n_pages * (page // CMP_STRIDE)
    chan_major = lambda a: jnp.transpose(a, (0, 2, 3, 4, 1)).reshape(a.shape[0], -1, a.shape[1])
    cks, cvs = _compress_sample(table, chan_major(cache_cmp_kv[0]), wc, cb, w2p, gkc, seg,
                                n_db, n_pages, _largest_divisor(n_pages, COMPRESS_PAGES))
    pad8 = lambda a: jnp.pad(a.reshape(n_db, dec_seq, a.shape[-1]).astype(F32),
                             ((0, 0), (0, Q_PAD - dec_seq), (0, 0)))
    qa8, qs8, gate8 = pad8(q16s), pad8(qs16s), pad8(gates_s)
    n_blocks = -(-(past_len + dec_seq) // SEL_BLOCK)
    n_pick = min(SEL_TOPK, n_blocks) - 1
    ocmp, selblk = _cmp_sample(qa8, cks.reshape(n_db, n_sub_s, LANES), cvs.reshape(n_db, n_sub_s, 2 * LANES),
                               selection_map(n_sub_s), past_len, n_pick, _largest_divisor(n_db, CMP_SAMPLE_SEQS))
    n_slots = min(SLC_SLOTS, n_pages)
    n_list = -(-min(n_pages, NSA_KV_GROUPS * n_pick) // n_slots) * n_slots
    plist, pcount = _page_schedule(selblk, n_pages, n_list, n_slots)
    n_near = min(ATTN_PAGES, n_pages)
    sb_pool = chan_major(cache_sb_kv[0])
    oa8, accb, carry = _attn_sample(table, plist, pcount, qa8, qs8, gate8, ocmp, selblk,
                                    pad8(slckv_s), pad8(winkv_s), pad8(sbkv_s),
                                    chan_major(state_win_kv[0]), u, chan_major(cache_slc_kv[0]), sb_pool,
                                    n_pages, n_list, n_slots, n_near, past_len, dec_seq)
    need = (jnp.max(carry, axis=(1, 2)) > EXP_UNDERFLOW).astype(jnp.int32)
    n_rest = n_pages - n_near
    ob8 = _sb_rest(table, need, qs8, accb, carry, u, sb_pool, n_pages, n_rest,
                   _largest_divisor(n_rest, ATTN_PAGES) if n_rest else 1)
    y_sample = _final(xs2, oa8[:, :dec_seq].reshape(rows_s, nsa_w), sza_s,
                      ob8[:, :dec_seq].reshape(rows_s, sb_w), szb_s, w_out_perm, tms).reshape(n_db, dec_seq, d)
    kv5s = lambda a, heads: a.reshape(1, n_db, dec_seq, 2, heads, HEAD_DIM)
    cmp_s = kv5s(cmpkv_s, NSA_KV_GROUPS)
    slc_s = kv5s(slckv_s, NSA_KV_GROUPS)
    sb_s = kv5s(sbkv_s, SB_KV_HEADS)
    win_new = kv5s(winkv_s, NSA_KV_GROUPS)
    win_s = jnp.concatenate([state_win_kv[:, :, dec_seq:], win_new], axis=2)

    return (y_prompt, y_sample, cmp_p, cmp_s, slc_p, slc_s, sb_p, sb_s, win_p, win_s)
```
